```python
import math
import jax, jax.numpy as jnp
from jax import lax
import numpy as np

D_MODEL = 1024
BATCH = 4
SEQ = 4096
DEPTH = 4

GRID_W = 64
CTX_LEN = 256
CONV_WIDTH = 3
D_CONV = 512
N_DIFF_HEADS = 4
DIFF_QK_DIM = 64
DIFF_V_DIM = 128
D_ATTN = N_DIFF_HEADS * DIFF_V_DIM
N_FOURIER_GROUPS = 8
FOURIER_GROUP = D_MODEL // N_FOURIER_GROUPS
D_FF = 4 * D_MODEL
Q_BLOCK = 128
ROPE_BASE = 10000.0
LN_EPS = 1e-6
SUBLN_EPS = 1e-5
DEEPNORM_ALPHA = (2.0 * DEPTH) ** 0.25
DEEPNORM_BETA = (8.0 * DEPTH) ** -0.25
N_EVEN = (DEPTH + 1) // 2
N_ODD = DEPTH // 2
COL_CONV = 3 * D_CONV
COL_Q = N_DIFF_HEADS * 2 * DIFF_QK_DIM
COL_K = N_DIFF_HEADS * 2 * DIFF_QK_DIM
COL_V = D_ATTN
Q0 = COL_CONV
K0 = Q0 + COL_Q
V0 = K0 + COL_K
D_IN = V0 + COL_V

kernel_name = 'hybrid_conv_diffattn_fourier_dit'


def layer_norm(x):
    xf = x.astype(jnp.float32)
    mu = jnp.mean(xf, axis=-1, keepdims=True)
    var = jnp.mean(jnp.square(xf - mu), axis=-1, keepdims=True)
    return ((xf - mu) * lax.rsqrt(var + LN_EPS)).astype(x.dtype)


def layer_norm_affine(x, g, b):
    xf = x.astype(jnp.float32)
    mu = jnp.mean(xf, axis=-1, keepdims=True)
    var = jnp.mean(jnp.square(xf - mu), axis=-1, keepdims=True)
    y = (xf - mu) * lax.rsqrt(var + LN_EPS) * g.astype(jnp.float32) + b.astype(jnp.float32)
    return y.astype(x.dtype)


def modulation(cond, w, b):
    return jnp.split(jax.nn.silu(cond) @ w + b, 6, axis=-1)


def modulate(h, shift, scale):
    return layer_norm(h) * (1.0 + scale) + shift


def post_norm_residual(h, y, gate, g, b):
    return layer_norm_affine(DEEPNORM_ALPHA * h + gate * y, g, b)


def rope_1d(x, pos):
    half = x.shape[-1] // 2
    inv = ROPE_BASE ** (-jnp.arange(half, dtype=jnp.float32) / half)
    ang = pos.astype(jnp.float32)[:, None] * inv[None, :]
    cos = jnp.cos(ang)[:, None, None, :].astype(x.dtype)
    sin = jnp.sin(ang)[:, None, None, :].astype(x.dtype)
    x1, x2 = x[..., :half], x[..., half:]
    return jnp.concatenate([x1 * cos - x2 * sin, x1 * sin + x2 * cos], axis=-1)


def rope_2d(x, row, col):
    d = x.shape[-1] // 2
    return jnp.concatenate([rope_1d(x[..., :d], row), rope_1d(x[..., d:], col)], axis=-1)


def short_conv(u, w):
    n = u.shape[1]
    pad = CONV_WIDTH // 2
    up = jnp.pad(u, ((0, 0), (pad, pad), (0, 0)))
    return sum(up[:, t:t + n] * w[t] for t in range(CONV_WIDTH))


def conv_mixer(p, w):
    gb, gc, v = jnp.split(p, 3, axis=-1)
    return gb * short_conv(gc * v, w)


def heads_qk(p):
    return p.reshape(p.shape[0], p.shape[1], N_DIFF_HEADS, 2, DIFF_QK_DIM)


def heads_v(p):
    return p.reshape(p.shape[0], p.shape[1], N_DIFF_HEADS, DIFF_V_DIM)


def diff_attend(q, k, v, lam):
    s = jnp.einsum('bqhcd,bkhcd->bhcqk', q, k).astype(jnp.float32) * (DIFF_QK_DIM ** -0.5)
    p = jax.nn.softmax(s, axis=-1)
    pd = (p[:, :, 0] - lam * p[:, :, 1]).astype(v.dtype)
    return jnp.einsum('bhqk,bkhd->bqhd', pd, v)


def diff_attention_blocks(q, k, v, lam):
    b, n = q.shape[0], q.shape[1]
    nb = n // Q_BLOCK
    qb = q.reshape(b, nb, Q_BLOCK, N_DIFF_HEADS, 2, DIFF_QK_DIM).transpose(1, 0, 2, 3, 4, 5)
    out = lax.map(lambda blk: diff_attend(blk, k, v, lam), qb)
    return out.transpose(1, 0, 2, 3, 4).reshape(b, n, N_DIFF_HEADS, DIFF_V_DIM)


def sub_ln(o, g, lam_init):
    of = o.astype(jnp.float32)
    y = of * lax.rsqrt(jnp.mean(of * of, axis=-1, keepdims=True) + SUBLN_EPS)
    y = y * g.astype(jnp.float32) * (1.0 - lam_init)
    return y.reshape(o.shape[0], o.shape[1], D_ATTN).astype(o.dtype)


def fourier_mix(u):
    b, n, _ = u.shape
    ug = u.astype(jnp.float32).reshape(b, n, N_FOURIER_GROUPS, FOURIER_GROUP)
    f = jnp.fft.fft2(ug, axes=(1, 3), norm='ortho').real
    return f.reshape(b, n, D_MODEL).astype(u.dtype)


def sq_relu_mlp(u, w1, w2):
    return jnp.square(jax.nn.relu(u @ w1)) @ w2


def setup_inputs(seed: int = 0) -> dict:
    key = jax.random.key(seed)
    ks = jax.random.split(key, 16)
    f32 = jnp.float32

    def nrm(k, shape, s):
        return s * jax.random.normal(k, shape, f32)

    return {
        'x': nrm(ks[0], (BATCH, SEQ, D_MODEL), 1.0),
        'c': nrm(ks[1], (BATCH, D_MODEL), 1.0),
        'ctx': nrm(ks[2], (BATCH, CTX_LEN, D_MODEL), 1.0),
        'c_ctx': nrm(ks[3], (D_MODEL,), 1.0),
        'ada_w': nrm(ks[4], (DEPTH, D_MODEL, 6 * D_MODEL), 0.5 * D_MODEL ** -0.5),
        'ada_b': nrm(ks[5], (DEPTH, 6 * D_MODEL), 0.01),
        'ln_g': 1.0 + nrm(ks[6], (DEPTH, 2, D_MODEL), 0.05),
        'ln_b': nrm(ks[7], (DEPTH, 2, D_MODEL), 0.02),
        'mlp_w1': nrm(ks[8], (DEPTH, D_MODEL, D_FF), D_MODEL ** -0.5),
        'mlp_w2': nrm(ks[9], (DEPTH, D_FF, D_MODEL), DEEPNORM_BETA * D_FF ** -0.5),
        'w_in': nrm(ks[10], (N_EVEN, D_MODEL, D_IN), D_MODEL ** -0.5),
        'conv_w': nrm(ks[11], (N_EVEN, CONV_WIDTH, D_CONV), CONV_WIDTH ** -0.5),
        'lambda_qk': nrm(ks[12], (N_EVEN, 4, DIFF_QK_DIM), 0.1),
        'subln_g': 1.0 + nrm(ks[13], (N_EVEN, DIFF_V_DIM), 0.05),
        'w_out_mix': nrm(ks[14], (N_EVEN, D_CONV + D_ATTN, D_MODEL), DEEPNORM_BETA * (D_CONV + D_ATTN) ** -0.5),
        'w_out_fourier': nrm(ks[15], (N_ODD, D_MODEL, D_MODEL), DEEPNORM_BETA * D_MODEL ** -0.5),
    }


def reference(x, c, ctx, c_ctx, ada_w, ada_b, ln_g, ln_b, mlp_w1, mlp_w2, w_in, conv_w, lambda_qk, subln_g, w_out_mix, w_out_fourier):
    n = x.shape[1]
    rows = n // GRID_W
    row = jnp.repeat(jnp.arange(rows, dtype=jnp.int32), GRID_W)
    col = jnp.tile(jnp.arange(GRID_W, dtype=jnp.int32), rows)
    last_attn = 2 * ((DEPTH - 1) // 2)
    cond_lat = c[:, None, :]
    cond_ctx = c_ctx[None, None, :]
    h, hc = x, ctx
    for i in range(DEPTH):
        j = i // 2
        update_ctx = i < last_attn
        sh1, sc1, g1, sh2, sc2, g2 = modulation(cond_lat, ada_w[i], ada_b[i])
        u = modulate(h, sh1, sc1)
        if update_ctx or i == last_attn:
            mc = modulation(cond_ctx, ada_w[i], ada_b[i])
            uc = modulate(hc, mc[0], mc[1])
        if i % 2 == 0:
            lam_init = 0.8 - 0.6 * math.exp(-0.3 * i)
            lq = lambda_qk[j].astype(jnp.float32)
            lam = jnp.exp(jnp.sum(lq[0] * lq[1])) - jnp.exp(jnp.sum(lq[2] * lq[3])) + lam_init
            p = u @ w_in[j]
            q = rope_2d(heads_qk(p[..., Q0:K0]), row, col)
            k = rope_2d(heads_qk(p[..., K0:V0]), row, col)
            v = heads_v(p[..., V0:])
            if update_ctx:
                pc = uc @ w_in[j]
                kvc = pc[..., K0:]
            else:
                kvc = uc @ w_in[j][:, K0:]
            kc = heads_qk(kvc[..., :COL_K])
            vc = heads_v(kvc[..., COL_K:])
            k_all = jnp.concatenate([k, kc], axis=1)
            v_all = jnp.concatenate([v, vc], axis=1)
            o = diff_attention_blocks(q, k_all, v_all, lam)
            y = jnp.concatenate([conv_mixer(p[..., :COL_CONV], conv_w[j]), sub_ln(o, subln_g[j], lam_init)], axis=-1) @ w_out_mix[j]
            if update_ctx:
                oc = diff_attend(heads_qk(pc[..., Q0:K0]), kc, vc, lam)
                yc = jnp.concatenate([conv_mixer(pc[..., :COL_CONV], conv_w[j]), sub_ln(oc, subln_g[j], lam_init)], axis=-1) @ w_out_mix[j]
        else:
            y = fourier_mix(u) @ w_out_fourier[j]
            if update_ctx:
                yc = fourier_mix(uc) @ w_out_fourier[j]
        h = post_norm_residual(h, y, g1, ln_g[i, 0], ln_b[i, 0])
        h = post_norm_residual(h, sq_relu_mlp(modulate(h, sh2, sc2), mlp_w1[i], mlp_w2[i]), g2, ln_g[i, 1], ln_b[i, 1])
        if update_ctx:
            hc = post_norm_residual(hc, yc, mc[2], ln_g[i, 0], ln_b[i, 0])
            hc = post_norm_residual(hc, sq_relu_mlp(modulate(hc, mc[3], mc[4]), mlp_w1[i], mlp_w2[i]), mc[5], ln_g[i, 1], ln_b[i, 1])
    return h
```

```python
import functools
import math

import jax
import jax.numpy as jnp
import numpy as np
from jax import lax
from jax.experimental import pallas as pl
from jax.experimental.pallas import tpu as pltpu

F32 = jnp.float32
BF16 = jnp.bfloat16

GRID_W = 64
CONV_WIDTH = 3
D_CONV = 512
N_DIFF_HEADS = 4
DIFF_QK_DIM = 64
DIFF_V_DIM = 128
D_ATTN = N_DIFF_HEADS * DIFF_V_DIM
FOURIER_GROUP = 128
ROPE_BASE = 10000.0
LN_EPS = 1e-6
SUBLN_EPS = 1e-5
COL_CONV = 3 * D_CONV
COL_QK = N_DIFF_HEADS * 2 * DIFF_QK_DIM
Q0 = COL_CONV
K0 = Q0 + COL_QK
V0 = K0 + COL_QK

LANES = 128
SUBLANES = 8
VMEM_LIMIT = 56 * 1024 * 1024


def _params(n_axes):
    return pltpu.CompilerParams(
        dimension_semantics=("arbitrary",) * n_axes, vmem_limit_bytes=VMEM_LIMIT)


def _const_spec(shape):
    nd = len(shape)
    return pl.BlockSpec(shape, lambda *_: (0,) * nd, pipeline_mode=pl.Buffered(1))


def _layer_norm(x):
    mu = jnp.mean(x, axis=-1, keepdims=True)
    xc = x - mu
    var = jnp.mean(xc * xc, axis=-1, keepdims=True)
    return xc * lax.rsqrt(var + LN_EPS)


def _modulated(x, shift_ref, scale_ref):
    return (_layer_norm(x) * (1.0 + scale_ref[0]) + shift_ref[0]).astype(BF16)


def _post_norm(h, y, gate_ref, g_ref, b_ref, alpha):
    return _layer_norm(alpha * h + gate_ref[0] * y) * g_ref[...] + b_ref[...]


def _mod_kernel(cond_ref, w_ref, b_ref, o_ref):
    c = cond_ref[...]
    a = (c * (1.0 / (1.0 + jnp.exp(-c)))).astype(BF16)
    o_ref[0] = jnp.dot(a, w_ref[0].astype(BF16), preferred_element_type=F32) + b_ref[0]


def _modulation(cond8, ada_w, ada_b):
    depth, d, n6 = ada_w.shape
    tn = 1536
    return pl.pallas_call(
        _mod_kernel,
        grid=(depth, n6 // tn),
        in_specs=[
            pl.BlockSpec((SUBLANES, d), lambda l, j: (0, 0)),
            pl.BlockSpec((1, d, tn), lambda l, j: (l, 0, j)),
            pl.BlockSpec((1, 1, tn), lambda l, j: (l, 0, j)),
        ],
        out_specs=pl.BlockSpec((1, SUBLANES, tn), lambda l, j: (l, 0, j)),
        out_shape=jax.ShapeDtypeStruct((depth, SUBLANES, n6), F32),
        compiler_params=_params(2),
        name="modulation",
    )(cond8, ada_w, ada_b.reshape(depth, 1, n6))


class _Mod:
    def __init__(self, table, d):
        depth = table.shape[0]
        self.rows = table.reshape(depth, SUBLANES, 6, d).transpose(0, 2, 1, 3).reshape(
            depth * 6 * SUBLANES, 1, d)
        self.d = d

    def spec(self, layer, which, row_of_step):
        base = (layer * 6 + which) * SUBLANES
        return pl.BlockSpec((1, 1, self.d), lambda i: (base + row_of_step(i), 0, 0))


def _rope_store(dst_ref, p, cos_ref, sin_lo_ref, sin_hi_ref, scale):
    cos, sin_lo, sin_hi = cos_ref[...], sin_lo_ref[...], sin_hi_ref[...]
    half = DIFF_QK_DIM // 4
    for c in range(p.shape[1] // LANES):
        x = p[:, c * LANES:(c + 1) * LANES]
        ahead = pltpu.roll(x, LANES - half, 1)
        behind = pltpu.roll(x, half, 1)
        r = x * cos + ahead * sin_lo + behind * sin_hi
        if scale != 1.0:
            r = r * scale
        dst_ref[:, c * LANES:(c + 1) * LANES] = r.astype(dst_ref.dtype)


def _inproj_kernel(x_ref, sh_ref, sc_ref, w_ref, cos_ref, sin_lo_ref, sin_hi_ref,
                   gb_ref, g_ref, q_ref, k_ref, v_ref):
    u = _modulated(x_ref[...], sh_ref, sc_ref)
    pc = jnp.dot(u, w_ref[:, 0:COL_CONV], preferred_element_type=F32)
    gb_ref[...] = pc[:, 0:D_CONV]
    g_ref[...] = pc[:, D_CONV:2 * D_CONV] * pc[:, 2 * D_CONV:3 * D_CONV]
    pq = jnp.dot(u, w_ref[:, Q0:K0], preferred_element_type=F32)
    _rope_store(q_ref, pq, cos_ref, sin_lo_ref, sin_hi_ref, DIFF_QK_DIM ** -0.5)
    pk = jnp.dot(u, w_ref[:, K0:V0], preferred_element_type=F32)
    _rope_store(k_ref, pk, cos_ref, sin_lo_ref, sin_hi_ref, 1.0)
    v_ref[...] = jnp.dot(u, w_ref[:, V0:V0 + D_ATTN], preferred_element_type=F32).astype(BF16)


def _inproj(x, mod, layer, row_of_step, w_bf, rope, tm, seq):
    t, d = x.shape
    steps_per_seq = seq // tm
    tok = lambda i: (i, 0)
    pos = lambda i: (i % steps_per_seq, 0)
    tab = pl.BlockSpec((tm, LANES), pos)
    out = lambda n: pl.BlockSpec((tm, n), tok)
    return pl.pallas_call(
        _inproj_kernel,
        grid=(t // tm,),
        in_specs=[pl.BlockSpec((tm, d), tok), mod.spec(layer, 0, row_of_step),
                  mod.spec(layer, 1, row_of_step), _const_spec(w_bf.shape), tab, tab, tab],
        out_specs=[out(D_CONV), out(D_CONV), out(COL_QK), out(COL_QK), out(D_ATTN)],
        out_shape=[jax.ShapeDtypeStruct((t, D_CONV), F32), jax.ShapeDtypeStruct((t, D_CONV), F32),
                   jax.ShapeDtypeStruct((t, COL_QK), BF16), jax.ShapeDtypeStruct((t, COL_QK), BF16),
                   jax.ShapeDtypeStruct((t, D_ATTN), BF16)],
        compiler_params=_params(1),
        name="inproj",
    )(x, mod.rows, mod.rows, w_bf, *rope)


def _attn_kernel(lq_ref, q_ref, k_ref, v_ref, sg_ref, o_ref, *, lam_init):
    lq = lq_ref[...]
    lam = (jnp.exp(jnp.sum(lq[0:1] * lq[1:2], axis=1, keepdims=True))
           - jnp.exp(jnp.sum(lq[2:3] * lq[3:4], axis=1, keepdims=True)) + lam_init)
    q = q_ref[...]
    lane = lax.broadcasted_iota(jnp.int32, q.shape, 1)
    zero = jnp.zeros_like(q)
    k = k_ref[0]
    nt = (((1,), (1,)), ((), ()))
    s1 = lax.dot_general(jnp.where(lane < DIFF_QK_DIM, q, zero), k, nt, preferred_element_type=F32)
    s2 = lax.dot_general(jnp.where(lane >= DIFF_QK_DIM, q, zero), k, nt, preferred_element_type=F32)
    e1 = jnp.exp(s1 - jnp.max(s1, axis=1, keepdims=True))
    e2 = jnp.exp(s2 - jnp.max(s2, axis=1, keepdims=True))
    w1 = 1.0 / jnp.sum(e1, axis=1, keepdims=True)
    w2 = lam / jnp.sum(e2, axis=1, keepdims=True)
    pd = (e1 * w1 - e2 * w2).astype(BF16)
    o = jnp.dot(pd, v_ref[0], preferred_element_type=F32)
    y = o * lax.rsqrt(jnp.mean(o * o, axis=1, keepdims=True) + SUBLN_EPS)
    o_ref[...] = (y * sg_ref[...] * (1.0 - lam_init)).astype(o_ref.dtype)


def _attention(lq, q, k_all, v_all, sg, lam_init, tq):
    t = q.shape[0]
    b, nk, _ = k_all.shape
    nq = t // b // tq
    qo = pl.BlockSpec((tq, DIFF_V_DIM), lambda bi, h, i: (bi * nq + i, h))
    kv = pl.BlockSpec((1, nk, DIFF_V_DIM), lambda bi, h, i: (bi, 0, h))
    return pl.pallas_call(
        functools.partial(_attn_kernel, lam_init=lam_init),
        grid=(b, N_DIFF_HEADS, nq),
        in_specs=[pl.BlockSpec(lq.shape, lambda bi, h, i: (0, 0)), qo, kv, kv,
                  pl.BlockSpec((1, DIFF_V_DIM), lambda bi, h, i: (0, 0))],
        out_specs=qo,
        out_shape=jax.ShapeDtypeStruct((t, D_ATTN), BF16),
        compiler_params=_params(3),
        name="diff_attention",
    )(lq, q, k_all, v_all, sg.reshape(1, DIFF_V_DIM))


def _mix_out_kernel(h_ref, gb_ref, g_ref, gprev_ref, gnext_ref, a_ref, w_ref, cw_ref,
                    gate_ref, lg_ref, lb_ref, o_ref, *, steps_per_seq, alpha):
    i = pl.program_id(0)
    g = g_ref[...]
    tm = g.shape[0]
    row = lax.broadcasted_iota(jnp.int32, g.shape, 0)
    pos = i % steps_per_seq
    keep_prev = jnp.where(pos == 0, 0.0, 1.0)
    keep_next = jnp.where(pos == steps_per_seq - 1, 0.0, 1.0)
    before = jnp.where(row == 0, gprev_ref[SUBLANES - 1:SUBLANES, :] * keep_prev, pltpu.roll(g, 1, 0))
    after = jnp.where(row == tm - 1, gnext_ref[0:1, :] * keep_next, pltpu.roll(g, tm - 1, 0))
    cw = cw_ref[...]
    conv = before * cw[0:1] + g * cw[1:2] + after * cw[2:3]
    yc = (gb_ref[...] * conv).astype(BF16)
    y = (jnp.dot(yc, w_ref[0:D_CONV, :], preferred_element_type=F32)
         + jnp.dot(a_ref[...], w_ref[D_CONV:D_CONV + D_ATTN, :], preferred_element_type=F32))
    o_ref[...] = _post_norm(h_ref[...], y, gate_ref, lg_ref, lb_ref, alpha)


def _mix_out(h, gb, g, attn, w_bf, conv_w, mod, layer, row_of_step, ln_g, ln_b, tm, seq, alpha):
    t, d = h.shape
    steps_per_seq = seq // tm
    blocks8 = tm // SUBLANES
    last8 = t // SUBLANES - 1
    tok = lambda i: (i, 0)
    vec = pl.BlockSpec((1, d), lambda i: (0, 0))
    return pl.pallas_call(
        functools.partial(_mix_out_kernel, steps_per_seq=steps_per_seq, alpha=alpha),
        grid=(t // tm,),
        in_specs=[pl.BlockSpec((tm, d), tok), pl.BlockSpec((tm, D_CONV), tok),
                  pl.BlockSpec((tm, D_CONV), tok),
                  pl.BlockSpec((SUBLANES, D_CONV), lambda i: (jnp.maximum(i * blocks8 - 1, 0), 0)),
                  pl.BlockSpec((SUBLANES, D_CONV), lambda i: (jnp.minimum((i + 1) * blocks8, last8), 0)),
                  pl.BlockSpec((tm, D_ATTN), tok), _const_spec(w_bf.shape),
                  pl.BlockSpec(conv_w.shape, lambda i: (0, 0)),
                  mod.spec(layer, 2, row_of_step), vec, vec],
        out_specs=pl.BlockSpec((tm, d), tok),
        out_shape=jax.ShapeDtypeStruct((t, d), F32),
        compiler_params=_params(1),
        name="mix_out",
    )(h, gb, g, g, g, attn, w_bf, conv_w, mod.rows, ln_g.reshape(1, d), ln_b.reshape(1, d))


def _mlp_kernel(h_ref, sh_ref, sc_ref, w1_ref, w2_ref, gate_ref, lg_ref, lb_ref, o_ref, *, chunk, alpha):
    h = h_ref[...]
    u = _modulated(h, sh_ref, sc_ref)
    y = jnp.zeros(h.shape, F32)
    for c in range(w1_ref.shape[1] // chunk):
        hid = jnp.dot(u, w1_ref[:, c * chunk:(c + 1) * chunk], preferred_element_type=F32)
        hid = jnp.square(jnp.maximum(hid, 0.0)).astype(BF16)
        y = y + jnp.dot(hid, w2_ref[c * chunk:(c + 1) * chunk, :], preferred_element_type=F32)
    o_ref[...] = _post_norm(h, y, gate_ref, lg_ref, lb_ref, alpha)


def _mlp(h, w1_bf, w2_bf, mod, layer, row_of_step, ln_g, ln_b, tm, alpha):
    t, d = h.shape
    tok = lambda i: (i, 0)
    vec = pl.BlockSpec((1, d), lambda i: (0, 0))
    return pl.pallas_call(
        functools.partial(_mlp_kernel, chunk=512, alpha=alpha),
        grid=(t // tm,),
        in_specs=[pl.BlockSpec((tm, d), tok), mod.spec(layer, 3, row_of_step),
                  mod.spec(layer, 4, row_of_step), _const_spec(w1_bf.shape), _const_spec(w2_bf.shape),
                  mod.spec(layer, 5, row_of_step), vec, vec],
        out_specs=pl.BlockSpec((tm, d), tok),
        out_shape=jax.ShapeDtypeStruct((t, d), F32),
        compiler_params=_params(1),
        name="mlp",
    )(h, mod.rows, mod.rows, w1_bf, w2_bf, mod.rows, ln_g.reshape(1, d), ln_b.reshape(1, d))


def _fourier_in_kernel(x_ref, sh_ref, sc_ref, gcs_ref, z_ref):
    u = _modulated(x_ref[...], sh_ref, sc_ref)
    gcs = gcs_ref[...]
    for g in range(u.shape[1] // FOURIER_GROUP):
        lo, hi = g * FOURIER_GROUP, (g + 1) * FOURIER_GROUP
        zz = jnp.dot(u[:, lo:hi], gcs, preferred_element_type=F32)
        z_ref[0, 0, :, lo:hi] = zz[:, 0:FOURIER_GROUP].astype(BF16)
        z_ref[0, 1, :, lo:hi] = zz[:, FOURIER_GROUP:2 * FOURIER_GROUP].astype(BF16)


def _fourier_in(x, mod, layer, row_of_step, gcs, tm, seq):
    t, d = x.shape
    steps_per_seq = seq // tm
    return pl.pallas_call(
        _fourier_in_kernel,
        grid=(t // tm,),
        in_specs=[pl.BlockSpec((tm, d), lambda i: (i, 0)), mod.spec(layer, 0, row_of_step),
                  mod.spec(layer, 1, row_of_step), _const_spec(gcs.shape)],
        out_specs=pl.BlockSpec((1, 2, tm, d), lambda i: (i // steps_per_seq, 0, i % steps_per_seq, 0)),
        out_shape=jax.ShapeDtypeStruct((t // seq, 2, seq, d), BF16),
        compiler_params=_params(1),
        name="fourier_in",
    )(x, mod.rows, mod.rows, gcs)


def _fourier_rows_kernel(z_ref, m_ref, twr_ref, twi_ref, o_ref, *, d):
    y = jnp.dot(m_ref[...], z_ref[0], preferred_element_type=F32)
    w = y.shape[0] // 2
    for j in range(y.shape[1] // d):
        yr = y[0:w, j * d:(j + 1) * d]
        yi = y[w:2 * w, j * d:(j + 1) * d]
        tr = jnp.tile(twr_ref[:, j * LANES:(j + 1) * LANES], (1, d // LANES))
        ti = jnp.tile(twi_ref[:, j * LANES:(j + 1) * LANES], (1, d // LANES))
        o_ref[0, 0, :, j * d:(j + 1) * d] = (yr * tr - yi * ti).astype(BF16)
        o_ref[0, 1, :, j * d:(j + 1) * d] = (yr * ti + yi * tr).astype(BF16)


def _fourier_rows(z, m, twr, twi, cols_per_step):
    b, _, seq, d = z.shape
    w = GRID_W
    tn = cols_per_step * d
    zc = z.reshape(b, 2 * w, w * d)
    return pl.pallas_call(
        functools.partial(_fourier_rows_kernel, d=d),
        grid=(b, w // cols_per_step),
        in_specs=[pl.BlockSpec((1, 2 * w, tn), lambda bi, j: (bi, 0, j)), _const_spec(m.shape),
                  pl.BlockSpec((w, cols_per_step * LANES), lambda bi, j: (0, j)),
                  pl.BlockSpec((w, cols_per_step * LANES), lambda bi, j: (0, j))],
        out_specs=pl.BlockSpec((1, 2, w, tn), lambda bi, j: (bi, 0, 0, j)),
        out_shape=jax.ShapeDtypeStruct((b, 2, w, w * d), BF16),
        compiler_params=_params(2),
        name="fourier_rows",
    )(zc, m, twr, twi)


def _fourier_cols_kernel(z_ref, cs_ref, o_ref, *, norm):
    cs = cs_ref[...]
    for r in range(z_ref.shape[2]):
        zc = jnp.concatenate([z_ref[0, 0, r], z_ref[0, 1, r]], axis=0)
        o_ref[0, r] = (jnp.dot(cs, zc, preferred_element_type=F32) * norm).astype(BF16)


def _fourier_cols(z2, cs, d, slabs_per_step, norm):
    b = z2.shape[0]
    w = GRID_W
    z5 = z2.reshape(b, 2, w, w, d)
    return pl.pallas_call(
        functools.partial(_fourier_cols_kernel, norm=norm),
        grid=(b, w // slabs_per_step),
        in_specs=[pl.BlockSpec((1, 2, slabs_per_step, w, d), lambda bi, j: (bi, 0, j, 0, 0)),
                  _const_spec(cs.shape)],
        out_specs=pl.BlockSpec((1, slabs_per_step, w, d), lambda bi, j: (bi, j, 0, 0)),
        out_shape=jax.ShapeDtypeStruct((b, w, w, d), BF16),
        compiler_params=_params(2),
        name="fourier_cols",
    )(z5, cs)


def _fourier_dense_kernel(z_ref, cs_ref, o_ref, *, norm):
    zc = jnp.concatenate([z_ref[0, 0], z_ref[0, 1]], axis=0)
    o_ref[0] = (jnp.dot(cs_ref[...], zc, preferred_element_type=F32) * norm).astype(BF16)


def _fourier_dense(z, cs, norm):
    b, _, seq, d = z.shape
    return pl.pallas_call(
        functools.partial(_fourier_dense_kernel, norm=norm),
        grid=(b,),
        in_specs=[pl.BlockSpec((1, 2, seq, d), lambda bi: (bi, 0, 0, 0)), _const_spec(cs.shape)],
        out_specs=pl.BlockSpec((1, seq, d), lambda bi: (bi, 0, 0)),
        out_shape=jax.ShapeDtypeStruct((b, seq, d), BF16),
        compiler_params=_params(1),
        name="fourier_dense",
    )(z, cs)


def _proj_out_kernel(h_ref, a_ref, w_ref, gate_ref, lg_ref, lb_ref, o_ref, *, alpha):
    y = jnp.dot(a_ref[...], w_ref[...], preferred_element_type=F32)
    o_ref[...] = _post_norm(h_ref[...], y, gate_ref, lg_ref, lb_ref, alpha)


def _proj_out(h, a, w_bf, mod, layer, row_of_step, ln_g, ln_b, tm, alpha):
    t, d = h.shape
    tok = lambda i: (i, 0)
    vec = pl.BlockSpec((1, d), lambda i: (0, 0))
    return pl.pallas_call(
        functools.partial(_proj_out_kernel, alpha=alpha),
        grid=(t // tm,),
        in_specs=[pl.BlockSpec((tm, d), tok), pl.BlockSpec((tm, a.shape[1]), tok),
                  _const_spec(w_bf.shape), mod.spec(layer, 2, row_of_step), vec, vec],
        out_specs=pl.BlockSpec((tm, d), tok),
        out_shape=jax.ShapeDtypeStruct((t, d), F32),
        compiler_params=_params(1),
        name="proj_out",
    )(h, a, w_bf, mod.rows, ln_g.reshape(1, d), ln_b.reshape(1, d))


def _dft_cos_sin(n, denom):
    idx = jnp.arange(n, dtype=jnp.int32)
    ang = ((idx[:, None] * idx[None, :]) % denom).astype(F32) * (2.0 * math.pi / denom)
    return jnp.cos(ang), jnp.sin(ang)


def _rope_tables(seq):
    half = DIFF_QK_DIM // 4
    t = jnp.arange(seq, dtype=jnp.int32)
    row = (t // GRID_W).astype(F32)
    col = (t % GRID_W).astype(F32)
    inv = ROPE_BASE ** (-jnp.arange(half, dtype=F32) / half)
    lane = np.arange(LANES)
    use_col = (lane % DIFF_QK_DIM) >= DIFF_QK_DIM // 2
    second = (lane % (2 * half)) >= half
    pos = jnp.where(use_col[None, :], col[:, None], row[:, None])
    ang = pos * inv[lane % half][None, :]
    cos, sin = jnp.cos(ang), jnp.sin(ang)
    zero = jnp.zeros_like(sin)
    return cos, jnp.where(second[None, :], zero, -sin), jnp.where(second[None, :], sin, zero)


def _no_rope_tables(seq):
    return jnp.ones((seq, LANES), F32), jnp.zeros((seq, LANES), F32), jnp.zeros((seq, LANES), F32)


def kernel(x, c, ctx, c_ctx, ada_w, ada_b, ln_g, ln_b, mlp_w1, mlp_w2, w_in, conv_w, lambda_qk,
           subln_g, w_out_mix, w_out_fourier):
    batch, seq, d = x.shape
    ctx_len = ctx.shape[1]
    depth = ada_w.shape[0]
    alpha = (2.0 * depth) ** 0.25
    last_attn = 2 * ((depth - 1) // 2)
    tm = 512
    assert seq == GRID_W * GRID_W and seq % tm == 0 and batch < SUBLANES and ctx_len % SUBLANES == 0

    cond8 = jnp.zeros((SUBLANES, d), F32).at[:batch].set(c).at[batch].set(c_ctx)
    mod = _Mod(_modulation(cond8, ada_w, ada_b), d)
    lat_row = lambda i: i // (seq // tm)
    ctx_row = lambda i: batch

    w_in_bf = w_in.astype(BF16)
    w_mix_bf = w_out_mix.astype(BF16)
    w_four_bf = w_out_fourier.astype(BF16)
    w1_bf = mlp_w1.astype(BF16)
    w2_bf = mlp_w2.astype(BF16)

    rope = _rope_tables(seq)
    no_rope = _no_rope_tables(ctx_len)

    gc, gs = _dft_cos_sin(FOURIER_GROUP, FOURIER_GROUP)
    gcs = jnp.concatenate([gc, -gs], axis=1).astype(BF16)
    c64, s64 = _dft_cos_sin(GRID_W, GRID_W)
    m_rows = jnp.concatenate([jnp.concatenate([c64, s64], axis=1),
                              jnp.concatenate([-s64, c64], axis=1)], axis=0).astype(BF16)
    cs_cols = jnp.concatenate([c64, s64], axis=1).astype(BF16)
    twc, tws = _dft_cos_sin(GRID_W, seq)
    twr = jnp.repeat(twc, LANES, axis=1)
    twi = jnp.repeat(-tws, LANES, axis=1)
    cc, sc_ = _dft_cos_sin(ctx_len, ctx_len)
    cs_ctx = jnp.concatenate([cc, sc_], axis=1).astype(BF16)

    h = x.reshape(batch * seq, d)
    hc = ctx.reshape(batch * ctx_len, d)

    for i in range(depth):
        j = i // 2
        update_ctx = i < last_attn
        if i % 2 == 0:
            lam_init = 0.8 - 0.6 * math.exp(-0.3 * i)
            gb, g, q, k, v = _inproj(h, mod, i, lat_row, w_in_bf[j], rope, tm, seq)
            if update_ctx or i == last_attn:
                gbc, gcx, qc, kc, vc = _inproj(hc, mod, i, ctx_row, w_in_bf[j], no_rope, ctx_len, ctx_len)
                k_all = jnp.concatenate([k.reshape(batch, seq, COL_QK),
                                         kc.reshape(batch, ctx_len, COL_QK)], axis=1)
                v_all = jnp.concatenate([v.reshape(batch, seq, D_ATTN),
                                         vc.reshape(batch, ctx_len, D_ATTN)], axis=1)
            else:
                k_all = k.reshape(batch, seq, COL_QK)
                v_all = v.reshape(batch, seq, D_ATTN)
            attn = _attention(lambda_qk[j], q, k_all, v_all, subln_g[j], lam_init, 256)
            h1 = _mix_out(h, gb, g, attn, w_mix_bf[j], conv_w[j], mod, i, lat_row,
                          ln_g[i, 0], ln_b[i, 0], tm, seq, alpha)
            if update_ctx:
                attn_c = _attention(lambda_qk[j], qc, kc.reshape(batch, ctx_len, COL_QK),
                                    vc.reshape(batch, ctx_len, D_ATTN), subln_g[j], lam_init, ctx_len)
                hc1 = _mix_out(hc, gbc, gcx, attn_c, w_mix_bf[j], conv_w[j], mod, i, ctx_row,
                               ln_g[i, 0], ln_b[i, 0], ctx_len, ctx_len, alpha)
        else:
            z = _fourier_in(h, mod, i, lat_row, gcs, tm, seq)
            z2 = _fourier_rows(z, m_rows, twr, twi, 4)
            ft = _fourier_cols(z2, cs_cols, d, 8, (seq * FOURIER_GROUP) ** -0.5)
            f = ft.swapaxes(1, 2).reshape(batch * seq, d)
            h1 = _proj_out(h, f, w_four_bf[j], mod, i, lat_row, ln_g[i, 0], ln_b[i, 0], tm, alpha)
            if update_ctx:
                zc = _fourier_in(hc, mod, i, ctx_row, gcs, ctx_len, ctx_len)
                fc = _fourier_dense(zc, cs_ctx, (ctx_len * FOURIER_GROUP) ** -0.5)
                hc1 = _proj_out(hc, fc.reshape(batch * ctx_len, d), w_four_bf[j], mod, i, ctx_row,
                                ln_g[i, 0], ln_b[i, 0], ctx_len, alpha)
        h = _mlp(h1, w1_bf[i], w2_bf[i], mod, i, lat_row, ln_g[i, 1], ln_b[i, 1], tm, alpha)
        if update_ctx:
            hc = _mlp(hc1, w1_bf[i], w2_bf[i], mod, i, ctx_row, ln_g[i, 1], ln_b[i, 1], ctx_len, alpha)
    return h.reshape(batch, seq, d)
```

```python
import functools
import math

import jax
import jax.numpy as jnp
import numpy as np
from jax import lax
from jax.experimental import pallas as pl
from jax.experimental.pallas import tpu as pltpu

F32 = jnp.float32
BF16 = jnp.bfloat16

GRID_W = 64
CONV_WIDTH = 3
D_CONV = 512
N_DIFF_HEADS = 4
DIFF_QK_DIM = 64
DIFF_V_DIM = 128
D_ATTN = N_DIFF_HEADS * DIFF_V_DIM
FOURIER_GROUP = 128
ROPE_BASE = 10000.0
LN_EPS = 1e-6
SUBLN_EPS = 1e-5
COL_CONV = 3 * D_CONV
COL_QK = N_DIFF_HEADS * 2 * DIFF_QK_DIM
Q0 = COL_CONV
K0 = Q0 + COL_QK
V0 = K0 + COL_QK
Q_SCALE = math.log2(math.e) * DIFF_QK_DIM ** -0.5

LANES = 128
SUBLANES = 8
VMEM_LIMIT = 56 * 1024 * 1024


def _params(n_axes):
    return pltpu.CompilerParams(
        dimension_semantics=("arbitrary",) * n_axes, vmem_limit_bytes=VMEM_LIMIT)


def _const_spec(shape):
    nd = len(shape)
    return pl.BlockSpec(shape, lambda *_: (0,) * nd, pipeline_mode=pl.Buffered(1))


def _layer_norm(x):
    mu = jnp.mean(x, axis=-1, keepdims=True)
    xc = x - mu
    var = jnp.mean(xc * xc, axis=-1, keepdims=True)
    return xc * lax.rsqrt(var + LN_EPS)


def _modulated(x, shift_ref, scale_ref):
    return (_layer_norm(x) * (1.0 + scale_ref[0]) + shift_ref[0]).astype(BF16)


def _post_norm(h, y, gate_ref, g_ref, b_ref, alpha):
    return _layer_norm(alpha * h + gate_ref[0] * y) * g_ref[...] + b_ref[...]


def _mod_kernel(cond_ref, w_ref, b_ref, o_ref):
    c = cond_ref[...]
    a = (c * (1.0 / (1.0 + jnp.exp(-c)))).astype(BF16)
    o_ref[0] = jnp.dot(a, w_ref[0].astype(BF16), preferred_element_type=F32) + b_ref[0]


def _modulation(cond8, ada_w, ada_b):
    depth, d, n6 = ada_w.shape
    tn = 1536
    return pl.pallas_call(
        _mod_kernel,
        grid=(depth, n6 // tn),
        in_specs=[
            pl.BlockSpec((SUBLANES, d), lambda l, j: (0, 0)),
            pl.BlockSpec((1, d, tn), lambda l, j: (l, 0, j)),
            pl.BlockSpec((1, 1, tn), lambda l, j: (l, 0, j)),
        ],
        out_specs=pl.BlockSpec((1, SUBLANES, tn), lambda l, j: (l, 0, j)),
        out_shape=jax.ShapeDtypeStruct((depth, SUBLANES, n6), F32),
        compiler_params=_params(2),
        name="modulation",
    )(cond8, ada_w, ada_b.reshape(depth, 1, n6))


class _Mod:
    def __init__(self, table, d):
        depth = table.shape[0]
        self.rows = table.reshape(depth, SUBLANES, 6, d).transpose(0, 2, 1, 3).reshape(
            depth * 6 * SUBLANES, 1, d)
        self.d = d

    def spec(self, layer, which, row_of_step):
        base = (layer * 6 + which) * SUBLANES
        return pl.BlockSpec((1, 1, self.d), lambda *idx: (base + row_of_step(*idx), 0, 0))


def _rope_store(dst_ref, p, cos_ref, sin_lo_ref, sin_hi_ref, scale):
    cos, sin_lo, sin_hi = cos_ref[...], sin_lo_ref[...], sin_hi_ref[...]
    half = DIFF_QK_DIM // 4
    for c in range(p.shape[1] // LANES):
        x = p[:, c * LANES:(c + 1) * LANES]
        ahead = pltpu.roll(x, LANES - half, 1)
        behind = pltpu.roll(x, half, 1)
        r = x * cos + ahead * sin_lo + behind * sin_hi
        if scale != 1.0:
            r = r * scale
        dst_ref[:, c * LANES:(c + 1) * LANES] = r.astype(dst_ref.dtype)


def _inproj_kernel(x_ref, sh_ref, sc_ref, w_ref, cos_ref, sin_lo_ref, sin_hi_ref, *rest):
    gb_ref, g_ref, q_ref, k_ref, v_ref = rest[-5:]
    u = _modulated(x_ref[...], sh_ref, sc_ref)
    pc = jnp.dot(u, w_ref[:, 0:COL_CONV], preferred_element_type=F32)
    gb_ref[...] = pc[:, 0:D_CONV]
    g_ref[...] = pc[:, D_CONV:2 * D_CONV] * pc[:, 2 * D_CONV:3 * D_CONV]
    pq = jnp.dot(u, w_ref[:, Q0:K0], preferred_element_type=F32)
    _rope_store(q_ref, pq, cos_ref, sin_lo_ref, sin_hi_ref, Q_SCALE)
    pk = jnp.dot(u, w_ref[:, K0:V0], preferred_element_type=F32)
    _rope_store(k_ref.at[0], pk, cos_ref, sin_lo_ref, sin_hi_ref, 1.0)
    v_ref[0] = jnp.dot(u, w_ref[:, V0:V0 + D_ATTN], preferred_element_type=F32).astype(BF16)


def _inproj(x, mod, layer, row_of_step, w_bf, rope, tm, seq, kv_rows, kv_row0, kv_bufs=None):
    t, d = x.shape
    steps_per_seq = seq // tm
    tok = lambda i: (i, 0)
    pos = lambda i: (i % steps_per_seq, 0)
    kvi = lambda i: (i // steps_per_seq, kv_row0 // tm + i % steps_per_seq, 0)
    tab = pl.BlockSpec((tm, LANES), pos)
    out = lambda n: pl.BlockSpec((tm, n), tok)
    in_specs = [pl.BlockSpec((tm, d), tok), mod.spec(layer, 0, row_of_step),
                mod.spec(layer, 1, row_of_step), _const_spec(w_bf.shape), tab, tab, tab]
    args = [x, mod.rows, mod.rows, w_bf, *rope]
    aliases = {}
    if kv_bufs is not None:
        in_specs += [pl.BlockSpec(memory_space=pl.ANY)] * 2
        aliases = {len(args): 3, len(args) + 1: 4}
        args += list(kv_bufs)
    nb = t // seq
    return pl.pallas_call(
        _inproj_kernel,
        grid=(t // tm,),
        in_specs=in_specs,
        out_specs=[out(D_CONV), out(D_CONV), out(COL_QK),
                   pl.BlockSpec((1, tm, COL_QK), kvi), pl.BlockSpec((1, tm, D_ATTN), kvi)],
        out_shape=[jax.ShapeDtypeStruct((t, D_CONV), F32), jax.ShapeDtypeStruct((t, D_CONV), F32),
                   jax.ShapeDtypeStruct((t, COL_QK), BF16),
                   jax.ShapeDtypeStruct((nb, kv_rows, COL_QK), BF16),
                   jax.ShapeDtypeStruct((nb, kv_rows, D_ATTN), BF16)],
        input_output_aliases=aliases,
        compiler_params=_params(1),
        name="inproj",
    )(*args)


def _attn_kernel(lq_ref, q_ref, k_ref, v_ref, sg_ref, o_ref, *, lam_init):
    lq = lq_ref[...]
    lam = (jnp.exp(jnp.sum(lq[0:1] * lq[1:2], axis=1, keepdims=True))
           - jnp.exp(jnp.sum(lq[2:3] * lq[3:4], axis=1, keepdims=True)) + lam_init)
    q = q_ref[...]
    tq = q.shape[0]
    lane = lax.broadcasted_iota(jnp.int32, q.shape, 1)
    zero = jnp.zeros_like(q)
    def component(qc):
        s = lax.dot_general(qc, k_ref[0], (((1,), (1,)), ((), ())), preferred_element_type=F32)
        e = jnp.exp2(s - jnp.max(s, axis=1, keepdims=True))
        return e, jnp.sum(e, axis=1, keepdims=True)

    e1, l1 = component(jnp.where(lane < DIFF_QK_DIM, q, zero))
    e2, l2 = component(jnp.where(lane >= DIFF_QK_DIM, q, zero))
    pd = (e1 * (1.0 / l1) - e2 * (lam / l2)).astype(BF16)
    o = jnp.dot(pd, v_ref[0], preferred_element_type=F32)
    y = o * lax.rsqrt(jnp.mean(o * o, axis=1, keepdims=True) + SUBLN_EPS)
    o_ref[...] = (y * sg_ref[...] * (1.0 - lam_init)).astype(o_ref.dtype)


def _attention(lq, q, k_all, v_all, sg, lam_init, tq, nk, key_block):
    t = q.shape[0]
    b = k_all.shape[0]
    nq = t // b // tq
    qo = pl.BlockSpec((tq, DIFF_V_DIM), lambda bi, h, i: (bi * nq + i, h))
    return pl.pallas_call(
        functools.partial(_attn_kernel, lam_init=lam_init),
        grid=(b, N_DIFF_HEADS, nq),
        in_specs=[pl.BlockSpec(lq.shape, lambda bi, h, i: (0, 0)), qo,
                  pl.BlockSpec((1, nk, 2 * DIFF_QK_DIM), lambda bi, h, i: (bi, key_block, h)),
                  pl.BlockSpec((1, nk, DIFF_V_DIM), lambda bi, h, i: (bi, key_block, h)),
                  pl.BlockSpec((1, DIFF_V_DIM), lambda bi, h, i: (0, 0))],
        out_specs=qo,
        out_shape=jax.ShapeDtypeStruct((t, D_ATTN), BF16),
        compiler_params=_params(3),
        name="diff_attention",
    )(lq, q, k_all, v_all, sg.reshape(1, DIFF_V_DIM))


def _mix_out_kernel(h_ref, gb_ref, g_ref, gprev_ref, gnext_ref, a_ref, w_ref, cw_ref,
                    gate_ref, lg_ref, lb_ref, o_ref, *, steps_per_seq, alpha):
    i = pl.program_id(0)
    g = g_ref[...]
    tm = g.shape[0]
    row = lax.broadcasted_iota(jnp.int32, g.shape, 0)
    pos = i % steps_per_seq
    keep_prev = jnp.where(pos == 0, 0.0, 1.0)
    keep_next = jnp.where(pos == steps_per_seq - 1, 0.0, 1.0)
    before = jnp.where(row == 0, gprev_ref[SUBLANES - 1:SUBLANES, :] * keep_prev, pltpu.roll(g, 1, 0))
    after = jnp.where(row == tm - 1, gnext_ref[0:1, :] * keep_next, pltpu.roll(g, tm - 1, 0))
    cw = cw_ref[...]
    conv = before * cw[0:1] + g * cw[1:2] + after * cw[2:3]
    yc = (gb_ref[...] * conv).astype(BF16)
    y = (jnp.dot(yc, w_ref[0:D_CONV, :], preferred_element_type=F32)
         + jnp.dot(a_ref[...], w_ref[D_CONV:D_CONV + D_ATTN, :], preferred_element_type=F32))
    o_ref[...] = _post_norm(h_ref[...], y, gate_ref, lg_ref, lb_ref, alpha)


def _mix_out(h, gb, g, attn, w_bf, conv_w, mod, layer, row_of_step, ln_g, ln_b, tm, seq, alpha):
    t, d = h.shape
    steps_per_seq = seq // tm
    blocks8 = tm // SUBLANES
    last8 = t // SUBLANES - 1
    tok = lambda i: (i, 0)
    vec = pl.BlockSpec((1, d), lambda i: (0, 0))
    return pl.pallas_call(
        functools.partial(_mix_out_kernel, steps_per_seq=steps_per_seq, alpha=alpha),
        grid=(t // tm,),
        in_specs=[pl.BlockSpec((tm, d), tok), pl.BlockSpec((tm, D_CONV), tok),
                  pl.BlockSpec((tm, D_CONV), tok),
                  pl.BlockSpec((SUBLANES, D_CONV), lambda i: (jnp.maximum(i * blocks8 - 1, 0), 0)),
                  pl.BlockSpec((SUBLANES, D_CONV), lambda i: (jnp.minimum((i + 1) * blocks8, last8), 0)),
                  pl.BlockSpec((tm, D_ATTN), tok), _const_spec(w_bf.shape),
                  pl.BlockSpec(conv_w.shape, lambda i: (0, 0)),
                  mod.spec(layer, 2, row_of_step), vec, vec],
        out_specs=pl.BlockSpec((tm, d), tok),
        out_shape=jax.ShapeDtypeStruct((t, d), F32),
        compiler_params=_params(1),
        name="mix_out",
    )(h, gb, g, g, g, attn, w_bf, conv_w, mod.rows, ln_g.reshape(1, d), ln_b.reshape(1, d))


def _mlp_kernel(h_ref, sh_ref, sc_ref, w1_ref, w2_ref, gate_ref, lg_ref, lb_ref, o_ref, *, chunk, alpha):
    h = h_ref[...]
    u = _modulated(h, sh_ref, sc_ref)
    y = jnp.zeros(h.shape, F32)
    for c in range(w1_ref.shape[1] // chunk):
        hid = jnp.dot(u, w1_ref[:, c * chunk:(c + 1) * chunk], preferred_element_type=F32)
        hid = jnp.square(jnp.maximum(hid, 0.0)).astype(BF16)
        y = y + jnp.dot(hid, w2_ref[c * chunk:(c + 1) * chunk, :], preferred_element_type=F32)
    o_ref[...] = _post_norm(h, y, gate_ref, lg_ref, lb_ref, alpha)


def _mlp(h, w1_bf, w2_bf, mod, layer, row_of_step, ln_g, ln_b, tm, alpha):
    t, d = h.shape
    tok = lambda i: (i, 0)
    vec = pl.BlockSpec((1, d), lambda i: (0, 0))
    return pl.pallas_call(
        functools.partial(_mlp_kernel, chunk=512, alpha=alpha),
        grid=(t // tm,),
        in_specs=[pl.BlockSpec((tm, d), tok), mod.spec(layer, 3, row_of_step),
                  mod.spec(layer, 4, row_of_step), _const_spec(w1_bf.shape), _const_spec(w2_bf.shape),
                  mod.spec(layer, 5, row_of_step), vec, vec],
        out_specs=pl.BlockSpec((tm, d), tok),
        out_shape=jax.ShapeDtypeStruct((t, d), F32),
        compiler_params=_params(1),
        name="mlp",
    )(h, mod.rows, mod.rows, w1_bf, w2_bf, mod.rows, ln_g.reshape(1, d), ln_b.reshape(1, d))


def _fourier_in_kernel(x_ref, sh_ref, sc_ref, gcs_ref, z_ref):
    u = _modulated(x_ref[...], sh_ref, sc_ref)
    gcs = gcs_ref[...]
    for g in range(u.shape[1] // FOURIER_GROUP):
        lo, hi = g * FOURIER_GROUP, (g + 1) * FOURIER_GROUP
        zz = jnp.dot(u[:, lo:hi], gcs, preferred_element_type=F32)
        z_ref[0, 0, :, lo:hi] = zz[:, 0:FOURIER_GROUP].astype(BF16)
        z_ref[0, 1, :, lo:hi] = zz[:, FOURIER_GROUP:2 * FOURIER_GROUP].astype(BF16)


def _fourier_in(x, mod, layer, row_of_step, gcs, tm, seq):
    t, d = x.shape
    steps_per_seq = seq // tm
    return pl.pallas_call(
        _fourier_in_kernel,
        grid=(t // tm,),
        in_specs=[pl.BlockSpec((tm, d), lambda i: (i, 0)), mod.spec(layer, 0, row_of_step),
                  mod.spec(layer, 1, row_of_step), _const_spec(gcs.shape)],
        out_specs=pl.BlockSpec((1, 2, tm, d), lambda i: (i // steps_per_seq, 0, i % steps_per_seq, 0)),
        out_shape=jax.ShapeDtypeStruct((t // seq, 2, seq, d), BF16),
        compiler_params=_params(1),
        name="fourier_in",
    )(x, mod.rows, mod.rows, gcs)


FOURIER_SLAB = 16


def _fourier_rows_kernel(x_ref, sh_ref, sc_ref, k1_ref, gm_ref, z_ref):
    w, nb, d = x_ref.shape[1], x_ref.shape[2], x_ref.shape[3]
    rows = w * nb
    u = _modulated(x_ref[0].reshape(rows, d), sh_ref, sc_ref)
    y = jnp.dot(k1_ref[0], u, preferred_element_type=F32).astype(BF16)
    gm = gm_ref[...]
    for g in range(d // FOURIER_GROUP):
        lo, hi = g * FOURIER_GROUP, (g + 1) * FOURIER_GROUP
        yy = jnp.concatenate([y[0:rows, lo:hi], y[rows:2 * rows, lo:hi]], axis=1)
        zz = jnp.dot(yy, gm, preferred_element_type=F32).astype(BF16)
        z_ref[0, 0, :, :, lo:hi] = zz[:, 0:FOURIER_GROUP].reshape(w, nb, FOURIER_GROUP)
        z_ref[0, 1, :, :, lo:hi] = zz[:, FOURIER_GROUP:2 * FOURIER_GROUP].reshape(w, nb, FOURIER_GROUP)


def _fourier_rows(h, mod, layer, k1, gm, batch):
    t, d = h.shape
    w, nb = GRID_W, FOURIER_SLAB
    h4 = h.reshape(batch, w, w, d)
    return pl.pallas_call(
        _fourier_rows_kernel,
        grid=(w // nb, batch),
        in_specs=[pl.BlockSpec((1, w, nb, d), lambda j, bi: (bi, 0, j, 0)),
                  mod.spec(layer, 0, lambda j, bi: bi), mod.spec(layer, 1, lambda j, bi: bi),
                  pl.BlockSpec((1,) + k1.shape[1:], lambda j, bi: (j, 0, 0)), _const_spec(gm.shape)],
        out_specs=pl.BlockSpec((1, 2, w, nb, d), lambda j, bi: (bi, 0, 0, j, 0)),
        out_shape=jax.ShapeDtypeStruct((batch, 2, w, w, d), BF16),
        compiler_params=_params(2),
        name="fourier_rows",
    )(h4, mod.rows, mod.rows, k1, gm)


def _fourier_cols_kernel(z_ref, h_ref, k2_ref, w_ref, gate_ref, lg_ref, lb_ref, o_ref, *, alpha):
    nd, w, d = z_ref.shape[2], z_ref.shape[3], z_ref.shape[4]
    zz = z_ref[0].reshape(2 * nd * w, d)
    f = jnp.dot(k2_ref[...], zz, preferred_element_type=F32).astype(BF16)
    y = jnp.dot(f, w_ref[...], preferred_element_type=F32)
    hn = _post_norm(h_ref[0].reshape(w * nd, d), y, gate_ref, lg_ref, lb_ref, alpha)
    o_ref[0] = hn.reshape(w, nd, d)


def _fourier_cols(z, h, k2, w_bf, mod, layer, ln_g, ln_b, alpha):
    batch, _, w, _, d = z.shape
    nd = FOURIER_SLAB
    h4 = h.reshape(batch, w, w, d)
    vec = pl.BlockSpec((1, d), lambda bi, j: (0, 0))
    hblk = pl.BlockSpec((1, w, nd, d), lambda bi, j: (bi, 0, j, 0))
    out = pl.pallas_call(
        functools.partial(_fourier_cols_kernel, alpha=alpha),
        grid=(batch, w // nd),
        in_specs=[pl.BlockSpec((1, 2, nd, w, d), lambda bi, j: (bi, 0, j, 0, 0)), hblk,
                  _const_spec(k2.shape), _const_spec(w_bf.shape),
                  mod.spec(layer, 2, lambda bi, j: bi), vec, vec],
        out_specs=hblk,
        out_shape=jax.ShapeDtypeStruct((batch, w, w, d), F32),
        compiler_params=_params(2),
        name="fourier_cols",
    )(z, h4, k2, w_bf, mod.rows, ln_g.reshape(1, d), ln_b.reshape(1, d))
    return out.reshape(batch * w * w, d)


def _fourier_dense_kernel(z_ref, cs_ref, o_ref, *, norm):
    zc = jnp.concatenate([z_ref[0, 0], z_ref[0, 1]], axis=0)
    o_ref[0] = (jnp.dot(cs_ref[...], zc, preferred_element_type=F32) * norm).astype(BF16)


def _fourier_dense(z, cs, norm):
    b, _, seq, d = z.shape
    return pl.pallas_call(
        functools.partial(_fourier_dense_kernel, norm=norm),
        grid=(b,),
        in_specs=[pl.BlockSpec((1, 2, seq, d), lambda bi: (bi, 0, 0, 0)), _const_spec(cs.shape)],
        out_specs=pl.BlockSpec((1, seq, d), lambda bi: (bi, 0, 0)),
        out_shape=jax.ShapeDtypeStruct((b, seq, d), BF16),
        compiler_params=_params(1),
        name="fourier_dense",
    )(z, cs)


def _proj_out_kernel(h_ref, a_ref, w_ref, gate_ref, lg_ref, lb_ref, o_ref, *, alpha):
    y = jnp.dot(a_ref[...], w_ref[...], preferred_element_type=F32)
    o_ref[...] = _post_norm(h_ref[...], y, gate_ref, lg_ref, lb_ref, alpha)


def _proj_out(h, a, w_bf, mod, layer, row_of_step, ln_g, ln_b, tm, alpha):
    t, d = h.shape
    tok = lambda i: (i, 0)
    vec = pl.BlockSpec((1, d), lambda i: (0, 0))
    return pl.pallas_call(
        functools.partial(_proj_out_kernel, alpha=alpha),
        grid=(t // tm,),
        in_specs=[pl.BlockSpec((tm, d), tok), pl.BlockSpec((tm, a.shape[1]), tok),
                  _const_spec(w_bf.shape), mod.spec(layer, 2, row_of_step), vec, vec],
        out_specs=pl.BlockSpec((tm, d), tok),
        out_shape=jax.ShapeDtypeStruct((t, d), F32),
        compiler_params=_params(1),
        name="proj_out",
    )(h, a, w_bf, mod.rows, ln_g.reshape(1, d), ln_b.reshape(1, d))


def _dft_cos_sin(n, denom):
    idx = jnp.arange(n, dtype=jnp.int32)
    ang = ((idx[:, None] * idx[None, :]) % denom).astype(F32) * (2.0 * math.pi / denom)
    return jnp.cos(ang), jnp.sin(ang)


def _rope_tables(seq):
    half = DIFF_QK_DIM // 4
    t = jnp.arange(seq, dtype=jnp.int32)
    row = (t // GRID_W).astype(F32)
    col = (t % GRID_W).astype(F32)
    inv = ROPE_BASE ** (-jnp.arange(half, dtype=F32) / half)
    lane = np.arange(LANES)
    use_col = (lane % DIFF_QK_DIM) >= DIFF_QK_DIM // 2
    second = (lane % (2 * half)) >= half
    pos = jnp.where(use_col[None, :], col[:, None], row[:, None])
    ang = pos * inv[lane % half][None, :]
    cos, sin = jnp.cos(ang), jnp.sin(ang)
    zero = jnp.zeros_like(sin)
    return cos, jnp.where(second[None, :], zero, -sin), jnp.where(second[None, :], sin, zero)


def _fourier_tables(seq, norm):
    w, nb = GRID_W, FOURIER_SLAB
    eye = jnp.eye(nb, dtype=F32)
    idx = jnp.arange(w, dtype=jnp.int32)
    slab = jnp.arange(w // nb, dtype=jnp.int32)
    col = jnp.arange(nb, dtype=jnp.int32)
    t = idx[None, None, None, :] * w + slab[:, None, None, None] * nb + col[None, None, :, None]
    ang = ((idx[None, :, None, None] * t) % seq).astype(F32) * (2.0 * math.pi / seq)
    trig = jnp.stack([jnp.cos(ang), -jnp.sin(ang)], axis=1)
    k1 = trig[..., None] * eye[None, None, None, :, None, :]
    k1 = k1.reshape(w // nb, 2 * w * nb, w * nb).astype(BF16)
    c64, s64 = _dft_cos_sin(w, w)
    cs = jnp.stack([c64, s64], axis=1) * norm
    k2 = cs[:, None, :, None, :] * eye[None, :, None, :, None]
    k2 = k2.reshape(w * nb, 2 * nb * w).astype(BF16)
    return k1, k2


def _no_rope_tables(seq):
    return jnp.ones((seq, LANES), F32), jnp.zeros((seq, LANES), F32), jnp.zeros((seq, LANES), F32)


def kernel(x, c, ctx, c_ctx, ada_w, ada_b, ln_g, ln_b, mlp_w1, mlp_w2, w_in, conv_w, lambda_qk,
           subln_g, w_out_mix, w_out_fourier):
    batch, seq, d = x.shape
    ctx_len = ctx.shape[1]
    depth = ada_w.shape[0]
    alpha = (2.0 * depth) ** 0.25
    last_attn = 2 * ((depth - 1) // 2)
    tm = 512
    assert seq == GRID_W * GRID_W and seq % tm == 0 and batch < SUBLANES and ctx_len % SUBLANES == 0

    cond8 = jnp.zeros((SUBLANES, d), F32).at[:batch].set(c).at[batch].set(c_ctx)
    mod = _Mod(_modulation(cond8, ada_w, ada_b), d)
    lat_row = lambda i: i // (seq // tm)
    ctx_row = lambda i: batch

    w_in_bf = w_in.astype(BF16)
    w_mix_bf = w_out_mix.astype(BF16)
    w_four_bf = w_out_fourier.astype(BF16)
    w1_bf = mlp_w1.astype(BF16)
    w2_bf = mlp_w2.astype(BF16)

    rope = _rope_tables(seq)
    no_rope = _no_rope_tables(ctx_len)

    gc, gs = _dft_cos_sin(FOURIER_GROUP, FOURIER_GROUP)
    gcs = jnp.concatenate([gc, -gs], axis=1).astype(BF16)
    gmat = jnp.concatenate([jnp.concatenate([gc, -gs], axis=1),
                            jnp.concatenate([gs, gc], axis=1)], axis=0).astype(BF16)
    k1, k2 = _fourier_tables(seq, (seq * FOURIER_GROUP) ** -0.5)
    cc, sc_ = _dft_cos_sin(ctx_len, ctx_len)
    cs_ctx = jnp.concatenate([cc, sc_], axis=1).astype(BF16)

    h = x.reshape(batch * seq, d)
    hc = ctx.reshape(batch * ctx_len, d)

    for i in range(depth):
        j = i // 2
        update_ctx = i < last_attn
        if i % 2 == 0:
            lam_init = 0.8 - 0.6 * math.exp(-0.3 * i)
            with_ctx = update_ctx or i == last_attn
            kv_rows = seq + ctx_len if with_ctx else seq
            gb, g, q, k_all, v_all = _inproj(h, mod, i, lat_row, w_in_bf[j], rope, tm, seq, kv_rows, 0)
            if with_ctx:
                gbc, gcx, qc, k_all, v_all = _inproj(hc, mod, i, ctx_row, w_in_bf[j], no_rope, ctx_len,
                                                     ctx_len, kv_rows, seq, (k_all, v_all))
            attn = _attention(lambda_qk[j], q, k_all, v_all, subln_g[j], lam_init, 256, kv_rows, 0)
            h1 = _mix_out(h, gb, g, attn, w_mix_bf[j], conv_w[j], mod, i, lat_row,
                          ln_g[i, 0], ln_b[i, 0], tm, seq, alpha)
            if update_ctx:
                attn_c = _attention(lambda_qk[j], qc, k_all, v_all, subln_g[j], lam_init, ctx_len,
                                    ctx_len, seq // ctx_len)
                hc1 = _mix_out(hc, gbc, gcx, attn_c, w_mix_bf[j], conv_w[j], mod, i, ctx_row,
                               ln_g[i, 0], ln_b[i, 0], ctx_len, ctx_len, alpha)
        else:
            z = _fourier_rows(h, mod, i, k1, gmat, batch)
            h1 = _fourier_cols(z, h, k2, w_four_bf[j], mod, i, ln_g[i, 0], ln_b[i, 0], alpha)
            if update_ctx:
                zc = _fourier_in(hc, mod, i, ctx_row, gcs, ctx_len, ctx_len)
                fc = _fourier_dense(zc, cs_ctx, (ctx_len * FOURIER_GROUP) ** -0.5)
                hc1 = _proj_out(hc, fc.reshape(batch * ctx_len, d), w_four_bf[j], mod, i, ctx_row,
                                ln_g[i, 0], ln_b[i, 0], ctx_len, alpha)
        h = _mlp(h1, w1_bf[i], w2_bf[i], mod, i, lat_row, ln_g[i, 1], ln_b[i, 1], tm, alpha)
        if update_ctx:
            hc = _mlp(hc1, w1_bf[i], w2_bf[i], mod, i, ctx_row, ln_g[i, 1], ln_b[i, 1], ctx_len, alpha)
    return h.reshape(batch, seq, d)
```

```python
import functools
import math

import jax
import jax.numpy as jnp
import numpy as np
from jax import lax
from jax.experimental import pallas as pl
from jax.experimental.pallas import tpu as pltpu

F32 = jnp.float32
BF16 = jnp.bfloat16

GRID_W = 64
CONV_WIDTH = 3
D_CONV = 512
N_DIFF_HEADS = 4
DIFF_QK_DIM = 64
DIFF_V_DIM = 128
D_ATTN = N_DIFF_HEADS * DIFF_V_DIM
FOURIER_GROUP = 128
ROPE_BASE = 10000.0
LN_EPS = 1e-6
SUBLN_EPS = 1e-5
COL_CONV = 3 * D_CONV
COL_QK = N_DIFF_HEADS * 2 * DIFF_QK_DIM
Q0 = COL_CONV
K0 = Q0 + COL_QK
V0 = K0 + COL_QK
Q_SCALE = math.log2(math.e) * DIFF_QK_DIM ** -0.5

LANES = 128
SUBLANES = 8
VMEM_LIMIT = 56 * 1024 * 1024


def _params(n_axes):
    return pltpu.CompilerParams(
        dimension_semantics=("arbitrary",) * n_axes, vmem_limit_bytes=VMEM_LIMIT)


def _const_spec(shape):
    nd = len(shape)
    return pl.BlockSpec(shape, lambda *_: (0,) * nd, pipeline_mode=pl.Buffered(1))


class _Stacked:
    def __init__(self, array, index):
        self.array, self.index = array, index

    def spec(self):
        k, n = self.array.shape[1:]
        index = self.index
        return pl.BlockSpec((None, k, n), lambda *_: (index, 0, 0), pipeline_mode=pl.Buffered(1))


def _layer_norm(x):
    mu = jnp.mean(x, axis=-1, keepdims=True)
    xc = x - mu
    var = jnp.mean(xc * xc, axis=-1, keepdims=True)
    return xc * lax.rsqrt(var + LN_EPS)


def _modulated(x, shift_ref, scale_ref):
    return (_layer_norm(x) * (1.0 + scale_ref[0]) + shift_ref[0]).astype(BF16)


def _post_norm(h, y, gate_ref, g_ref, b_ref, alpha):
    return _layer_norm(alpha * h + gate_ref[0] * y) * g_ref[...] + b_ref[...]


def _mod_kernel(cond_ref, w_ref, b_ref, o_ref):
    c = cond_ref[...]
    a = (c * (1.0 / (1.0 + jnp.exp(-c)))).astype(BF16)
    o_ref[0] = jnp.dot(a, w_ref[0].astype(BF16), preferred_element_type=F32) + b_ref[0]


def _modulation(cond8, ada_w, ada_b):
    depth, d, n6 = ada_w.shape
    tn = 1536
    return pl.pallas_call(
        _mod_kernel,
        grid=(depth, n6 // tn),
        in_specs=[
            pl.BlockSpec((SUBLANES, d), lambda l, j: (0, 0)),
            pl.BlockSpec((1, d, tn), lambda l, j: (l, 0, j)),
            pl.BlockSpec((1, 1, tn), lambda l, j: (l, 0, j)),
        ],
        out_specs=pl.BlockSpec((1, SUBLANES, tn), lambda l, j: (l, 0, j)),
        out_shape=jax.ShapeDtypeStruct((depth, SUBLANES, n6), F32),
        compiler_params=_params(2),
        name="modulation",
    )(cond8, ada_w, ada_b.reshape(depth, 1, n6))


class _Mod:
    def __init__(self, table, d):
        depth = table.shape[0]
        self.rows = table.reshape(depth, SUBLANES, 6, d).transpose(0, 2, 1, 3).reshape(
            depth * 6 * SUBLANES, 1, d)
        self.d = d

    def spec(self, layer, which, row_of_step):
        base = (layer * 6 + which) * SUBLANES
        return pl.BlockSpec((1, 1, self.d), lambda *idx: (base + row_of_step(*idx), 0, 0))


def _rope_store(dst_ref, p, cos_ref, sin_lo_ref, sin_hi_ref, scale):
    cos, sin_lo, sin_hi = cos_ref[...], sin_lo_ref[...], sin_hi_ref[...]
    half = DIFF_QK_DIM // 4
    for c in range(p.shape[1] // LANES):
        x = p[:, c * LANES:(c + 1) * LANES]
        ahead = pltpu.roll(x, LANES - half, 1)
        behind = pltpu.roll(x, half, 1)
        r = x * cos + ahead * sin_lo + behind * sin_hi
        if scale != 1.0:
            r = r * scale
        dst_ref[:, c * LANES:(c + 1) * LANES] = r.astype(dst_ref.dtype)


def _inproj_kernel(x_ref, sh_ref, sc_ref, w_ref, cos_ref, sin_lo_ref, sin_hi_ref, *rest):
    gb_ref, g_ref, q_ref, k_ref, v_ref = rest[-5:]
    u = _modulated(x_ref[...], sh_ref, sc_ref)
    pc = jnp.dot(u, w_ref[:, 0:COL_CONV], preferred_element_type=F32)
    gb_ref[...] = pc[:, 0:D_CONV]
    g_ref[...] = pc[:, D_CONV:2 * D_CONV] * pc[:, 2 * D_CONV:3 * D_CONV]
    pq = jnp.dot(u, w_ref[:, Q0:K0], preferred_element_type=F32)
    _rope_store(q_ref, pq, cos_ref, sin_lo_ref, sin_hi_ref, Q_SCALE)
    pk = jnp.dot(u, w_ref[:, K0:V0], preferred_element_type=F32)
    _rope_store(k_ref.at[0], pk, cos_ref, sin_lo_ref, sin_hi_ref, 1.0)
    v_ref[0] = jnp.dot(u, w_ref[:, V0:V0 + D_ATTN], preferred_element_type=F32).astype(BF16)


def _inproj(x, mod, layer, row_of_step, w_bf, rope, tm, seq, kv_rows, kv_row0, kv_bufs=None):
    t, d = x.shape
    steps_per_seq = seq // tm
    tok = lambda i: (i, 0)
    pos = lambda i: (i % steps_per_seq, 0)
    kvi = lambda i: (i // steps_per_seq, kv_row0 // tm + i % steps_per_seq, 0)
    tab = pl.BlockSpec((tm, LANES), pos)
    out = lambda n: pl.BlockSpec((tm, n), tok)
    in_specs = [pl.BlockSpec((tm, d), tok), mod.spec(layer, 0, row_of_step),
                mod.spec(layer, 1, row_of_step), w_bf.spec(), tab, tab, tab]
    args = [x, mod.rows, mod.rows, w_bf.array, *rope]
    aliases = {}
    if kv_bufs is not None:
        in_specs += [pl.BlockSpec(memory_space=pl.ANY)] * 2
        aliases = {len(args): 3, len(args) + 1: 4}
        args += list(kv_bufs)
    nb = t // seq
    return pl.pallas_call(
        _inproj_kernel,
        grid=(t // tm,),
        in_specs=in_specs,
        out_specs=[out(D_CONV), out(D_CONV), out(COL_QK),
                   pl.BlockSpec((1, tm, COL_QK), kvi), pl.BlockSpec((1, tm, D_ATTN), kvi)],
        out_shape=[jax.ShapeDtypeStruct((t, D_CONV), F32), jax.ShapeDtypeStruct((t, D_CONV), F32),
                   jax.ShapeDtypeStruct((t, COL_QK), BF16),
                   jax.ShapeDtypeStruct((nb, kv_rows, COL_QK), BF16),
                   jax.ShapeDtypeStruct((nb, kv_rows, D_ATTN), BF16)],
        input_output_aliases=aliases,
        compiler_params=_params(1),
        name="inproj",
    )(*args)


def _attn_kernel(lq_ref, q_ref, k_ref, v_ref, sg_ref, o_ref, *, lam_init):
    lq = lq_ref[...]
    lam = (jnp.exp(jnp.sum(lq[0:1] * lq[1:2], axis=1, keepdims=True))
           - jnp.exp(jnp.sum(lq[2:3] * lq[3:4], axis=1, keepdims=True)) + lam_init)
    q = q_ref[...]
    tq = q.shape[0]
    lane = lax.broadcasted_iota(jnp.int32, q.shape, 1)
    zero = jnp.zeros_like(q)
    def component(qc):
        s = lax.dot_general(qc, k_ref[0], (((1,), (1,)), ((), ())), preferred_element_type=F32)
        e = jnp.exp2(s - jnp.max(s, axis=1, keepdims=True))
        return e, jnp.sum(e, axis=1, keepdims=True)

    e1, l1 = component(jnp.where(lane < DIFF_QK_DIM, q, zero))
    e2, l2 = component(jnp.where(lane >= DIFF_QK_DIM, q, zero))
    pd = (e1 * (1.0 / l1) - e2 * (lam / l2)).astype(BF16)
    o = jnp.dot(pd, v_ref[0], preferred_element_type=F32)
    y = o * lax.rsqrt(jnp.mean(o * o, axis=1, keepdims=True) + SUBLN_EPS)
    o_ref[...] = (y * sg_ref[...] * (1.0 - lam_init)).astype(o_ref.dtype)


def _attention(lq, q, k_all, v_all, sg, lam_init, tq, nk, key_block):
    t = q.shape[0]
    b = k_all.shape[0]
    nq = t // b // tq
    qo = pl.BlockSpec((tq, DIFF_V_DIM), lambda bi, h, i: (bi * nq + i, h))
    return pl.pallas_call(
        functools.partial(_attn_kernel, lam_init=lam_init),
        grid=(b, N_DIFF_HEADS, nq),
        in_specs=[pl.BlockSpec(lq.shape, lambda bi, h, i: (0, 0)), qo,
                  pl.BlockSpec((1, nk, 2 * DIFF_QK_DIM), lambda bi, h, i: (bi, key_block, h)),
                  pl.BlockSpec((1, nk, DIFF_V_DIM), lambda bi, h, i: (bi, key_block, h)),
                  pl.BlockSpec((1, DIFF_V_DIM), lambda bi, h, i: (0, 0))],
        out_specs=qo,
        out_shape=jax.ShapeDtypeStruct((t, D_ATTN), BF16),
        compiler_params=_params(3),
        name="diff_attention",
    )(lq, q, k_all, v_all, sg.reshape(1, DIFF_V_DIM))


MLP_CHUNK = 512


def _mlp_residual(h, mlp_refs, alpha):
    sh_ref, sc_ref, w1_ref, w2_ref, gate_ref, lg_ref, lb_ref = mlp_refs
    u = _modulated(h, sh_ref, sc_ref)
    y = jnp.zeros(h.shape, F32)
    for c in range(w1_ref.shape[1] // MLP_CHUNK):
        hid = jnp.dot(u, w1_ref[:, c * MLP_CHUNK:(c + 1) * MLP_CHUNK], preferred_element_type=F32)
        hid = jnp.square(jnp.maximum(hid, 0.0)).astype(BF16)
        y = y + jnp.dot(hid, w2_ref[c * MLP_CHUNK:(c + 1) * MLP_CHUNK, :], preferred_element_type=F32)
    return _post_norm(h, y, gate_ref, lg_ref, lb_ref, alpha)


class _MlpTail:
    def __init__(self, w1, w2, mod, layer, row_of_step, ln_g, ln_b):
        d = mod.d
        vec = pl.BlockSpec((1, d), lambda *_: (0, 0))
        self.specs = [mod.spec(layer, 3, row_of_step), mod.spec(layer, 4, row_of_step), w1.spec(),
                      w2.spec(), mod.spec(layer, 5, row_of_step), vec, vec]
        self.args = [mod.rows, mod.rows, w1.array, w2.array, mod.rows, ln_g.reshape(1, d),
                     ln_b.reshape(1, d)]


def _mix_out_kernel(h_ref, gb_ref, g_ref, gprev_ref, gnext_ref, a_ref, w_ref, cw_ref,
                    gate_ref, lg_ref, lb_ref, *rest, steps_per_seq, alpha):
    mlp_refs, o_ref = rest[:-1], rest[-1]
    i = pl.program_id(0)
    g = g_ref[...]
    tm = g.shape[0]
    row = lax.broadcasted_iota(jnp.int32, g.shape, 0)
    pos = i % steps_per_seq
    keep_prev = jnp.where(pos == 0, 0.0, 1.0)
    keep_next = jnp.where(pos == steps_per_seq - 1, 0.0, 1.0)
    before = jnp.where(row == 0, gprev_ref[SUBLANES - 1:SUBLANES, :] * keep_prev, pltpu.roll(g, 1, 0))
    after = jnp.where(row == tm - 1, gnext_ref[0:1, :] * keep_next, pltpu.roll(g, tm - 1, 0))
    cw = cw_ref[...]
    conv = before * cw[0:1] + g * cw[1:2] + after * cw[2:3]
    yc = (gb_ref[...] * conv).astype(BF16)
    y = (jnp.dot(yc, w_ref[0:D_CONV, :], preferred_element_type=F32)
         + jnp.dot(a_ref[...], w_ref[D_CONV:D_CONV + D_ATTN, :], preferred_element_type=F32))
    h1 = _post_norm(h_ref[...], y, gate_ref, lg_ref, lb_ref, alpha)
    o_ref[...] = _mlp_residual(h1, mlp_refs, alpha)


def _mix_out(h, gb, g, attn, w_bf, conv_w, mod, layer, row_of_step, ln_g, ln_b, mlp, tm, seq, alpha):
    t, d = h.shape
    steps_per_seq = seq // tm
    blocks8 = tm // SUBLANES
    last8 = t // SUBLANES - 1
    tok = lambda i: (i, 0)
    vec = pl.BlockSpec((1, d), lambda i: (0, 0))
    return pl.pallas_call(
        functools.partial(_mix_out_kernel, steps_per_seq=steps_per_seq, alpha=alpha),
        grid=(t // tm,),
        in_specs=[pl.BlockSpec((tm, d), tok), pl.BlockSpec((tm, D_CONV), tok),
                  pl.BlockSpec((tm, D_CONV), tok),
                  pl.BlockSpec((SUBLANES, D_CONV), lambda i: (jnp.maximum(i * blocks8 - 1, 0), 0)),
                  pl.BlockSpec((SUBLANES, D_CONV), lambda i: (jnp.minimum((i + 1) * blocks8, last8), 0)),
                  pl.BlockSpec((tm, D_ATTN), tok), w_bf.spec(),
                  pl.BlockSpec(conv_w.shape, lambda i: (0, 0)),
                  mod.spec(layer, 2, row_of_step), vec, vec] + mlp.specs,
        out_specs=pl.BlockSpec((tm, d), tok),
        out_shape=jax.ShapeDtypeStruct((t, d), F32),
        compiler_params=_params(1),
        name="mix_out_mlp",
    )(h, gb, g, g, g, attn, w_bf.array, conv_w, mod.rows, ln_g.reshape(1, d), ln_b.reshape(1, d),
      *mlp.args)


def _mlp_kernel(h_ref, *rest, alpha):
    rest[-1][...] = _mlp_residual(h_ref[...], rest[:-1], alpha)


def _mlp(h, mlp, tm, alpha):
    t, d = h.shape
    tok = lambda i: (i, 0)
    return pl.pallas_call(
        functools.partial(_mlp_kernel, alpha=alpha),
        grid=(t // tm,),
        in_specs=[pl.BlockSpec((tm, d), tok)] + mlp.specs,
        out_specs=pl.BlockSpec((tm, d), tok),
        out_shape=jax.ShapeDtypeStruct((t, d), F32),
        compiler_params=_params(1),
        name="mlp",
    )(h, *mlp.args)


def _fourier_in_kernel(x_ref, sh_ref, sc_ref, gcs_ref, z_ref):
    u = _modulated(x_ref[...], sh_ref, sc_ref)
    gcs = gcs_ref[...]
    for g in range(u.shape[1] // FOURIER_GROUP):
        lo, hi = g * FOURIER_GROUP, (g + 1) * FOURIER_GROUP
        zz = jnp.dot(u[:, lo:hi], gcs, preferred_element_type=F32)
        z_ref[0, 0, :, lo:hi] = zz[:, 0:FOURIER_GROUP].astype(BF16)
        z_ref[0, 1, :, lo:hi] = zz[:, FOURIER_GROUP:2 * FOURIER_GROUP].astype(BF16)


def _fourier_in(x, mod, layer, row_of_step, gcs, tm, seq):
    t, d = x.shape
    steps_per_seq = seq // tm
    return pl.pallas_call(
        _fourier_in_kernel,
        grid=(t // tm,),
        in_specs=[pl.BlockSpec((tm, d), lambda i: (i, 0)), mod.spec(layer, 0, row_of_step),
                  mod.spec(layer, 1, row_of_step), _const_spec(gcs.shape)],
        out_specs=pl.BlockSpec((1, 2, tm, d), lambda i: (i // steps_per_seq, 0, i % steps_per_seq, 0)),
        out_shape=jax.ShapeDtypeStruct((t // seq, 2, seq, d), BF16),
        compiler_params=_params(1),
        name="fourier_in",
    )(x, mod.rows, mod.rows, gcs)


FOURIER_SLAB = 16
FOURIER_OUT_SLAB = 8


def _fourier_rows_kernel(x_ref, sh_ref, sc_ref, k1_ref, gm_ref, z_ref):
    w, nb, d = x_ref.shape[1], x_ref.shape[2], x_ref.shape[3]
    rows = w * nb
    u = _modulated(x_ref[0].reshape(rows, d), sh_ref, sc_ref)
    y = jnp.dot(k1_ref[0], u, preferred_element_type=F32).astype(BF16)
    gm = gm_ref[...]
    for g in range(d // FOURIER_GROUP):
        lo, hi = g * FOURIER_GROUP, (g + 1) * FOURIER_GROUP
        yy = jnp.concatenate([y[0:rows, lo:hi], y[rows:2 * rows, lo:hi]], axis=1)
        zz = jnp.dot(yy, gm, preferred_element_type=F32).astype(BF16)
        z_ref[0, 0, :, :, lo:hi] = zz[:, 0:FOURIER_GROUP].reshape(w, nb, FOURIER_GROUP)
        z_ref[0, 1, :, :, lo:hi] = zz[:, FOURIER_GROUP:2 * FOURIER_GROUP].reshape(w, nb, FOURIER_GROUP)


def _fourier_rows(h, mod, layer, k1, gm, batch):
    t, d = h.shape
    w, nb = GRID_W, FOURIER_SLAB
    h4 = h.reshape(batch, w, w, d)
    return pl.pallas_call(
        _fourier_rows_kernel,
        grid=(w // nb, batch),
        in_specs=[pl.BlockSpec((1, w, nb, d), lambda j, bi: (bi, 0, j, 0)),
                  mod.spec(layer, 0, lambda j, bi: bi), mod.spec(layer, 1, lambda j, bi: bi),
                  pl.BlockSpec((1,) + k1.shape[1:], lambda j, bi: (j, 0, 0)), _const_spec(gm.shape)],
        out_specs=pl.BlockSpec((1, 2, w, nb, d), lambda j, bi: (bi, 0, 0, j, 0)),
        out_shape=jax.ShapeDtypeStruct((batch, 2, w, w, d), BF16),
        compiler_params=_params(2),
        name="fourier_rows",
    )(h4, mod.rows, mod.rows, k1, gm)


def _fourier_cols_kernel(z_ref, h_ref, k2_ref, w_ref, gate_ref, lg_ref, lb_ref, *rest, alpha):
    mlp_refs, o_ref = rest[:-1], rest[-1]
    nd, w, d = z_ref.shape[2], z_ref.shape[3], z_ref.shape[4]
    zz = z_ref[0].reshape(2 * nd * w, d)
    f = jnp.dot(k2_ref[...], zz, preferred_element_type=F32).astype(BF16)
    y = jnp.dot(f, w_ref[...], preferred_element_type=F32)
    h1 = _post_norm(h_ref[0].reshape(w * nd, d), y, gate_ref, lg_ref, lb_ref, alpha)
    o_ref[0] = _mlp_residual(h1, mlp_refs, alpha).reshape(w, nd, d)


def _fourier_cols(z, h, k2, w_bf, mod, layer, ln_g, ln_b, mlp, alpha):
    batch, _, w, _, d = z.shape
    nd = FOURIER_OUT_SLAB
    h4 = h.reshape(batch, w, w, d)
    vec = pl.BlockSpec((1, d), lambda bi, j: (0, 0))
    hblk = pl.BlockSpec((1, w, nd, d), lambda bi, j: (bi, 0, j, 0))
    out = pl.pallas_call(
        functools.partial(_fourier_cols_kernel, alpha=alpha),
        grid=(batch, w // nd),
        in_specs=[pl.BlockSpec((1, 2, nd, w, d), lambda bi, j: (bi, 0, j, 0, 0)), hblk,
                  _const_spec(k2.shape), w_bf.spec(),
                  mod.spec(layer, 2, lambda bi, j: bi), vec, vec] + mlp.specs,
        out_specs=hblk,
        out_shape=jax.ShapeDtypeStruct((batch, w, w, d), F32),
        compiler_params=_params(2),
        name="fourier_cols_mlp",
    )(z, h4, k2, w_bf.array, mod.rows, ln_g.reshape(1, d), ln_b.reshape(1, d), *mlp.args)
    return out.reshape(batch * w * w, d)


def _fourier_dense_kernel(z_ref, cs_ref, o_ref, *, norm):
    zc = jnp.concatenate([z_ref[0, 0], z_ref[0, 1]], axis=0)
    o_ref[0] = (jnp.dot(cs_ref[...], zc, preferred_element_type=F32) * norm).astype(BF16)


def _fourier_dense(z, cs, norm):
    b, _, seq, d = z.shape
    return pl.pallas_call(
        functools.partial(_fourier_dense_kernel, norm=norm),
        grid=(b,),
        in_specs=[pl.BlockSpec((1, 2, seq, d), lambda bi: (bi, 0, 0, 0)), _const_spec(cs.shape)],
        out_specs=pl.BlockSpec((1, seq, d), lambda bi: (bi, 0, 0)),
        out_shape=jax.ShapeDtypeStruct((b, seq, d), BF16),
        compiler_params=_params(1),
        name="fourier_dense",
    )(z, cs)


def _proj_out_kernel(h_ref, a_ref, w_ref, gate_ref, lg_ref, lb_ref, o_ref, *, alpha):
    y = jnp.dot(a_ref[...], w_ref[...], preferred_element_type=F32)
    o_ref[...] = _post_norm(h_ref[...], y, gate_ref, lg_ref, lb_ref, alpha)


def _proj_out(h, a, w_bf, mod, layer, row_of_step, ln_g, ln_b, tm, alpha):
    t, d = h.shape
    tok = lambda i: (i, 0)
    vec = pl.BlockSpec((1, d), lambda i: (0, 0))
    return pl.pallas_call(
        functools.partial(_proj_out_kernel, alpha=alpha),
        grid=(t // tm,),
        in_specs=[pl.BlockSpec((tm, d), tok), pl.BlockSpec((tm, a.shape[1]), tok),
                  w_bf.spec(), mod.spec(layer, 2, row_of_step), vec, vec],
        out_specs=pl.BlockSpec((tm, d), tok),
        out_shape=jax.ShapeDtypeStruct((t, d), F32),
        compiler_params=_params(1),
        name="proj_out",
    )(h, a, w_bf.array, mod.rows, ln_g.reshape(1, d), ln_b.reshape(1, d))


def _dft_cos_sin(n, denom):
    idx = np.arange(n, dtype=np.int64)
    ang = ((idx[:, None] * idx[None, :]) % denom) * (2.0 * math.pi / denom)
    return np.cos(ang), np.sin(ang)


def _bf16_const(a):
    return jnp.asarray(np.asarray(a, np.float32)).astype(BF16)


def _rope_tables(seq):
    half = DIFF_QK_DIM // 4
    t = np.arange(seq)
    inv = ROPE_BASE ** (-np.arange(half, dtype=np.float64) / half)
    lane = np.arange(LANES)
    use_col = (lane % DIFF_QK_DIM) >= DIFF_QK_DIM // 2
    second = (lane % (2 * half)) >= half
    pos = np.where(use_col[None, :], (t % GRID_W)[:, None], (t // GRID_W)[:, None])
    ang = pos * inv[lane % half][None, :]
    cos, sin = np.cos(ang), np.sin(ang)
    tabs = (cos, np.where(second[None, :], 0.0, -sin), np.where(second[None, :], sin, 0.0))
    return tuple(jnp.asarray(a, F32) for a in tabs)


def _fourier_tables(seq, norm):
    w, nb, nd = GRID_W, FOURIER_SLAB, FOURIER_OUT_SLAB
    idx = np.arange(w, dtype=np.int64)
    slab = np.arange(w // nb, dtype=np.int64)
    col = np.arange(nb, dtype=np.int64)
    t = idx[None, None, None, :] * w + slab[:, None, None, None] * nb + col[None, None, :, None]
    ang = ((idx[None, :, None, None] * t) % seq) * (2.0 * math.pi / seq)
    trig = np.stack([np.cos(ang), -np.sin(ang)], axis=1)
    k1 = trig[..., None] * np.eye(nb)[None, None, None, :, None, :]
    c64, s64 = _dft_cos_sin(w, w)
    cs = np.stack([c64, s64], axis=1) * norm
    k2 = cs[:, None, :, None, :] * np.eye(nd)[None, :, None, :, None]
    return (_bf16_const(k1.reshape(w // nb, 2 * w * nb, w * nb)),
            _bf16_const(k2.reshape(w * nd, 2 * nd * w)))


def _no_rope_tables(seq):
    return jnp.ones((seq, LANES), F32), jnp.zeros((seq, LANES), F32), jnp.zeros((seq, LANES), F32)


def kernel(x, c, ctx, c_ctx, ada_w, ada_b, ln_g, ln_b, mlp_w1, mlp_w2, w_in, conv_w, lambda_qk,
           subln_g, w_out_mix, w_out_fourier):
    batch, seq, d = x.shape
    ctx_len = ctx.shape[1]
    depth = ada_w.shape[0]
    alpha = (2.0 * depth) ** 0.25
    last_attn = 2 * ((depth - 1) // 2)
    tm = 512
    assert seq == GRID_W * GRID_W and seq % tm == 0 and batch < SUBLANES and ctx_len % SUBLANES == 0

    cond8 = jnp.zeros((SUBLANES, d), F32).at[:batch].set(c).at[batch].set(c_ctx)
    mod = _Mod(_modulation(cond8, ada_w, ada_b), d)
    lat_row = lambda i: i // (seq // tm)
    ctx_row = lambda i: batch

    w_in_bf = w_in.astype(BF16)
    w_mix_bf = w_out_mix.astype(BF16)
    w_four_bf = w_out_fourier.astype(BF16)
    w1_bf = mlp_w1.astype(BF16)
    w2_bf = mlp_w2.astype(BF16)

    rope = _rope_tables(seq)
    no_rope = _no_rope_tables(ctx_len)

    gc, gs = _dft_cos_sin(FOURIER_GROUP, FOURIER_GROUP)
    gcs = _bf16_const(np.concatenate([gc, -gs], axis=1))
    gmat = _bf16_const(np.block([[gc, -gs], [gs, gc]]))
    k1, k2 = _fourier_tables(seq, (seq * FOURIER_GROUP) ** -0.5)
    cc, sc_ = _dft_cos_sin(ctx_len, ctx_len)
    cs_ctx = _bf16_const(np.concatenate([cc, sc_], axis=1))

    h = x.reshape(batch * seq, d)
    hc = ctx.reshape(batch * ctx_len, d)

    for i in range(depth):
        j = i // 2
        update_ctx = i < last_attn
        w1, w2 = _Stacked(w1_bf, i), _Stacked(w2_bf, i)
        mlp_ctx = _MlpTail(w1, w2, mod, i, ctx_row, ln_g[i, 1], ln_b[i, 1])
        if i % 2 == 0:
            lam_init = 0.8 - 0.6 * math.exp(-0.3 * i)
            with_ctx = update_ctx or i == last_attn
            kv_rows = seq + ctx_len if with_ctx else seq
            w_in_j, w_mix_j = _Stacked(w_in_bf, j), _Stacked(w_mix_bf, j)
            gb, g, q, k_all, v_all = _inproj(h, mod, i, lat_row, w_in_j, rope, tm, seq, kv_rows, 0)
            if with_ctx:
                gbc, gcx, qc, k_all, v_all = _inproj(hc, mod, i, ctx_row, w_in_j, no_rope, ctx_len,
                                                     ctx_len, kv_rows, seq, (k_all, v_all))
            attn = _attention(lambda_qk[j], q, k_all, v_all, subln_g[j], lam_init, 256, kv_rows, 0)
            mlp_lat = _MlpTail(w1, w2, mod, i, lat_row, ln_g[i, 1], ln_b[i, 1])
            h = _mix_out(h, gb, g, attn, w_mix_j, conv_w[j], mod, i, lat_row,
                         ln_g[i, 0], ln_b[i, 0], mlp_lat, tm, seq, alpha)
            if update_ctx:
                attn_c = _attention(lambda_qk[j], qc, k_all, v_all, subln_g[j], lam_init, ctx_len,
                                    ctx_len, seq // ctx_len)
                hc = _mix_out(hc, gbc, gcx, attn_c, w_mix_j, conv_w[j], mod, i, ctx_row,
                              ln_g[i, 0], ln_b[i, 0], mlp_ctx, ctx_len, ctx_len, alpha)
        else:
            w_four_j = _Stacked(w_four_bf, j)
            z = _fourier_rows(h, mod, i, k1, gmat, batch)
            mlp_slab = _MlpTail(w1, w2, mod, i, lambda bi, sl: bi, ln_g[i, 1], ln_b[i, 1])
            h = _fourier_cols(z, h, k2, w_four_j, mod, i, ln_g[i, 0], ln_b[i, 0], mlp_slab, alpha)
            if update_ctx:
                zc = _fourier_in(hc, mod, i, ctx_row, gcs, ctx_len, ctx_len)
                fc = _fourier_dense(zc, cs_ctx, (ctx_len * FOURIER_GROUP) ** -0.5)
                hc1 = _proj_out(hc, fc.reshape(batch * ctx_len, d), w_four_j, mod, i, ctx_row,
                                ln_g[i, 0], ln_b[i, 0], ctx_len, alpha)
                hc = _mlp(hc1, mlp_ctx, ctx_len, alpha)
    return h.reshape(batch, seq, d)
```

```python
import functools
import math

import jax
import jax.numpy as jnp
import numpy as np
from jax import lax
from jax.experimental import pallas as pl
from jax.experimental.pallas import tpu as pltpu

F32 = jnp.float32
BF16 = jnp.bfloat16

GRID_W = 64
CONV_WIDTH = 3
D_CONV = 512
N_DIFF_HEADS = 4
DIFF_QK_DIM = 64
DIFF_V_DIM = 128
D_ATTN = N_DIFF_HEADS * DIFF_V_DIM
FOURIER_GROUP = 128
ROPE_BASE = 10000.0
LN_EPS = 1e-6
SUBLN_EPS = 1e-5
COL_CONV = 3 * D_CONV
COL_QK = N_DIFF_HEADS * 2 * DIFF_QK_DIM
Q0 = COL_CONV
K0 = Q0 + COL_QK
V0 = K0 + COL_QK
Q_SCALE = math.log2(math.e) * DIFF_QK_DIM ** -0.5

LANES = 128
SUBLANES = 8
VMEM_LIMIT = 56 * 1024 * 1024


def _params(n_axes):
    return pltpu.CompilerParams(
        dimension_semantics=("arbitrary",) * n_axes, vmem_limit_bytes=VMEM_LIMIT)


def _const_spec(shape):
    nd = len(shape)
    return pl.BlockSpec(shape, lambda *_: (0,) * nd, pipeline_mode=pl.Buffered(1))


class _Stacked:
    def __init__(self, array, index):
        self.array, self.index = array, index

    def spec(self):
        k, n = self.array.shape[1:]
        index = self.index
        return pl.BlockSpec((None, k, n), lambda *_: (index, 0, 0), pipeline_mode=pl.Buffered(1))


def _layer_norm(x):
    mu = jnp.mean(x, axis=-1, keepdims=True)
    xc = x - mu
    var = jnp.mean(xc * xc, axis=-1, keepdims=True)
    return xc * lax.rsqrt(var + LN_EPS)


def _modulated(x, shift_ref, scale_ref):
    return (_layer_norm(x) * (1.0 + scale_ref[0]) + shift_ref[0]).astype(BF16)


def _post_norm(h, y, gate_ref, g_ref, b_ref, alpha):
    return _layer_norm(alpha * h + gate_ref[0] * y) * g_ref[...] + b_ref[...]


def _mod_kernel(cond_ref, w_ref, b_ref, o_ref):
    c = cond_ref[...]
    a = (c * (1.0 / (1.0 + jnp.exp(-c)))).astype(BF16)
    o_ref[0] = jnp.dot(a, w_ref[0].astype(BF16), preferred_element_type=F32) + b_ref[0]


def _modulation(cond8, ada_w, ada_b):
    depth, d, n6 = ada_w.shape
    tn = 1536
    return pl.pallas_call(
        _mod_kernel,
        grid=(depth, n6 // tn),
        in_specs=[
            pl.BlockSpec((SUBLANES, d), lambda l, j: (0, 0)),
            pl.BlockSpec((1, d, tn), lambda l, j: (l, 0, j)),
            pl.BlockSpec((1, 1, tn), lambda l, j: (l, 0, j)),
        ],
        out_specs=pl.BlockSpec((1, SUBLANES, tn), lambda l, j: (l, 0, j)),
        out_shape=jax.ShapeDtypeStruct((depth, SUBLANES, n6), F32),
        compiler_params=_params(2),
        name="modulation",
    )(cond8, ada_w, ada_b.reshape(depth, 1, n6))


class _Mod:
    def __init__(self, table, d):
        depth = table.shape[0]
        self.rows = table.reshape(depth, SUBLANES, 6, d).transpose(0, 2, 1, 3).reshape(
            depth * 6 * SUBLANES, 1, d)
        self.d = d

    def spec(self, layer, which, row_of_step):
        base = (layer * 6 + which) * SUBLANES
        return pl.BlockSpec((1, 1, self.d), lambda *idx: (base + row_of_step(*idx), 0, 0))


def _rope_store(dst_ref, p, cos_ref, sin_lo_ref, sin_hi_ref, scale):
    cos, sin_lo, sin_hi = cos_ref[...], sin_lo_ref[...], sin_hi_ref[...]
    half = DIFF_QK_DIM // 4
    for c in range(p.shape[1] // LANES):
        x = p[:, c * LANES:(c + 1) * LANES]
        ahead = pltpu.roll(x, LANES - half, 1)
        behind = pltpu.roll(x, half, 1)
        r = x * cos + ahead * sin_lo + behind * sin_hi
        if scale != 1.0:
            r = r * scale
        dst_ref[:, c * LANES:(c + 1) * LANES] = r.astype(dst_ref.dtype)


def _inproj_kernel(x_ref, sh_ref, sc_ref, w_ref, cos_ref, sin_lo_ref, sin_hi_ref,
                   gb_ref, g_ref, q_ref, k_ref, v_ref):
    u = _modulated(x_ref[...], sh_ref, sc_ref)
    pc = jnp.dot(u, w_ref[:, 0:COL_CONV], preferred_element_type=F32)
    gb_ref[...] = pc[:, 0:D_CONV]
    g_ref[...] = pc[:, D_CONV:2 * D_CONV] * pc[:, 2 * D_CONV:3 * D_CONV]
    pq = jnp.dot(u, w_ref[:, Q0:K0], preferred_element_type=F32)
    _rope_store(q_ref, pq, cos_ref, sin_lo_ref, sin_hi_ref, Q_SCALE)
    pk = jnp.dot(u, w_ref[:, K0:V0], preferred_element_type=F32)
    _rope_store(k_ref.at[0], pk, cos_ref, sin_lo_ref, sin_hi_ref, 1.0)
    v_ref[0] = jnp.dot(u, w_ref[:, V0:V0 + D_ATTN], preferred_element_type=F32).astype(BF16)


def _inproj(x, mod, layer, row_of_step, w_bf, rope, tm, seq):
    t, d = x.shape
    steps_per_seq = seq // tm
    tok = lambda i: (i, 0)
    kvi = lambda i: (i // steps_per_seq, i % steps_per_seq, 0)
    tab = pl.BlockSpec((tm, LANES), lambda i: (i % steps_per_seq, 0))
    out = lambda n: pl.BlockSpec((tm, n), tok)
    return pl.pallas_call(
        _inproj_kernel,
        grid=(t // tm,),
        in_specs=[pl.BlockSpec((tm, d), tok), mod.spec(layer, 0, row_of_step),
                  mod.spec(layer, 1, row_of_step), w_bf.spec(), tab, tab, tab],
        out_specs=[out(D_CONV), out(D_CONV), out(COL_QK),
                   pl.BlockSpec((1, tm, COL_QK), kvi), pl.BlockSpec((1, tm, D_ATTN), kvi)],
        out_shape=[jax.ShapeDtypeStruct((t, D_CONV), F32), jax.ShapeDtypeStruct((t, D_CONV), F32),
                   jax.ShapeDtypeStruct((t, COL_QK), BF16),
                   jax.ShapeDtypeStruct((t // seq, seq, COL_QK), BF16),
                   jax.ShapeDtypeStruct((t // seq, seq, D_ATTN), BF16)],
        compiler_params=_params(1),
        name="inproj",
    )(x, mod.rows, mod.rows, w_bf.array, *rope)


def _attn_kernel(lq_ref, q_ref, *refs, lam_init, n_src, sub):
    k_refs, v_refs = refs[:n_src], refs[n_src:2 * n_src]
    sg_ref, o_ref = refs[2 * n_src], refs[2 * n_src + 1]
    lq = lq_ref[...]
    lam = (jnp.exp(jnp.sum(lq[0:1] * lq[1:2], axis=1, keepdims=True))
           - jnp.exp(jnp.sum(lq[2:3] * lq[3:4], axis=1, keepdims=True)) + lam_init)
    nt = (((1,), (1,)), ((), ()))

    def scores(qc):
        return [lax.dot_general(qc, k[0], nt, preferred_element_type=F32) for k in k_refs]

    def softmax_parts(s):
        m = functools.reduce(jnp.maximum, [jnp.max(x, axis=1, keepdims=True) for x in s])
        e = [jnp.exp2(x - m) for x in s]
        return e, sum(jnp.sum(x, axis=1, keepdims=True) for x in e)

    def score_tile(r):
        q = q_ref[r * sub:(r + 1) * sub, :]
        lane = lax.broadcasted_iota(jnp.int32, q.shape, 1)
        zero = jnp.zeros_like(q)
        return (scores(jnp.where(lane < DIFF_QK_DIM, q, zero)),
                scores(jnp.where(lane >= DIFF_QK_DIM, q, zero)))

    def project(r, e1, l1, e2, l2):
        ratio = lam * l1 / l2
        o = sum(jnp.dot((a - ratio * b).astype(BF16), v[0], preferred_element_type=F32)
                for a, b, v in zip(e1, e2, v_refs)) / l1
        y = o * lax.rsqrt(jnp.mean(o * o, axis=1, keepdims=True) + SUBLN_EPS)
        o_ref[r * sub:(r + 1) * sub, :] = (y * sg_ref[...] * (1.0 - lam_init)).astype(o_ref.dtype)

    n_sub = q_ref.shape[0] // sub
    s_next = score_tile(0)
    parts = None
    for r in range(n_sub):
        s1, s2 = s_next
        cur = softmax_parts(s1) + softmax_parts(s2)
        if r + 1 < n_sub:
            s_next = score_tile(r + 1)
        if parts is not None:
            project(r - 1, *parts)
        parts = cur
    project(n_sub - 1, *parts)


def _attention(lq, q, ks, vs, sg, lam_init, tq, sub):
    t = q.shape[0]
    b = ks[0].shape[0]
    nq = t // b // tq
    qo = pl.BlockSpec((tq, DIFF_V_DIM), lambda bi, h, i: (bi * nq + i, h))
    src = lambda a: pl.BlockSpec((1, a.shape[1], DIFF_V_DIM), lambda bi, h, i: (bi, 0, h))
    return pl.pallas_call(
        functools.partial(_attn_kernel, lam_init=lam_init, n_src=len(ks), sub=sub),
        grid=(b, N_DIFF_HEADS, nq),
        in_specs=[pl.BlockSpec(lq.shape, lambda bi, h, i: (0, 0)), qo] + [src(a) for a in ks + vs]
                 + [pl.BlockSpec((1, DIFF_V_DIM), lambda bi, h, i: (0, 0))],
        out_specs=qo,
        out_shape=jax.ShapeDtypeStruct((t, D_ATTN), BF16),
        compiler_params=_params(3),
        name="diff_attention",
    )(lq, q, *ks, *vs, sg.reshape(1, DIFF_V_DIM))


MLP_CHUNK = 512


def _mlp_residual(h, mlp_refs, alpha):
    sh_ref, sc_ref, w1_ref, w2_ref, gate_ref, lg_ref, lb_ref = mlp_refs
    u = _modulated(h, sh_ref, sc_ref)
    y = jnp.zeros(h.shape, F32)
    for c in range(w1_ref.shape[1] // MLP_CHUNK):
        hid = jnp.dot(u, w1_ref[:, c * MLP_CHUNK:(c + 1) * MLP_CHUNK], preferred_element_type=F32)
        hid = jnp.square(jnp.maximum(hid, 0.0)).astype(BF16)
        y = y + jnp.dot(hid, w2_ref[c * MLP_CHUNK:(c + 1) * MLP_CHUNK, :], preferred_element_type=F32)
    return _post_norm(h, y, gate_ref, lg_ref, lb_ref, alpha)


class _MlpTail:
    def __init__(self, w1, w2, mod, layer, row_of_step, ln_g, ln_b):
        d = mod.d
        vec = pl.BlockSpec((1, d), lambda *_: (0, 0))
        self.specs = [mod.spec(layer, 3, row_of_step), mod.spec(layer, 4, row_of_step), w1.spec(),
                      w2.spec(), mod.spec(layer, 5, row_of_step), vec, vec]
        self.args = [mod.rows, mod.rows, w1.array, w2.array, mod.rows, ln_g.reshape(1, d),
                     ln_b.reshape(1, d)]


def _mix_out_kernel(h_ref, gb_ref, g_ref, gprev_ref, gnext_ref, a_ref, w_ref, cw_ref,
                    gate_ref, lg_ref, lb_ref, *rest, steps_per_seq, alpha):
    mlp_refs, o_ref = rest[:-1], rest[-1]
    i = pl.program_id(0)
    g = g_ref[...]
    tm = g.shape[0]
    row = lax.broadcasted_iota(jnp.int32, g.shape, 0)
    pos = i % steps_per_seq
    keep_prev = jnp.where(pos == 0, 0.0, 1.0)
    keep_next = jnp.where(pos == steps_per_seq - 1, 0.0, 1.0)
    before = jnp.where(row == 0, gprev_ref[SUBLANES - 1:SUBLANES, :] * keep_prev, pltpu.roll(g, 1, 0))
    after = jnp.where(row == tm - 1, gnext_ref[0:1, :] * keep_next, pltpu.roll(g, tm - 1, 0))
    cw = cw_ref[...]
    conv = before * cw[0:1] + g * cw[1:2] + after * cw[2:3]
    yc = (gb_ref[...] * conv).astype(BF16)
    y = (jnp.dot(yc, w_ref[0:D_CONV, :], preferred_element_type=F32)
         + jnp.dot(a_ref[...], w_ref[D_CONV:D_CONV + D_ATTN, :], preferred_element_type=F32))
    h1 = _post_norm(h_ref[...], y, gate_ref, lg_ref, lb_ref, alpha)
    o_ref[...] = _mlp_residual(h1, mlp_refs, alpha)


def _mix_out(h, gb, g, attn, w_bf, conv_w, mod, layer, row_of_step, ln_g, ln_b, mlp, tm, seq, alpha):
    t, d = h.shape
    steps_per_seq = seq // tm
    blocks8 = tm // SUBLANES
    last8 = t // SUBLANES - 1
    tok = lambda i: (i, 0)
    vec = pl.BlockSpec((1, d), lambda i: (0, 0))
    return pl.pallas_call(
        functools.partial(_mix_out_kernel, steps_per_seq=steps_per_seq, alpha=alpha),
        grid=(t // tm,),
        in_specs=[pl.BlockSpec((tm, d), tok), pl.BlockSpec((tm, D_CONV), tok),
                  pl.BlockSpec((tm, D_CONV), tok),
                  pl.BlockSpec((SUBLANES, D_CONV), lambda i: (jnp.maximum(i * blocks8 - 1, 0), 0)),
                  pl.BlockSpec((SUBLANES, D_CONV), lambda i: (jnp.minimum((i + 1) * blocks8, last8), 0)),
                  pl.BlockSpec((tm, D_ATTN), tok), w_bf.spec(),
                  pl.BlockSpec(conv_w.shape, lambda i: (0, 0)),
                  mod.spec(layer, 2, row_of_step), vec, vec] + mlp.specs,
        out_specs=pl.BlockSpec((tm, d), tok),
        out_shape=jax.ShapeDtypeStruct((t, d), F32),
        compiler_params=_params(1),
        name="mix_out_mlp",
    )(h, gb, g, g, g, attn, w_bf.array, conv_w, mod.rows, ln_g.reshape(1, d), ln_b.reshape(1, d),
      *mlp.args)


def _mlp_kernel(h_ref, *rest, alpha):
    rest[-1][...] = _mlp_residual(h_ref[...], rest[:-1], alpha)


def _mlp(h, mlp, tm, alpha):
    t, d = h.shape
    tok = lambda i: (i, 0)
    return pl.pallas_call(
        functools.partial(_mlp_kernel, alpha=alpha),
        grid=(t // tm,),
        in_specs=[pl.BlockSpec((tm, d), tok)] + mlp.specs,
        out_specs=pl.BlockSpec((tm, d), tok),
        out_shape=jax.ShapeDtypeStruct((t, d), F32),
        compiler_params=_params(1),
        name="mlp",
    )(h, *mlp.args)


def _fourier_in_kernel(x_ref, sh_ref, sc_ref, gcs_ref, z_ref):
    u = _modulated(x_ref[...], sh_ref, sc_ref)
    gcs = gcs_ref[...]
    for g in range(u.shape[1] // FOURIER_GROUP):
        lo, hi = g * FOURIER_GROUP, (g + 1) * FOURIER_GROUP
        zz = jnp.dot(u[:, lo:hi], gcs, preferred_element_type=F32)
        z_ref[0, 0, :, lo:hi] = zz[:, 0:FOURIER_GROUP].astype(BF16)
        z_ref[0, 1, :, lo:hi] = zz[:, FOURIER_GROUP:2 * FOURIER_GROUP].astype(BF16)


def _fourier_in(x, mod, layer, row_of_step, gcs, tm, seq):
    t, d = x.shape
    steps_per_seq = seq // tm
    return pl.pallas_call(
        _fourier_in_kernel,
        grid=(t // tm,),
        in_specs=[pl.BlockSpec((tm, d), lambda i: (i, 0)), mod.spec(layer, 0, row_of_step),
                  mod.spec(layer, 1, row_of_step), _const_spec(gcs.shape)],
        out_specs=pl.BlockSpec((1, 2, tm, d), lambda i: (i // steps_per_seq, 0, i % steps_per_seq, 0)),
        out_shape=jax.ShapeDtypeStruct((t // seq, 2, seq, d), BF16),
        compiler_params=_params(1),
        name="fourier_in",
    )(x, mod.rows, mod.rows, gcs)


FOURIER_SLAB = 16
FOURIER_OUT_SLAB = 8


def _fourier_rows_kernel(x_ref, sh_ref, sc_ref, k1_ref, gm_ref, z_ref):
    w, nb, d = x_ref.shape[1], x_ref.shape[2], x_ref.shape[3]
    rows = w * nb
    u = _modulated(x_ref[0].reshape(rows, d), sh_ref, sc_ref)
    y = jnp.dot(k1_ref[0], u, preferred_element_type=F32).astype(BF16)
    gm = gm_ref[...]
    for g in range(d // FOURIER_GROUP):
        lo, hi = g * FOURIER_GROUP, (g + 1) * FOURIER_GROUP
        yy = jnp.concatenate([y[0:rows, lo:hi], y[rows:2 * rows, lo:hi]], axis=1)
        zz = jnp.dot(yy, gm, preferred_element_type=F32).astype(BF16)
        z_ref[0, 0, :, :, lo:hi] = zz[:, 0:FOURIER_GROUP].reshape(w, nb, FOURIER_GROUP)
        z_ref[0, 1, :, :, lo:hi] = zz[:, FOURIER_GROUP:2 * FOURIER_GROUP].reshape(w, nb, FOURIER_GROUP)


def _fourier_rows(h, mod, layer, k1, gm, batch):
    t, d = h.shape
    w, nb = GRID_W, FOURIER_SLAB
    h4 = h.reshape(batch, w, w, d)
    return pl.pallas_call(
        _fourier_rows_kernel,
        grid=(w // nb, batch),
        in_specs=[pl.BlockSpec((1, w, nb, d), lambda j, bi: (bi, 0, j, 0)),
                  mod.spec(layer, 0, lambda j, bi: bi), mod.spec(layer, 1, lambda j, bi: bi),
                  pl.BlockSpec((1,) + k1.shape[1:], lambda j, bi: (j, 0, 0)), _const_spec(gm.shape)],
        out_specs=pl.BlockSpec((1, 2, w, nb, d), lambda j, bi: (bi, 0, 0, j, 0)),
        out_shape=jax.ShapeDtypeStruct((batch, 2, w, w, d), BF16),
        compiler_params=_params(2),
        name="fourier_rows",
    )(h4, mod.rows, mod.rows, k1, gm)


def _fourier_cols_kernel(z_ref, h_ref, k2_ref, w_ref, gate_ref, lg_ref, lb_ref, *rest, alpha):
    mlp_refs, o_ref = rest[:-1], rest[-1]
    nd, w, d = z_ref.shape[2], z_ref.shape[3], z_ref.shape[4]
    zz = z_ref[0].reshape(2 * nd * w, d)
    f = jnp.dot(k2_ref[...], zz, preferred_element_type=F32).astype(BF16)
    y = jnp.dot(f, w_ref[...], preferred_element_type=F32)
    h1 = _post_norm(h_ref[0].reshape(w * nd, d), y, gate_ref, lg_ref, lb_ref, alpha)
    o_ref[0] = _mlp_residual(h1, mlp_refs, alpha).reshape(w, nd, d)


def _fourier_cols(z, h, k2, w_bf, mod, layer, ln_g, ln_b, mlp, alpha):
    batch, _, w, _, d = z.shape
    nd = FOURIER_OUT_SLAB
    h4 = h.reshape(batch, w, w, d)
    vec = pl.BlockSpec((1, d), lambda bi, j: (0, 0))
    hblk = pl.BlockSpec((1, w, nd, d), lambda bi, j: (bi, 0, j, 0))
    out = pl.pallas_call(
        functools.partial(_fourier_cols_kernel, alpha=alpha),
        grid=(batch, w // nd),
        in_specs=[pl.BlockSpec((1, 2, nd, w, d), lambda bi, j: (bi, 0, j, 0, 0)), hblk,
                  _const_spec(k2.shape), w_bf.spec(),
                  mod.spec(layer, 2, lambda bi, j: bi), vec, vec] + mlp.specs,
        out_specs=hblk,
        out_shape=jax.ShapeDtypeStruct((batch, w, w, d), F32),
        compiler_params=_params(2),
        name="fourier_cols_mlp",
    )(z, h4, k2, w_bf.array, mod.rows, ln_g.reshape(1, d), ln_b.reshape(1, d), *mlp.args)
    return out.reshape(batch * w * w, d)


def _fourier_dense_kernel(z_ref, cs_ref, o_ref, *, norm):
    zc = jnp.concatenate([z_ref[0, 0], z_ref[0, 1]], axis=0)
    o_ref[0] = (jnp.dot(cs_ref[...], zc, preferred_element_type=F32) * norm).astype(BF16)


def _fourier_dense(z, cs, norm):
    b, _, seq, d = z.shape
    return pl.pallas_call(
        functools.partial(_fourier_dense_kernel, norm=norm),
        grid=(b,),
        in_specs=[pl.BlockSpec((1, 2, seq, d), lambda bi: (bi, 0, 0, 0)), _const_spec(cs.shape)],
        out_specs=pl.BlockSpec((1, seq, d), lambda bi: (bi, 0, 0)),
        out_shape=jax.ShapeDtypeStruct((b, seq, d), BF16),
        compiler_params=_params(1),
        name="fourier_dense",
    )(z, cs)


def _proj_out_kernel(h_ref, a_ref, w_ref, gate_ref, lg_ref, lb_ref, o_ref, *, alpha):
    y = jnp.dot(a_ref[...], w_ref[...], preferred_element_type=F32)
    o_ref[...] = _post_norm(h_ref[...], y, gate_ref, lg_ref, lb_ref, alpha)


def _proj_out(h, a, w_bf, mod, layer, row_of_step, ln_g, ln_b, tm, alpha):
    t, d = h.shape
    tok = lambda i: (i, 0)
    vec = pl.BlockSpec((1, d), lambda i: (0, 0))
    return pl.pallas_call(
        functools.partial(_proj_out_kernel, alpha=alpha),
        grid=(t // tm,),
        in_specs=[pl.BlockSpec((tm, d), tok), pl.BlockSpec((tm, a.shape[1]), tok),
                  w_bf.spec(), mod.spec(layer, 2, row_of_step), vec, vec],
        out_specs=pl.BlockSpec((tm, d), tok),
        out_shape=jax.ShapeDtypeStruct((t, d), F32),
        compiler_params=_params(1),
        name="proj_out",
    )(h, a, w_bf.array, mod.rows, ln_g.reshape(1, d), ln_b.reshape(1, d))


def _dft_cos_sin(n, denom):
    idx = np.arange(n, dtype=np.int64)
    ang = ((idx[:, None] * idx[None, :]) % denom) * (2.0 * math.pi / denom)
    return np.cos(ang), np.sin(ang)


def _bf16_const(a):
    return jnp.asarray(np.asarray(a, np.float32)).astype(BF16)


def _rope_tables(seq):
    half = DIFF_QK_DIM // 4
    t = np.arange(seq)
    inv = ROPE_BASE ** (-np.arange(half, dtype=np.float64) / half)
    lane = np.arange(LANES)
    use_col = (lane % DIFF_QK_DIM) >= DIFF_QK_DIM // 2
    second = (lane % (2 * half)) >= half
    pos = np.where(use_col[None, :], (t % GRID_W)[:, None], (t // GRID_W)[:, None])
    ang = pos * inv[lane % half][None, :]
    cos, sin = np.cos(ang), np.sin(ang)
    tabs = (cos, np.where(second[None, :], 0.0, -sin), np.where(second[None, :], sin, 0.0))
    return tuple(jnp.asarray(a, F32) for a in tabs)


def _fourier_tables(seq, norm):
    w, nb, nd = GRID_W, FOURIER_SLAB, FOURIER_OUT_SLAB
    idx = np.arange(w, dtype=np.int64)
    slab = np.arange(w // nb, dtype=np.int64)
    col = np.arange(nb, dtype=np.int64)
    t = idx[None, None, None, :] * w + slab[:, None, None, None] * nb + col[None, None, :, None]
    ang = ((idx[None, :, None, None] * t) % seq) * (2.0 * math.pi / seq)
    trig = np.stack([np.cos(ang), -np.sin(ang)], axis=1)
    k1 = trig[..., None] * np.eye(nb)[None, None, None, :, None, :]
    c64, s64 = _dft_cos_sin(w, w)
    cs = np.stack([c64, s64], axis=1) * norm
    k2 = cs[:, None, :, None, :] * np.eye(nd)[None, :, None, :, None]
    return (_bf16_const(k1.reshape(w // nb, 2 * w * nb, w * nb)),
            _bf16_const(k2.reshape(w * nd, 2 * nd * w)))


def _no_rope_tables(seq):
    return jnp.ones((seq, LANES), F32), jnp.zeros((seq, LANES), F32), jnp.zeros((seq, LANES), F32)


def kernel(x, c, ctx, c_ctx, ada_w, ada_b, ln_g, ln_b, mlp_w1, mlp_w2, w_in, conv_w, lambda_qk,
           subln_g, w_out_mix, w_out_fourier):
    batch, seq, d = x.shape
    ctx_len = ctx.shape[1]
    depth = ada_w.shape[0]
    alpha = (2.0 * depth) ** 0.25
    last_attn = 2 * ((depth - 1) // 2)
    tm = 512
    assert seq == GRID_W * GRID_W and seq % tm == 0 and batch < SUBLANES and ctx_len % SUBLANES == 0

    cond8 = jnp.zeros((SUBLANES, d), F32).at[:batch].set(c).at[batch].set(c_ctx)
    mod = _Mod(_modulation(cond8, ada_w, ada_b), d)
    lat_row = lambda i: i // (seq // tm)
    ctx_row = lambda i: batch

    w_in_bf = w_in.astype(BF16)
    w_mix_bf = w_out_mix.astype(BF16)
    w_four_bf = w_out_fourier.astype(BF16)
    w1_bf = mlp_w1.astype(BF16)
    w2_bf = mlp_w2.astype(BF16)

    rope = _rope_tables(seq)
    no_rope = _no_rope_tables(ctx_len)

    gc, gs = _dft_cos_sin(FOURIER_GROUP, FOURIER_GROUP)
    gcs = _bf16_const(np.concatenate([gc, -gs], axis=1))
    gmat = _bf16_const(np.block([[gc, -gs], [gs, gc]]))
    k1, k2 = _fourier_tables(seq, (seq * FOURIER_GROUP) ** -0.5)
    cc, sc_ = _dft_cos_sin(ctx_len, ctx_len)
    cs_ctx = _bf16_const(np.concatenate([cc, sc_], axis=1))

    h = x.reshape(batch * seq, d)
    hc = ctx.reshape(batch * ctx_len, d)

    for i in range(depth):
        j = i // 2
        update_ctx = i < last_attn
        w1, w2 = _Stacked(w1_bf, i), _Stacked(w2_bf, i)
        mlp_ctx = _MlpTail(w1, w2, mod, i, ctx_row, ln_g[i, 1], ln_b[i, 1])
        if i % 2 == 0:
            lam_init = 0.8 - 0.6 * math.exp(-0.3 * i)
            w_in_j, w_mix_j = _Stacked(w_in_bf, j), _Stacked(w_mix_bf, j)
            gb, g, q, k, v = _inproj(h, mod, i, lat_row, w_in_j, rope, tm, seq)
            gbc, gcx, qc, kc, vc = _inproj(hc, mod, i, ctx_row, w_in_j, no_rope, ctx_len, ctx_len)
            attn = _attention(lambda_qk[j], q, [k, kc], [v, vc], subln_g[j], lam_init, 512, 256)
            mlp_lat = _MlpTail(w1, w2, mod, i, lat_row, ln_g[i, 1], ln_b[i, 1])
            h = _mix_out(h, gb, g, attn, w_mix_j, conv_w[j], mod, i, lat_row,
                         ln_g[i, 0], ln_b[i, 0], mlp_lat, tm, seq, alpha)
            if update_ctx:
                attn_c = _attention(lambda_qk[j], qc, [kc], [vc], subln_g[j], lam_init, ctx_len, ctx_len)
                hc = _mix_out(hc, gbc, gcx, attn_c, w_mix_j, conv_w[j], mod, i, ctx_row,
                              ln_g[i, 0], ln_b[i, 0], mlp_ctx, ctx_len, ctx_len, alpha)
        else:
            w_four_j = _Stacked(w_four_bf, j)
            z = _fourier_rows(h, mod, i, k1, gmat, batch)
            mlp_slab = _MlpTail(w1, w2, mod, i, lambda bi, sl: bi, ln_g[i, 1], ln_b[i, 1])
            h = _fourier_cols(z, h, k2, w_four_j, mod, i, ln_g[i, 0], ln_b[i, 0], mlp_slab, alpha)
            if update_ctx:
                zc = _fourier_in(hc, mod, i, ctx_row, gcs, ctx_len, ctx_len)
                fc = _fourier_dense(zc, cs_ctx, (ctx_len * FOURIER_GROUP) ** -0.5)
                hc1 = _proj_out(hc, fc.reshape(batch * ctx_len, d), w_four_j, mod, i, ctx_row,
                                ln_g[i, 0], ln_b[i, 0], ctx_len, alpha)
                hc = _mlp(hc1, mlp_ctx, ctx_len, alpha)
    return h.reshape(batch, seq, d)
```

```python
import functools
import math

import jax
import jax.numpy as jnp
import numpy as np
from jax import lax
from jax.experimental import pallas as pl
from jax.experimental.pallas import tpu as pltpu

F32 = jnp.float32
BF16 = jnp.bfloat16

GRID_W = 64
CONV_WIDTH = 3
D_CONV = 512
N_DIFF_HEADS = 4
DIFF_QK_DIM = 64
DIFF_V_DIM = 128
D_ATTN = N_DIFF_HEADS * DIFF_V_DIM
FOURIER_GROUP = 128
ROPE_BASE = 10000.0
LN_EPS = 1e-6
SUBLN_EPS = 1e-5
COL_CONV = 3 * D_CONV
COL_QK = N_DIFF_HEADS * 2 * DIFF_QK_DIM
Q0 = COL_CONV
K0 = Q0 + COL_QK
V0 = K0 + COL_QK
Q_SCALE = math.log2(math.e) * DIFF_QK_DIM ** -0.5

LANES = 128
SUBLANES = 8
VMEM_LIMIT = 56 * 1024 * 1024


def _params(n_axes):
    return pltpu.CompilerParams(
        dimension_semantics=("arbitrary",) * n_axes, vmem_limit_bytes=VMEM_LIMIT)


def _const_spec(shape):
    nd = len(shape)
    return pl.BlockSpec(shape, lambda *_: (0,) * nd, pipeline_mode=pl.Buffered(1))


class _Stacked:
    def __init__(self, array, index):
        self.array, self.index = array, index

    def spec(self):
        k, n = self.array.shape[1:]
        index = self.index
        return pl.BlockSpec((None, k, n), lambda *_: (index, 0, 0), pipeline_mode=pl.Buffered(1))


def _layer_norm(x):
    mu = jnp.mean(x, axis=-1, keepdims=True)
    xc = x - mu
    var = jnp.mean(xc * xc, axis=-1, keepdims=True)
    return xc * lax.rsqrt(var + LN_EPS)


def _modulated(x, shift_ref, scale_ref):
    return (_layer_norm(x) * (1.0 + scale_ref[0]) + shift_ref[0]).astype(BF16)


def _post_norm(h, y, gate_ref, g_ref, b_ref, alpha):
    return _layer_norm(alpha * h + gate_ref[0] * y) * g_ref[...] + b_ref[...]


def _mod_kernel(cond_ref, w_ref, b_ref, o_ref):
    c = cond_ref[...]
    a = (c * (1.0 / (1.0 + jnp.exp(-c)))).astype(BF16)
    o_ref[0] = jnp.dot(a, w_ref[0].astype(BF16), preferred_element_type=F32) + b_ref[0]


def _modulation(cond8, ada_w, ada_b):
    depth, d, n6 = ada_w.shape
    tn = 1536
    return pl.pallas_call(
        _mod_kernel,
        grid=(depth, n6 // tn),
        in_specs=[
            pl.BlockSpec((SUBLANES, d), lambda l, j: (0, 0)),
            pl.BlockSpec((1, d, tn), lambda l, j: (l, 0, j)),
            pl.BlockSpec((1, 1, tn), lambda l, j: (l, 0, j)),
        ],
        out_specs=pl.BlockSpec((1, SUBLANES, tn), lambda l, j: (l, 0, j)),
        out_shape=jax.ShapeDtypeStruct((depth, SUBLANES, n6), F32),
        compiler_params=_params(2),
        name="modulation",
    )(cond8, ada_w, ada_b.reshape(depth, 1, n6))


class _Mod:
    def __init__(self, table, d):
        depth = table.shape[0]
        self.rows = table.reshape(depth, SUBLANES, 6, d).transpose(0, 2, 1, 3).reshape(
            depth * 6 * SUBLANES, 1, d)
        self.d = d

    def spec(self, layer, which, row_of_step):
        base = (layer * 6 + which) * SUBLANES
        return pl.BlockSpec((1, 1, self.d), lambda *idx: (base + row_of_step(*idx), 0, 0))


def _rope_store(dst_ref, rows, p, cos_ref, sin_lo_ref, sin_hi_ref, scale):
    cos, sin_lo, sin_hi = cos_ref[rows, :], sin_lo_ref[rows, :], sin_hi_ref[rows, :]
    half = DIFF_QK_DIM // 4
    for c in range(p.shape[1] // LANES):
        x = p[:, c * LANES:(c + 1) * LANES]
        ahead = pltpu.roll(x, LANES - half, 1)
        behind = pltpu.roll(x, half, 1)
        r = x * cos + ahead * sin_lo + behind * sin_hi
        if scale != 1.0:
            r = r * scale
        dst_ref[rows, c * LANES:(c + 1) * LANES] = r.astype(dst_ref.dtype)


def _inproj_kernel(x_ref, sh_ref, sc_ref, w_ref, cos_ref, sin_lo_ref, sin_hi_ref,
                   gb_ref, g_ref, q_ref, k_ref, v_ref, *, sub):
    rope = (cos_ref, sin_lo_ref, sin_hi_ref)
    for lo in range(0, x_ref.shape[0], sub):
        rows = pl.ds(lo, sub)
        u = _modulated(x_ref[rows, :], sh_ref, sc_ref)
        pc = jnp.dot(u, w_ref[:, 0:COL_CONV], preferred_element_type=F32)
        gb_ref[rows, :] = pc[:, 0:D_CONV]
        g_ref[rows, :] = pc[:, D_CONV:2 * D_CONV] * pc[:, 2 * D_CONV:3 * D_CONV]
        pq = jnp.dot(u, w_ref[:, Q0:K0], preferred_element_type=F32)
        _rope_store(q_ref, rows, pq, *rope, Q_SCALE)
        pk = jnp.dot(u, w_ref[:, K0:V0], preferred_element_type=F32)
        _rope_store(k_ref.at[0], rows, pk, *rope, 1.0)
        v_ref[0, rows, :] = jnp.dot(u, w_ref[:, V0:V0 + D_ATTN], preferred_element_type=F32).astype(BF16)


def _inproj(x, mod, layer, row_of_step, w_bf, rope, tm, sub, seq):
    t, d = x.shape
    steps_per_seq = seq // tm
    tok = lambda i: (i, 0)
    kvi = lambda i: (i // steps_per_seq, i % steps_per_seq, 0)
    tab = pl.BlockSpec((tm, LANES), lambda i: (i % steps_per_seq, 0))
    out = lambda n: pl.BlockSpec((tm, n), tok)
    return pl.pallas_call(
        functools.partial(_inproj_kernel, sub=sub),
        grid=(t // tm,),
        in_specs=[pl.BlockSpec((tm, d), tok), mod.spec(layer, 0, row_of_step),
                  mod.spec(layer, 1, row_of_step), w_bf.spec(), tab, tab, tab],
        out_specs=[out(D_CONV), out(D_CONV), out(COL_QK),
                   pl.BlockSpec((1, tm, COL_QK), kvi), pl.BlockSpec((1, tm, D_ATTN), kvi)],
        out_shape=[jax.ShapeDtypeStruct((t, D_CONV), F32), jax.ShapeDtypeStruct((t, D_CONV), F32),
                   jax.ShapeDtypeStruct((t, COL_QK), BF16),
                   jax.ShapeDtypeStruct((t // seq, seq, COL_QK), BF16),
                   jax.ShapeDtypeStruct((t // seq, seq, D_ATTN), BF16)],
        compiler_params=_params(1),
        name="inproj",
    )(x, mod.rows, mod.rows, w_bf.array, *rope)


def _attn_kernel(lq_ref, q_ref, *refs, lam_init, n_src, sub):
    k_refs, v_refs = refs[:n_src], refs[n_src:2 * n_src]
    sg_ref, o_ref = refs[2 * n_src], refs[2 * n_src + 1]
    lq = lq_ref[...]
    lam = (jnp.exp(jnp.sum(lq[0:1] * lq[1:2], axis=1, keepdims=True))
           - jnp.exp(jnp.sum(lq[2:3] * lq[3:4], axis=1, keepdims=True)) + lam_init)
    nt = (((1,), (1,)), ((), ()))

    def scores(qc):
        return [lax.dot_general(qc, k[0], nt, preferred_element_type=F32) for k in k_refs]

    def softmax_parts(s):
        m = functools.reduce(jnp.maximum, [jnp.max(x, axis=1, keepdims=True) for x in s])
        e = [jnp.exp2(x - m) for x in s]
        return e, sum(jnp.sum(x, axis=1, keepdims=True) for x in e)

    def score_tile(r):
        q = q_ref[r * sub:(r + 1) * sub, :]
        lane = lax.broadcasted_iota(jnp.int32, q.shape, 1)
        zero = jnp.zeros_like(q)
        return (scores(jnp.where(lane < DIFF_QK_DIM, q, zero)),
                scores(jnp.where(lane >= DIFF_QK_DIM, q, zero)))

    def project(r, e1, l1, e2, l2):
        ratio = lam * l1 / l2
        o = sum(jnp.dot((a - ratio * b).astype(BF16), v[0], preferred_element_type=F32)
                for a, b, v in zip(e1, e2, v_refs)) / l1
        y = o * lax.rsqrt(jnp.mean(o * o, axis=1, keepdims=True) + SUBLN_EPS)
        o_ref[r * sub:(r + 1) * sub, :] = (y * sg_ref[...] * (1.0 - lam_init)).astype(o_ref.dtype)

    n_sub = q_ref.shape[0] // sub
    s_next = score_tile(0)
    parts = None
    for r in range(n_sub):
        s1, s2 = s_next
        cur = softmax_parts(s1) + softmax_parts(s2)
        if r + 1 < n_sub:
            s_next = score_tile(r + 1)
        if parts is not None:
            project(r - 1, *parts)
        parts = cur
    project(n_sub - 1, *parts)


def _attention(lq, q, ks, vs, sg, lam_init, tq, sub):
    t = q.shape[0]
    b = ks[0].shape[0]
    nq = t // b // tq
    qo = pl.BlockSpec((tq, DIFF_V_DIM), lambda bi, h, i: (bi * nq + i, h))
    src = lambda a: pl.BlockSpec((1, a.shape[1], DIFF_V_DIM), lambda bi, h, i: (bi, 0, h))
    return pl.pallas_call(
        functools.partial(_attn_kernel, lam_init=lam_init, n_src=len(ks), sub=sub),
        grid=(b, N_DIFF_HEADS, nq),
        in_specs=[pl.BlockSpec(lq.shape, lambda bi, h, i: (0, 0)), qo] + [src(a) for a in ks + vs]
                 + [pl.BlockSpec((1, DIFF_V_DIM), lambda bi, h, i: (0, 0))],
        out_specs=qo,
        out_shape=jax.ShapeDtypeStruct((t, D_ATTN), BF16),
        compiler_params=_params(3),
        name="diff_attention",
    )(lq, q, *ks, *vs, sg.reshape(1, DIFF_V_DIM))


MLP_CHUNK = 512


def _mlp_residual(h, mlp_refs, alpha):
    sh_ref, sc_ref, w1_ref, w2_ref, gate_ref, lg_ref, lb_ref = mlp_refs
    u = _modulated(h, sh_ref, sc_ref)
    y = jnp.zeros(h.shape, F32)
    for c in range(w1_ref.shape[1] // MLP_CHUNK):
        hid = jnp.dot(u, w1_ref[:, c * MLP_CHUNK:(c + 1) * MLP_CHUNK], preferred_element_type=F32)
        hid = jnp.square(jnp.maximum(hid, 0.0)).astype(BF16)
        y = y + jnp.dot(hid, w2_ref[c * MLP_CHUNK:(c + 1) * MLP_CHUNK, :], preferred_element_type=F32)
    return _post_norm(h, y, gate_ref, lg_ref, lb_ref, alpha)


class _MlpTail:
    def __init__(self, w1, w2, mod, layer, row_of_step, ln_g, ln_b):
        d = mod.d
        vec = pl.BlockSpec((1, d), lambda *_: (0, 0))
        self.specs = [mod.spec(layer, 3, row_of_step), mod.spec(layer, 4, row_of_step), w1.spec(),
                      w2.spec(), mod.spec(layer, 5, row_of_step), vec, vec]
        self.args = [mod.rows, mod.rows, w1.array, w2.array, mod.rows, ln_g.reshape(1, d),
                     ln_b.reshape(1, d)]


def _mix_out_kernel(h_ref, gb_ref, g_ref, gprev_ref, gnext_ref, a_ref, w_ref, cw_ref,
                    gate_ref, lg_ref, lb_ref, *rest, steps_per_seq, sub, alpha):
    mlp_refs, o_ref = rest[:-1], rest[-1]
    i = pl.program_id(0)
    g = g_ref[...]
    tm = g.shape[0]
    row = lax.broadcasted_iota(jnp.int32, g.shape, 0)
    pos = i % steps_per_seq
    keep_prev = jnp.where(pos == 0, 0.0, 1.0)
    keep_next = jnp.where(pos == steps_per_seq - 1, 0.0, 1.0)
    before = jnp.where(row == 0, gprev_ref[SUBLANES - 1:SUBLANES, :] * keep_prev, pltpu.roll(g, 1, 0))
    after = jnp.where(row == tm - 1, gnext_ref[0:1, :] * keep_next, pltpu.roll(g, tm - 1, 0))
    cw = cw_ref[...]
    conv = before * cw[0:1] + g * cw[1:2] + after * cw[2:3]
    yc = (gb_ref[...] * conv).astype(BF16)
    subs = range(0, tm, sub)
    ys = [jnp.dot(yc[lo:lo + sub], w_ref[0:D_CONV, :], preferred_element_type=F32)
          + jnp.dot(a_ref[lo:lo + sub, :], w_ref[D_CONV:D_CONV + D_ATTN, :], preferred_element_type=F32)
          for lo in subs]
    h1s = [_post_norm(h_ref[lo:lo + sub, :], y, gate_ref, lg_ref, lb_ref, alpha) for lo, y in zip(subs, ys)]
    for lo, h1 in zip(subs, h1s):
        o_ref[lo:lo + sub, :] = _mlp_residual(h1, mlp_refs, alpha)


def _mix_out(h, gb, g, attn, w_bf, conv_w, mod, layer, row_of_step, ln_g, ln_b, mlp, tm, sub, seq, alpha):
    t, d = h.shape
    steps_per_seq = seq // tm
    blocks8 = tm // SUBLANES
    last8 = t // SUBLANES - 1
    tok = lambda i: (i, 0)
    vec = pl.BlockSpec((1, d), lambda i: (0, 0))
    return pl.pallas_call(
        functools.partial(_mix_out_kernel, steps_per_seq=steps_per_seq, sub=sub, alpha=alpha),
        grid=(t // tm,),
        in_specs=[pl.BlockSpec((tm, d), tok), pl.BlockSpec((tm, D_CONV), tok),
                  pl.BlockSpec((tm, D_CONV), tok),
                  pl.BlockSpec((SUBLANES, D_CONV), lambda i: (jnp.maximum(i * blocks8 - 1, 0), 0)),
                  pl.BlockSpec((SUBLANES, D_CONV), lambda i: (jnp.minimum((i + 1) * blocks8, last8), 0)),
                  pl.BlockSpec((tm, D_ATTN), tok), w_bf.spec(),
                  pl.BlockSpec(conv_w.shape, lambda i: (0, 0)),
                  mod.spec(layer, 2, row_of_step), vec, vec] + mlp.specs,
        out_specs=pl.BlockSpec((tm, d), tok),
        out_shape=jax.ShapeDtypeStruct((t, d), F32),
        compiler_params=_params(1),
        name="mix_out_mlp",
    )(h, gb, g, g, g, attn, w_bf.array, conv_w, mod.rows, ln_g.reshape(1, d), ln_b.reshape(1, d),
      *mlp.args)


def _mlp_kernel(h_ref, *rest, alpha):
    rest[-1][...] = _mlp_residual(h_ref[...], rest[:-1], alpha)


def _mlp(h, mlp, tm, alpha):
    t, d = h.shape
    tok = lambda i: (i, 0)
    return pl.pallas_call(
        functools.partial(_mlp_kernel, alpha=alpha),
        grid=(t // tm,),
        in_specs=[pl.BlockSpec((tm, d), tok)] + mlp.specs,
        out_specs=pl.BlockSpec((tm, d), tok),
        out_shape=jax.ShapeDtypeStruct((t, d), F32),
        compiler_params=_params(1),
        name="mlp",
    )(h, *mlp.args)


def _fourier_in_kernel(x_ref, sh_ref, sc_ref, gcs_ref, z_ref):
    u = _modulated(x_ref[...], sh_ref, sc_ref)
    gcs = gcs_ref[...]
    for g in range(u.shape[1] // FOURIER_GROUP):
        lo, hi = g * FOURIER_GROUP, (g + 1) * FOURIER_GROUP
        zz = jnp.dot(u[:, lo:hi], gcs, preferred_element_type=F32)
        z_ref[0, 0, :, lo:hi] = zz[:, 0:FOURIER_GROUP].astype(BF16)
        z_ref[0, 1, :, lo:hi] = zz[:, FOURIER_GROUP:2 * FOURIER_GROUP].astype(BF16)


def _fourier_in(x, mod, layer, row_of_step, gcs, tm, seq):
    t, d = x.shape
    steps_per_seq = seq // tm
    return pl.pallas_call(
        _fourier_in_kernel,
        grid=(t // tm,),
        in_specs=[pl.BlockSpec((tm, d), lambda i: (i, 0)), mod.spec(layer, 0, row_of_step),
                  mod.spec(layer, 1, row_of_step), _const_spec(gcs.shape)],
        out_specs=pl.BlockSpec((1, 2, tm, d), lambda i: (i // steps_per_seq, 0, i % steps_per_seq, 0)),
        out_shape=jax.ShapeDtypeStruct((t // seq, 2, seq, d), BF16),
        compiler_params=_params(1),
        name="fourier_in",
    )(x, mod.rows, mod.rows, gcs)


FOURIER_SLAB = 16
FOURIER_OUT_SLAB = 8


def _fourier_rows_kernel(x_ref, sh_ref, sc_ref, k1_ref, gm_ref, z_ref):
    w, nb, d = x_ref.shape[1], x_ref.shape[2], x_ref.shape[3]
    rows = w * nb
    u = _modulated(x_ref[0].reshape(rows, d), sh_ref, sc_ref)
    y = jnp.dot(k1_ref[0], u, preferred_element_type=F32).astype(BF16)
    gm = gm_ref[...]
    for g in range(d // FOURIER_GROUP):
        lo, hi = g * FOURIER_GROUP, (g + 1) * FOURIER_GROUP
        yy = jnp.concatenate([y[0:rows, lo:hi], y[rows:2 * rows, lo:hi]], axis=1)
        zz = jnp.dot(yy, gm, preferred_element_type=F32).astype(BF16)
        z_ref[0, 0, :, :, lo:hi] = zz[:, 0:FOURIER_GROUP].reshape(w, nb, FOURIER_GROUP)
        z_ref[0, 1, :, :, lo:hi] = zz[:, FOURIER_GROUP:2 * FOURIER_GROUP].reshape(w, nb, FOURIER_GROUP)


def _fourier_rows(h, mod, layer, k1, gm, batch):
    t, d = h.shape
    w, nb = GRID_W, FOURIER_SLAB
    h4 = h.reshape(batch, w, w, d)
    return pl.pallas_call(
        _fourier_rows_kernel,
        grid=(w // nb, batch),
        in_specs=[pl.BlockSpec((1, w, nb, d), lambda j, bi: (bi, 0, j, 0)),
                  mod.spec(layer, 0, lambda j, bi: bi), mod.spec(layer, 1, lambda j, bi: bi),
                  pl.BlockSpec((1,) + k1.shape[1:], lambda j, bi: (j, 0, 0)), _const_spec(gm.shape)],
        out_specs=pl.BlockSpec((1, 2, w, nb, d), lambda j, bi: (bi, 0, 0, j, 0)),
        out_shape=jax.ShapeDtypeStruct((batch, 2, w, w, d), BF16),
        compiler_params=_params(2),
        name="fourier_rows",
    )(h4, mod.rows, mod.rows, k1, gm)


def _fourier_cols_kernel(z_ref, h_ref, k2_ref, w_ref, gate_ref, lg_ref, lb_ref, *rest, alpha):
    mlp_refs, o_ref = rest[:-1], rest[-1]
    w, d = z_ref.shape[3], z_ref.shape[4]
    nd = FOURIER_OUT_SLAB
    slabs = range(0, z_ref.shape[2], nd)
    ys = []
    for s in slabs:
        zz = z_ref[0, :, s:s + nd].reshape(2 * nd * w, d)
        f = jnp.dot(k2_ref[...], zz, preferred_element_type=F32).astype(BF16)
        ys.append(jnp.dot(f, w_ref[...], preferred_element_type=F32))
    h1s = [_post_norm(h_ref[0, :, s:s + nd, :].reshape(w * nd, d), y, gate_ref, lg_ref, lb_ref, alpha)
           for s, y in zip(slabs, ys)]
    for s, h1 in zip(slabs, h1s):
        o_ref[0, :, s:s + nd, :] = _mlp_residual(h1, mlp_refs, alpha).reshape(w, nd, d)


def _fourier_cols(z, h, k2, w_bf, mod, layer, ln_g, ln_b, mlp, alpha):
    batch, _, w, _, d = z.shape
    nd = 2 * FOURIER_OUT_SLAB
    h4 = h.reshape(batch, w, w, d)
    vec = pl.BlockSpec((1, d), lambda bi, j: (0, 0))
    hblk = pl.BlockSpec((1, w, nd, d), lambda bi, j: (bi, 0, j, 0))
    out = pl.pallas_call(
        functools.partial(_fourier_cols_kernel, alpha=alpha),
        grid=(batch, w // nd),
        in_specs=[pl.BlockSpec((1, 2, nd, w, d), lambda bi, j: (bi, 0, j, 0, 0)), hblk,
                  _const_spec(k2.shape), w_bf.spec(),
                  mod.spec(layer, 2, lambda bi, j: bi), vec, vec] + mlp.specs,
        out_specs=hblk,
        out_shape=jax.ShapeDtypeStruct((batch, w, w, d), F32),
        compiler_params=_params(2),
        name="fourier_cols_mlp",
    )(z, h4, k2, w_bf.array, mod.rows, ln_g.reshape(1, d), ln_b.reshape(1, d), *mlp.args)
    return out.reshape(batch * w * w, d)


def _fourier_dense_kernel(z_ref, cs_ref, o_ref, *, norm):
    zc = jnp.concatenate([z_ref[0, 0], z_ref[0, 1]], axis=0)
    o_ref[0] = (jnp.dot(cs_ref[...], zc, preferred_element_type=F32) * norm).astype(BF16)


def _fourier_dense(z, cs, norm):
    b, _, seq, d = z.shape
    return pl.pallas_call(
        functools.partial(_fourier_dense_kernel, norm=norm),
        grid=(b,),
        in_specs=[pl.BlockSpec((1, 2, seq, d), lambda bi: (bi, 0, 0, 0)), _const_spec(cs.shape)],
        out_specs=pl.BlockSpec((1, seq, d), lambda bi: (bi, 0, 0)),
        out_shape=jax.ShapeDtypeStruct((b, seq, d), BF16),
        compiler_params=_params(1),
        name="fourier_dense",
    )(z, cs)


def _proj_out_kernel(h_ref, a_ref, w_ref, gate_ref, lg_ref, lb_ref, o_ref, *, alpha):
    y = jnp.dot(a_ref[...], w_ref[...], preferred_element_type=F32)
    o_ref[...] = _post_norm(h_ref[...], y, gate_ref, lg_ref, lb_ref, alpha)


def _proj_out(h, a, w_bf, mod, layer, row_of_step, ln_g, ln_b, tm, alpha):
    t, d = h.shape
    tok = lambda i: (i, 0)
    vec = pl.BlockSpec((1, d), lambda i: (0, 0))
    return pl.pallas_call(
        functools.partial(_proj_out_kernel, alpha=alpha),
        grid=(t // tm,),
        in_specs=[pl.BlockSpec((tm, d), tok), pl.BlockSpec((tm, a.shape[1]), tok),
                  w_bf.spec(), mod.spec(layer, 2, row_of_step), vec, vec],
        out_specs=pl.BlockSpec((tm, d), tok),
        out_shape=jax.ShapeDtypeStruct((t, d), F32),
        compiler_params=_params(1),
        name="proj_out",
    )(h, a, w_bf.array, mod.rows, ln_g.reshape(1, d), ln_b.reshape(1, d))


def _dft_cos_sin(n, denom):
    idx = np.arange(n, dtype=np.int64)
    ang = ((idx[:, None] * idx[None, :]) % denom) * (2.0 * math.pi / denom)
    return np.cos(ang), np.sin(ang)


def _bf16_const(a):
    return jnp.asarray(np.asarray(a, np.float32)).astype(BF16)


def _rope_tables(seq):
    half = DIFF_QK_DIM // 4
    t = np.arange(seq)
    inv = ROPE_BASE ** (-np.arange(half, dtype=np.float64) / half)
    lane = np.arange(LANES)
    use_col = (lane % DIFF_QK_DIM) >= DIFF_QK_DIM // 2
    second = (lane % (2 * half)) >= half
    pos = np.where(use_col[None, :], (t % GRID_W)[:, None], (t // GRID_W)[:, None])
    ang = pos * inv[lane % half][None, :]
    cos, sin = np.cos(ang), np.sin(ang)
    tabs = (cos, np.where(second[None, :], 0.0, -sin), np.where(second[None, :], sin, 0.0))
    return tuple(jnp.asarray(a, F32) for a in tabs)


def _fourier_tables(seq, norm):
    w, nb, nd = GRID_W, FOURIER_SLAB, FOURIER_OUT_SLAB
    idx = np.arange(w, dtype=np.int64)
    slab = np.arange(w // nb, dtype=np.int64)
    col = np.arange(nb, dtype=np.int64)
    t = idx[None, None, None, :] * w + slab[:, None, None, None] * nb + col[None, None, :, None]
    ang = ((idx[None, :, None, None] * t) % seq) * (2.0 * math.pi / seq)
    trig = np.stack([np.cos(ang), -np.sin(ang)], axis=1)
    k1 = trig[..., None] * np.eye(nb)[None, None, None, :, None, :]
    c64, s64 = _dft_cos_sin(w, w)
    cs = np.stack([c64, s64], axis=1) * norm
    k2 = cs[:, None, :, None, :] * np.eye(nd)[None, :, None, :, None]
    return (_bf16_const(k1.reshape(w // nb, 2 * w * nb, w * nb)),
            _bf16_const(k2.reshape(w * nd, 2 * nd * w)))


def _no_rope_tables(seq):
    return jnp.ones((seq, LANES), F32), jnp.zeros((seq, LANES), F32), jnp.zeros((seq, LANES), F32)


def kernel(x, c, ctx, c_ctx, ada_w, ada_b, ln_g, ln_b, mlp_w1, mlp_w2, w_in, conv_w, lambda_qk,
           subln_g, w_out_mix, w_out_fourier):
    batch, seq, d = x.shape
    ctx_len = ctx.shape[1]
    depth = ada_w.shape[0]
    alpha = (2.0 * depth) ** 0.25
    last_attn = 2 * ((depth - 1) // 2)
    tm = 512
    assert seq == GRID_W * GRID_W and seq % tm == 0 and batch < SUBLANES and ctx_len % SUBLANES == 0

    cond8 = jnp.zeros((SUBLANES, d), F32).at[:batch].set(c).at[batch].set(c_ctx)
    mod = _Mod(_modulation(cond8, ada_w, ada_b), d)
    lat_row = lambda i: i // (seq // tm)
    ctx_row = lambda i: batch

    w_in_bf = w_in.astype(BF16)
    w_mix_bf = w_out_mix.astype(BF16)
    w_four_bf = w_out_fourier.astype(BF16)
    w1_bf = mlp_w1.astype(BF16)
    w2_bf = mlp_w2.astype(BF16)

    rope = _rope_tables(seq)
    no_rope = _no_rope_tables(ctx_len)

    gc, gs = _dft_cos_sin(FOURIER_GROUP, FOURIER_GROUP)
    gcs = _bf16_const(np.concatenate([gc, -gs], axis=1))
    gmat = _bf16_const(np.block([[gc, -gs], [gs, gc]]))
    k1, k2 = _fourier_tables(seq, (seq * FOURIER_GROUP) ** -0.5)
    cc, sc_ = _dft_cos_sin(ctx_len, ctx_len)
    cs_ctx = _bf16_const(np.concatenate([cc, sc_], axis=1))

    h = x.reshape(batch * seq, d)
    hc = ctx.reshape(batch * ctx_len, d)

    for i in range(depth):
        j = i // 2
        update_ctx = i < last_attn
        w1, w2 = _Stacked(w1_bf, i), _Stacked(w2_bf, i)
        mlp_ctx = _MlpTail(w1, w2, mod, i, ctx_row, ln_g[i, 1], ln_b[i, 1])
        if i % 2 == 0:
            lam_init = 0.8 - 0.6 * math.exp(-0.3 * i)
            w_in_j, w_mix_j = _Stacked(w_in_bf, j), _Stacked(w_mix_bf, j)
            mix_row = lambda t: t // (seq // (2 * tm))
            gb, g, q, k, v = _inproj(h, mod, i, mix_row, w_in_j, rope, 2 * tm, tm, seq)
            gbc, gcx, qc, kc, vc = _inproj(hc, mod, i, ctx_row, w_in_j, no_rope, ctx_len, ctx_len, ctx_len)
            attn = _attention(lambda_qk[j], q, [k, kc], [v, vc], subln_g[j], lam_init, 512, 256)
            mlp_lat = _MlpTail(w1, w2, mod, i, mix_row, ln_g[i, 1], ln_b[i, 1])
            h = _mix_out(h, gb, g, attn, w_mix_j, conv_w[j], mod, i, mix_row,
                         ln_g[i, 0], ln_b[i, 0], mlp_lat, 2 * tm, tm, seq, alpha)
            if update_ctx:
                attn_c = _attention(lambda_qk[j], qc, [kc], [vc], subln_g[j], lam_init, ctx_len, ctx_len)
                hc = _mix_out(hc, gbc, gcx, attn_c, w_mix_j, conv_w[j], mod, i, ctx_row,
                              ln_g[i, 0], ln_b[i, 0], mlp_ctx, ctx_len, ctx_len, ctx_len, alpha)
        else:
            w_four_j = _Stacked(w_four_bf, j)
            z = _fourier_rows(h, mod, i, k1, gmat, batch)
            mlp_slab = _MlpTail(w1, w2, mod, i, lambda bi, sl: bi, ln_g[i, 1], ln_b[i, 1])
            h = _fourier_cols(z, h, k2, w_four_j, mod, i, ln_g[i, 0], ln_b[i, 0], mlp_slab, alpha)
            if update_ctx:
                zc = _fourier_in(hc, mod, i, ctx_row, gcs, ctx_len, ctx_len)
                fc = _fourier_dense(zc, cs_ctx, (ctx_len * FOURIER_GROUP) ** -0.5)
                hc1 = _proj_out(hc, fc.reshape(batch * ctx_len, d), w_four_j, mod, i, ctx_row,
                                ln_g[i, 0], ln_b[i, 0], ctx_len, alpha)
                hc = _mlp(hc1, mlp_ctx, ctx_len, alpha)
    return h.reshape(batch, seq, d)
```

```python
import functools
import math

import jax
import jax.numpy as jnp
import numpy as np
from jax import lax
from jax.experimental import pallas as pl
from jax.experimental.pallas import tpu as pltpu

F32 = jnp.float32
BF16 = jnp.bfloat16

GRID_W = 64
CONV_WIDTH = 3
D_CONV = 512
N_DIFF_HEADS = 4
DIFF_QK_DIM = 64
DIFF_V_DIM = 128
D_ATTN = N_DIFF_HEADS * DIFF_V_DIM
FOURIER_GROUP = 128
ROPE_BASE = 10000.0
LN_EPS = 1e-6
SUBLN_EPS = 1e-5
COL_CONV = 3 * D_CONV
COL_QK = N_DIFF_HEADS * 2 * DIFF_QK_DIM
Q0 = COL_CONV
K0 = Q0 + COL_QK
V0 = K0 + COL_QK
Q_SCALE = math.log2(math.e) * DIFF_QK_DIM ** -0.5

LANES = 128
SUBLANES = 8
VMEM_LIMIT = 56 * 1024 * 1024


def _params(n_axes):
    return pltpu.CompilerParams(
        dimension_semantics=("arbitrary",) * n_axes, vmem_limit_bytes=VMEM_LIMIT)


def _const_spec(shape):
    nd = len(shape)
    return pl.BlockSpec(shape, lambda *_: (0,) * nd, pipeline_mode=pl.Buffered(1))


class _Stacked:
    def __init__(self, array, index):
        self.array, self.index = array, index

    def spec(self):
        k, n = self.array.shape[1:]
        index = self.index
        return pl.BlockSpec((None, k, n), lambda *_: (index, 0, 0), pipeline_mode=pl.Buffered(1))


def _layer_norm(x):
    mu = jnp.mean(x, axis=-1, keepdims=True)
    xc = x - mu
    var = jnp.mean(xc * xc, axis=-1, keepdims=True)
    return xc * lax.rsqrt(var + LN_EPS)


def _modulated(x, shift_ref, scale_ref):
    return (_layer_norm(x) * (1.0 + scale_ref[0]) + shift_ref[0]).astype(BF16)


def _post_norm(h, y, gate_ref, g_ref, b_ref, alpha):
    return _layer_norm(alpha * h + gate_ref[0] * y) * g_ref[...] + b_ref[...]


def _mod_kernel(cond_ref, w_ref, b_ref, o_ref):
    c = cond_ref[...]
    a = (c * (1.0 / (1.0 + jnp.exp(-c)))).astype(BF16)
    o_ref[0] = jnp.dot(a, w_ref[0].astype(BF16), preferred_element_type=F32) + b_ref[0]


def _modulation(cond8, ada_w, ada_b):
    depth, d, n6 = ada_w.shape
    tn = 1536
    return pl.pallas_call(
        _mod_kernel,
        grid=(depth, n6 // tn),
        in_specs=[
            pl.BlockSpec((SUBLANES, d), lambda l, j: (0, 0)),
            pl.BlockSpec((1, d, tn), lambda l, j: (l, 0, j)),
            pl.BlockSpec((1, 1, tn), lambda l, j: (l, 0, j)),
        ],
        out_specs=pl.BlockSpec((1, SUBLANES, tn), lambda l, j: (l, 0, j)),
        out_shape=jax.ShapeDtypeStruct((depth, SUBLANES, n6), F32),
        compiler_params=_params(2),
        name="modulation",
    )(cond8, ada_w, ada_b.reshape(depth, 1, n6))


class _Mod:
    def __init__(self, table, d):
        depth = table.shape[0]
        self.rows = table.reshape(depth, SUBLANES, 6, d).transpose(0, 2, 1, 3).reshape(
            depth * 6 * SUBLANES, 1, d)
        self.d = d

    def spec(self, layer, which, row_of_step):
        base = (layer * 6 + which) * SUBLANES
        return pl.BlockSpec((1, 1, self.d), lambda *idx: (base + row_of_step(*idx), 0, 0))


def _rope_store(dst_ref, rows, p, cos_ref, sin_lo_ref, sin_hi_ref, scale):
    cos, sin_lo, sin_hi = cos_ref[rows, :], sin_lo_ref[rows, :], sin_hi_ref[rows, :]
    half = DIFF_QK_DIM // 4
    for c in range(p.shape[1] // LANES):
        x = p[:, c * LANES:(c + 1) * LANES]
        ahead = pltpu.roll(x, LANES - half, 1)
        behind = pltpu.roll(x, half, 1)
        r = x * cos + ahead * sin_lo + behind * sin_hi
        if scale != 1.0:
            r = r * scale
        dst_ref[rows, c * LANES:(c + 1) * LANES] = r.astype(dst_ref.dtype)


def _inproj_kernel(x_ref, sh_ref, sc_ref, w_ref, cos_ref, sin_lo_ref, sin_hi_ref,
                   gb_ref, g_ref, q_ref, k_ref, v_ref, *, sub):
    rope = (cos_ref, sin_lo_ref, sin_hi_ref)
    for lo in range(0, x_ref.shape[0], sub):
        rows = pl.ds(lo, sub)
        u = _modulated(x_ref[rows, :], sh_ref, sc_ref)
        pc = jnp.dot(u, w_ref[:, 0:COL_CONV], preferred_element_type=F32)
        gb_ref[rows, :] = pc[:, 0:D_CONV]
        g_ref[rows, :] = pc[:, D_CONV:2 * D_CONV] * pc[:, 2 * D_CONV:3 * D_CONV]
        pq = jnp.dot(u, w_ref[:, Q0:K0], preferred_element_type=F32)
        _rope_store(q_ref, rows, pq, *rope, Q_SCALE)
        pk = jnp.dot(u, w_ref[:, K0:V0], preferred_element_type=F32)
        _rope_store(k_ref.at[0], rows, pk, *rope, 1.0)
        v_ref[0, rows, :] = jnp.dot(u, w_ref[:, V0:V0 + D_ATTN], preferred_element_type=F32).astype(BF16)


def _inproj(x, mod, layer, row_of_step, w_bf, rope, tm, sub, seq):
    t, d = x.shape
    steps_per_seq = seq // tm
    tok = lambda i: (i, 0)
    kvi = lambda i: (i // steps_per_seq, i % steps_per_seq, 0)
    tab = pl.BlockSpec((tm, LANES), lambda i: (i % steps_per_seq, 0))
    out = lambda n: pl.BlockSpec((tm, n), tok)
    return pl.pallas_call(
        functools.partial(_inproj_kernel, sub=sub),
        grid=(t // tm,),
        in_specs=[pl.BlockSpec((tm, d), tok), mod.spec(layer, 0, row_of_step),
                  mod.spec(layer, 1, row_of_step), w_bf.spec(), tab, tab, tab],
        out_specs=[out(D_CONV), out(D_CONV), out(COL_QK),
                   pl.BlockSpec((1, tm, COL_QK), kvi), pl.BlockSpec((1, tm, D_ATTN), kvi)],
        out_shape=[jax.ShapeDtypeStruct((t, D_CONV), F32), jax.ShapeDtypeStruct((t, D_CONV), F32),
                   jax.ShapeDtypeStruct((t, COL_QK), BF16),
                   jax.ShapeDtypeStruct((t // seq, seq, COL_QK), BF16),
                   jax.ShapeDtypeStruct((t // seq, seq, D_ATTN), BF16)],
        compiler_params=_params(1),
        name="inproj",
    )(x, mod.rows, mod.rows, w_bf.array, *rope)


def _attn_kernel(lq_ref, q_ref, *refs, lam_init, n_src, sub):
    k_refs, v_refs = refs[:n_src], refs[n_src:2 * n_src]
    sg_ref, o_ref = refs[2 * n_src], refs[2 * n_src + 1]
    lq = lq_ref[...]
    lam = (jnp.exp(jnp.sum(lq[0:1] * lq[1:2], axis=1, keepdims=True))
           - jnp.exp(jnp.sum(lq[2:3] * lq[3:4], axis=1, keepdims=True)) + lam_init)
    nt = (((1,), (1,)), ((), ()))

    def scores(qc):
        return [lax.dot_general(qc, k[0], nt, preferred_element_type=F32) for k in k_refs]

    def softmax_parts(s):
        m = functools.reduce(jnp.maximum, [jnp.max(x, axis=1, keepdims=True) for x in s])
        e = [jnp.exp2(x - m) for x in s]
        return e, sum(jnp.sum(x, axis=1, keepdims=True) for x in e)

    def score_tile(r):
        q = q_ref[r * sub:(r + 1) * sub, :]
        lane = lax.broadcasted_iota(jnp.int32, q.shape, 1)
        zero = jnp.zeros_like(q)
        return (scores(jnp.where(lane < DIFF_QK_DIM, q, zero)),
                scores(jnp.where(lane >= DIFF_QK_DIM, q, zero)))

    def project(r, e1, l1, e2, l2):
        ratio = lam * l1 / l2
        o = sum(jnp.dot((a - ratio * b).astype(BF16), v[0], preferred_element_type=F32)
                for a, b, v in zip(e1, e2, v_refs)) / l1
        y = o * lax.rsqrt(jnp.mean(o * o, axis=1, keepdims=True) + SUBLN_EPS)
        o_ref[r * sub:(r + 1) * sub, :] = (y * sg_ref[...] * (1.0 - lam_init)).astype(o_ref.dtype)

    n_sub = q_ref.shape[0] // sub
    s_next = score_tile(0)
    parts = None
    for r in range(n_sub):
        s1, s2 = s_next
        cur = softmax_parts(s1) + softmax_parts(s2)
        if r + 1 < n_sub:
            s_next = score_tile(r + 1)
        if parts is not None:
            project(r - 1, *parts)
        parts = cur
    project(n_sub - 1, *parts)


def _attention(lq, q, ks, vs, sg, lam_init, tq, sub):
    t = q.shape[0]
    b = ks[0].shape[0]
    nq = t // b // tq
    qo = pl.BlockSpec((tq, DIFF_V_DIM), lambda bi, h, i: (bi * nq + i, h))
    src = lambda a: pl.BlockSpec((1, a.shape[1], DIFF_V_DIM), lambda bi, h, i: (bi, 0, h))
    return pl.pallas_call(
        functools.partial(_attn_kernel, lam_init=lam_init, n_src=len(ks), sub=sub),
        grid=(b, N_DIFF_HEADS, nq),
        in_specs=[pl.BlockSpec(lq.shape, lambda bi, h, i: (0, 0)), qo] + [src(a) for a in ks + vs]
                 + [pl.BlockSpec((1, DIFF_V_DIM), lambda bi, h, i: (0, 0))],
        out_specs=qo,
        out_shape=jax.ShapeDtypeStruct((t, D_ATTN), BF16),
        compiler_params=_params(3),
        name="diff_attention",
    )(lq, q, *ks, *vs, sg.reshape(1, DIFF_V_DIM))


STREAM_CHUNK = 512
SUM_FLOOR = 2.0 ** -80


def _attn_stream_kernel(lq_ref, q_ref, *refs, lam_init, n_src, sub):
    k_refs, v_refs = refs[:n_src], refs[n_src:2 * n_src]
    sg_ref, o_ref = refs[2 * n_src], refs[2 * n_src + 1]
    vaug_refs = refs[2 * n_src + 2:2 * n_src + 2 + n_src]
    kmax_ref = refs[-1]
    lq = lq_ref[...]
    lam = (jnp.exp(jnp.sum(lq[0:1] * lq[1:2], axis=1, keepdims=True))
           - jnp.exp(jnp.sum(lq[2:3] * lq[3:4], axis=1, keepdims=True)) + lam_init)
    nt = (((1,), (1,)), ((), ()))
    half = lax.broadcasted_iota(jnp.int32, (1, 2 * DIFF_QK_DIM), 1) < DIFF_QK_DIM

    @pl.when(pl.program_id(2) == 0)
    def _():
        best = jnp.zeros((1, 2 * DIFF_QK_DIM), F32)
        for k, v, va in zip(k_refs, v_refs, vaug_refs):
            va[:, 0:DIFF_V_DIM] = v[0]
            va[:, DIFF_V_DIM:2 * DIFF_V_DIM] = jnp.ones((va.shape[0], DIFF_V_DIM), BF16)
            k2 = jnp.square(k[0].astype(F32))
            n1 = jnp.sum(jnp.where(half, k2, 0.0), axis=1, keepdims=True)
            n2 = jnp.sum(jnp.where(half, 0.0, k2), axis=1, keepdims=True)
            both = jnp.where(half, jnp.max(n1, axis=0, keepdims=True), jnp.max(n2, axis=0, keepdims=True))
            best = jnp.maximum(best, both)
        kmax_ref[...] = jnp.broadcast_to(jnp.sqrt(best), kmax_ref.shape)

    chunks = []
    for j, k in enumerate(k_refs):
        size = min(k.shape[1], STREAM_CHUNK)
        chunks += [(j, c0, size) for c0 in range(0, k.shape[1], size)]

    def component(q, first):
        mask = half if first else jnp.logical_not(half)
        qc = jnp.where(mask, q, jnp.zeros_like(q))
        qf = qc.astype(F32)
        kmax = kmax_ref[0:1, 0:1] if first else kmax_ref[0:1, DIFF_QK_DIM:DIFF_QK_DIM + 1]
        bound = jnp.sqrt(jnp.sum(qf * qf, axis=1, keepdims=True)) * kmax
        score = lambda ch: lax.dot_general(qc, k_refs[ch[0]][0, ch[1]:ch[1] + ch[2], :], nt,
                                           preferred_element_type=F32)
        acc = None
        s_next = score(chunks[0])
        for idx, (j, c0, size) in enumerate(chunks):
            s = s_next
            if idx + 1 < len(chunks):
                s_next = score(chunks[idx + 1])
            e = jnp.exp2(s - bound).astype(BF16)
            part = jnp.dot(e, vaug_refs[j][c0:c0 + size, :], preferred_element_type=F32)
            acc = part if acc is None else acc + part
        return acc[:, 0:DIFF_V_DIM], acc[:, DIFF_V_DIM:2 * DIFF_V_DIM]

    def exact_component(q, first):
        mask = half if first else jnp.logical_not(half)
        qc = jnp.where(mask, q, jnp.zeros_like(q))
        s = [lax.dot_general(qc, k[0], nt, preferred_element_type=F32) for k in k_refs]
        m = functools.reduce(jnp.maximum, [jnp.max(x, axis=1, keepdims=True) for x in s])
        acc = sum(jnp.dot(jnp.exp2(x - m).astype(BF16), va[...], preferred_element_type=F32)
                  for x, va in zip(s, vaug_refs))
        return acc[:, 0:DIFF_V_DIM], acc[:, DIFF_V_DIM:2 * DIFF_V_DIM]

    def emit(r, parts):
        (o1, l1), (o2, l2) = parts
        o = o1 / l1 - lam * (o2 / l2)
        y = o * lax.rsqrt(jnp.mean(o * o, axis=1, keepdims=True) + SUBLN_EPS)
        o_ref[r * sub:(r + 1) * sub, :] = (y * sg_ref[...] * (1.0 - lam_init)).astype(o_ref.dtype)
        return jnp.minimum(jnp.min(l1), jnp.min(l2))

    n_sub = q_ref.shape[0] // sub
    tiles = [q_ref[r * sub:(r + 1) * sub, :] for r in range(n_sub)]
    smallest_sum = functools.reduce(
        jnp.minimum, [emit(r, (component(q, True), component(q, False))) for r, q in enumerate(tiles)])

    @pl.when(jnp.logical_not(smallest_sum >= SUM_FLOOR))
    def _():
        for r, q in enumerate(tiles):
            emit(r, (exact_component(q, True), exact_component(q, False)))


def _attention_stream(lq, q, ks, vs, sg, lam_init, tq, sub):
    t = q.shape[0]
    b = ks[0].shape[0]
    nq = t // b // tq
    qo = pl.BlockSpec((tq, DIFF_V_DIM), lambda bi, h, i: (bi * nq + i, h))
    src = lambda a: pl.BlockSpec((1, a.shape[1], DIFF_V_DIM), lambda bi, h, i: (bi, 0, h))
    return pl.pallas_call(
        functools.partial(_attn_stream_kernel, lam_init=lam_init, n_src=len(ks), sub=sub),
        grid=(b, N_DIFF_HEADS, nq),
        in_specs=[pl.BlockSpec(lq.shape, lambda bi, h, i: (0, 0)), qo] + [src(a) for a in ks + vs]
                 + [pl.BlockSpec((1, DIFF_V_DIM), lambda bi, h, i: (0, 0))],
        out_specs=qo,
        out_shape=jax.ShapeDtypeStruct((t, D_ATTN), BF16),
        scratch_shapes=[pltpu.VMEM((v.shape[1], 2 * DIFF_V_DIM), BF16) for v in vs]
                       + [pltpu.VMEM((SUBLANES, 2 * DIFF_QK_DIM), F32)],
        compiler_params=_params(3),
        name="diff_attention_stream",
    )(lq, q, *ks, *vs, sg.reshape(1, DIFF_V_DIM))


MLP_CHUNK = 512


def _mlp_residual(h, mlp_refs, alpha):
    sh_ref, sc_ref, w1_ref, w2_ref, gate_ref, lg_ref, lb_ref = mlp_refs
    u = _modulated(h, sh_ref, sc_ref)
    y = jnp.zeros(h.shape, F32)
    for c in range(w1_ref.shape[1] // MLP_CHUNK):
        hid = jnp.dot(u, w1_ref[:, c * MLP_CHUNK:(c + 1) * MLP_CHUNK], preferred_element_type=F32)
        hid = jnp.square(jnp.maximum(hid, 0.0)).astype(BF16)
        y = y + jnp.dot(hid, w2_ref[c * MLP_CHUNK:(c + 1) * MLP_CHUNK, :], preferred_element_type=F32)
    return _post_norm(h, y, gate_ref, lg_ref, lb_ref, alpha)


class _MlpTail:
    def __init__(self, w1, w2, mod, layer, row_of_step, ln_g, ln_b):
        d = mod.d
        vec = pl.BlockSpec((1, d), lambda *_: (0, 0))
        self.specs = [mod.spec(layer, 3, row_of_step), mod.spec(layer, 4, row_of_step), w1.spec(),
                      w2.spec(), mod.spec(layer, 5, row_of_step), vec, vec]
        self.args = [mod.rows, mod.rows, w1.array, w2.array, mod.rows, ln_g.reshape(1, d),
                     ln_b.reshape(1, d)]


def _mix_out_kernel(h_ref, gb_ref, g_ref, gprev_ref, gnext_ref, a_ref, w_ref, cw_ref,
                    gate_ref, lg_ref, lb_ref, *rest, steps_per_seq, sub, alpha):
    mlp_refs, o_ref = rest[:-1], rest[-1]
    i = pl.program_id(0)
    g = g_ref[...]
    tm = g.shape[0]
    row = lax.broadcasted_iota(jnp.int32, g.shape, 0)
    pos = i % steps_per_seq
    keep_prev = jnp.where(pos == 0, 0.0, 1.0)
    keep_next = jnp.where(pos == steps_per_seq - 1, 0.0, 1.0)
    before = jnp.where(row == 0, gprev_ref[SUBLANES - 1:SUBLANES, :] * keep_prev, pltpu.roll(g, 1, 0))
    after = jnp.where(row == tm - 1, gnext_ref[0:1, :] * keep_next, pltpu.roll(g, tm - 1, 0))
    cw = cw_ref[...]
    conv = before * cw[0:1] + g * cw[1:2] + after * cw[2:3]
    yc = (gb_ref[...] * conv).astype(BF16)
    subs = range(0, tm, sub)
    ys = [jnp.dot(yc[lo:lo + sub], w_ref[0:D_CONV, :], preferred_element_type=F32)
          + jnp.dot(a_ref[lo:lo + sub, :], w_ref[D_CONV:D_CONV + D_ATTN, :], preferred_element_type=F32)
          for lo in subs]
    h1s = [_post_norm(h_ref[lo:lo + sub, :], y, gate_ref, lg_ref, lb_ref, alpha) for lo, y in zip(subs, ys)]
    for lo, h1 in zip(subs, h1s):
        o_ref[lo:lo + sub, :] = _mlp_residual(h1, mlp_refs, alpha)


def _mix_out(h, gb, g, attn, w_bf, conv_w, mod, layer, row_of_step, ln_g, ln_b, mlp, tm, sub, seq, alpha):
    t, d = h.shape
    steps_per_seq = seq // tm
    blocks8 = tm // SUBLANES
    last8 = t // SUBLANES - 1
    tok = lambda i: (i, 0)
    vec = pl.BlockSpec((1, d), lambda i: (0, 0))
    return pl.pallas_call(
        functools.partial(_mix_out_kernel, steps_per_seq=steps_per_seq, sub=sub, alpha=alpha),
        grid=(t // tm,),
        in_specs=[pl.BlockSpec((tm, d), tok), pl.BlockSpec((tm, D_CONV), tok),
                  pl.BlockSpec((tm, D_CONV), tok),
                  pl.BlockSpec((SUBLANES, D_CONV), lambda i: (jnp.maximum(i * blocks8 - 1, 0), 0)),
                  pl.BlockSpec((SUBLANES, D_CONV), lambda i: (jnp.minimum((i + 1) * blocks8, last8), 0)),
                  pl.BlockSpec((tm, D_ATTN), tok), w_bf.spec(),
                  pl.BlockSpec(conv_w.shape, lambda i: (0, 0)),
                  mod.spec(layer, 2, row_of_step), vec, vec] + mlp.specs,
        out_specs=pl.BlockSpec((tm, d), tok),
        out_shape=jax.ShapeDtypeStruct((t, d), F32),
        compiler_params=_params(1),
        name="mix_out_mlp",
    )(h, gb, g, g, g, attn, w_bf.array, conv_w, mod.rows, ln_g.reshape(1, d), ln_b.reshape(1, d),
      *mlp.args)


def _mlp_kernel(h_ref, *rest, alpha):
    rest[-1][...] = _mlp_residual(h_ref[...], rest[:-1], alpha)


def _mlp(h, mlp, tm, alpha):
    t, d = h.shape
    tok = lambda i: (i, 0)
    return pl.pallas_call(
        functools.partial(_mlp_kernel, alpha=alpha),
        grid=(t // tm,),
        in_specs=[pl.BlockSpec((tm, d), tok)] + mlp.specs,
        out_specs=pl.BlockSpec((tm, d), tok),
        out_shape=jax.ShapeDtypeStruct((t, d), F32),
        compiler_params=_params(1),
        name="mlp",
    )(h, *mlp.args)


def _fourier_in_kernel(x_ref, sh_ref, sc_ref, gcs_ref, z_ref):
    u = _modulated(x_ref[...], sh_ref, sc_ref)
    gcs = gcs_ref[...]
    for g in range(u.shape[1] // FOURIER_GROUP):
        lo, hi = g * FOURIER_GROUP, (g + 1) * FOURIER_GROUP
        zz = jnp.dot(u[:, lo:hi], gcs, preferred_element_type=F32)
        z_ref[0, 0, :, lo:hi] = zz[:, 0:FOURIER_GROUP].astype(BF16)
        z_ref[0, 1, :, lo:hi] = zz[:, FOURIER_GROUP:2 * FOURIER_GROUP].astype(BF16)


def _fourier_in(x, mod, layer, row_of_step, gcs, tm, seq):
    t, d = x.shape
    steps_per_seq = seq // tm
    return pl.pallas_call(
        _fourier_in_kernel,
        grid=(t // tm,),
        in_specs=[pl.BlockSpec((tm, d), lambda i: (i, 0)), mod.spec(layer, 0, row_of_step),
                  mod.spec(layer, 1, row_of_step), _const_spec(gcs.shape)],
        out_specs=pl.BlockSpec((1, 2, tm, d), lambda i: (i // steps_per_seq, 0, i % steps_per_seq, 0)),
        out_shape=jax.ShapeDtypeStruct((t // seq, 2, seq, d), BF16),
        compiler_params=_params(1),
        name="fourier_in",
    )(x, mod.rows, mod.rows, gcs)


FOURIER_SLAB = 16
FOURIER_OUT_SLAB = 8


def _fourier_rows_kernel(x_ref, sh_ref, sc_ref, k1_ref, gm_ref, z_ref):
    w, nb, d = x_ref.shape[1], x_ref.shape[2], x_ref.shape[3]
    rows = w * nb
    u = _modulated(x_ref[0].reshape(rows, d), sh_ref, sc_ref)
    y = jnp.dot(k1_ref[0], u, preferred_element_type=F32).astype(BF16)
    gm = gm_ref[...]
    for g in range(d // FOURIER_GROUP):
        lo, hi = g * FOURIER_GROUP, (g + 1) * FOURIER_GROUP
        yy = jnp.concatenate([y[0:rows, lo:hi], y[rows:2 * rows, lo:hi]], axis=1)
        zz = jnp.dot(yy, gm, preferred_element_type=F32).astype(BF16)
        z_ref[0, 0, :, :, lo:hi] = zz[:, 0:FOURIER_GROUP].reshape(w, nb, FOURIER_GROUP)
        z_ref[0, 1, :, :, lo:hi] = zz[:, FOURIER_GROUP:2 * FOURIER_GROUP].reshape(w, nb, FOURIER_GROUP)


def _fourier_rows(h, mod, layer, k1, gm, batch):
    t, d = h.shape
    w, nb = GRID_W, FOURIER_SLAB
    h4 = h.reshape(batch, w, w, d)
    return pl.pallas_call(
        _fourier_rows_kernel,
        grid=(w // nb, batch),
        in_specs=[pl.BlockSpec((1, w, nb, d), lambda j, bi: (bi, 0, j, 0)),
                  mod.spec(layer, 0, lambda j, bi: bi), mod.spec(layer, 1, lambda j, bi: bi),
                  pl.BlockSpec((1,) + k1.shape[1:], lambda j, bi: (j, 0, 0)), _const_spec(gm.shape)],
        out_specs=pl.BlockSpec((1, 2, w, nb, d), lambda j, bi: (bi, 0, 0, j, 0)),
        out_shape=jax.ShapeDtypeStruct((batch, 2, w, w, d), BF16),
        compiler_params=_params(2),
        name="fourier_rows",
    )(h4, mod.rows, mod.rows, k1, gm)


def _fourier_cols_kernel(z_ref, h_ref, k2_ref, w_ref, gate_ref, lg_ref, lb_ref, *rest, alpha):
    mlp_refs, o_ref = rest[:-1], rest[-1]
    w, d = z_ref.shape[3], z_ref.shape[4]
    nd = FOURIER_OUT_SLAB
    slabs = range(0, z_ref.shape[2], nd)
    ys = []
    for s in slabs:
        zz = z_ref[0, :, s:s + nd].reshape(2 * nd * w, d)
        f = jnp.dot(k2_ref[...], zz, preferred_element_type=F32).astype(BF16)
        ys.append(jnp.dot(f, w_ref[...], preferred_element_type=F32))
    h1s = [_post_norm(h_ref[0, :, s:s + nd, :].reshape(w * nd, d), y, gate_ref, lg_ref, lb_ref, alpha)
           for s, y in zip(slabs, ys)]
    for s, h1 in zip(slabs, h1s):
        o_ref[0, :, s:s + nd, :] = _mlp_residual(h1, mlp_refs, alpha).reshape(w, nd, d)


def _fourier_cols(z, h, k2, w_bf, mod, layer, ln_g, ln_b, mlp, alpha):
    batch, _, w, _, d = z.shape
    nd = 2 * FOURIER_OUT_SLAB
    h4 = h.reshape(batch, w, w, d)
    vec = pl.BlockSpec((1, d), lambda bi, j: (0, 0))
    hblk = pl.BlockSpec((1, w, nd, d), lambda bi, j: (bi, 0, j, 0))
    out = pl.pallas_call(
        functools.partial(_fourier_cols_kernel, alpha=alpha),
        grid=(batch, w // nd),
        in_specs=[pl.BlockSpec((1, 2, nd, w, d), lambda bi, j: (bi, 0, j, 0, 0)), hblk,
                  _const_spec(k2.shape), w_bf.spec(),
                  mod.spec(layer, 2, lambda bi, j: bi), vec, vec] + mlp.specs,
        out_specs=hblk,
        out_shape=jax.ShapeDtypeStruct((batch, w, w, d), F32),
        compiler_params=_params(2),
        name="fourier_cols_mlp",
    )(z, h4, k2, w_bf.array, mod.rows, ln_g.reshape(1, d), ln_b.reshape(1, d), *mlp.args)
    return out.reshape(batch * w * w, d)


def _fourier_dense_kernel(z_ref, cs_ref, o_ref, *, norm):
    zc = jnp.concatenate([z_ref[0, 0], z_ref[0, 1]], axis=0)
    o_ref[0] = (jnp.dot(cs_ref[...], zc, preferred_element_type=F32) * norm).astype(BF16)


def _fourier_dense(z, cs, norm):
    b, _, seq, d = z.shape
    return pl.pallas_call(
        functools.partial(_fourier_dense_kernel, norm=norm),
        grid=(b,),
        in_specs=[pl.BlockSpec((1, 2, seq, d), lambda bi: (bi, 0, 0, 0)), _const_spec(cs.shape)],
        out_specs=pl.BlockSpec((1, seq, d), lambda bi: (bi, 0, 0)),
        out_shape=jax.ShapeDtypeStruct((b, seq, d), BF16),
        compiler_params=_params(1),
        name="fourier_dense",
    )(z, cs)


def _proj_out_kernel(h_ref, a_ref, w_ref, gate_ref, lg_ref, lb_ref, o_ref, *, alpha):
    y = jnp.dot(a_ref[...], w_ref[...], preferred_element_type=F32)
    o_ref[...] = _post_norm(h_ref[...], y, gate_ref, lg_ref, lb_ref, alpha)


def _proj_out(h, a, w_bf, mod, layer, row_of_step, ln_g, ln_b, tm, alpha):
    t, d = h.shape
    tok = lambda i: (i, 0)
    vec = pl.BlockSpec((1, d), lambda i: (0, 0))
    return pl.pallas_call(
        functools.partial(_proj_out_kernel, alpha=alpha),
        grid=(t // tm,),
        in_specs=[pl.BlockSpec((tm, d), tok), pl.BlockSpec((tm, a.shape[1]), tok),
                  w_bf.spec(), mod.spec(layer, 2, row_of_step), vec, vec],
        out_specs=pl.BlockSpec((tm, d), tok),
        out_shape=jax.ShapeDtypeStruct((t, d), F32),
        compiler_params=_params(1),
        name="proj_out",
    )(h, a, w_bf.array, mod.rows, ln_g.reshape(1, d), ln_b.reshape(1, d))


def _dft_cos_sin(n, denom):
    idx = np.arange(n, dtype=np.int64)
    ang = ((idx[:, None] * idx[None, :]) % denom) * (2.0 * math.pi / denom)
    return np.cos(ang), np.sin(ang)


def _bf16_const(a):
    return jnp.asarray(np.asarray(a, np.float32)).astype(BF16)


def _rope_tables(seq):
    half = DIFF_QK_DIM // 4
    t = np.arange(seq)
    inv = ROPE_BASE ** (-np.arange(half, dtype=np.float64) / half)
    lane = np.arange(LANES)
    use_col = (lane % DIFF_QK_DIM) >= DIFF_QK_DIM // 2
    second = (lane % (2 * half)) >= half
    pos = np.where(use_col[None, :], (t % GRID_W)[:, None], (t // GRID_W)[:, None])
    ang = pos * inv[lane % half][None, :]
    cos, sin = np.cos(ang), np.sin(ang)
    tabs = (cos, np.where(second[None, :], 0.0, -sin), np.where(second[None, :], sin, 0.0))
    return tuple(jnp.asarray(a, F32) for a in tabs)


def _fourier_tables(seq, norm):
    w, nb, nd = GRID_W, FOURIER_SLAB, FOURIER_OUT_SLAB
    idx = np.arange(w, dtype=np.int64)
    slab = np.arange(w // nb, dtype=np.int64)
    col = np.arange(nb, dtype=np.int64)
    t = idx[None, None, None, :] * w + slab[:, None, None, None] * nb + col[None, None, :, None]
    ang = ((idx[None, :, None, None] * t) % seq) * (2.0 * math.pi / seq)
    trig = np.stack([np.cos(ang), -np.sin(ang)], axis=1)
    k1 = trig[..., None] * np.eye(nb)[None, None, None, :, None, :]
    c64, s64 = _dft_cos_sin(w, w)
    cs = np.stack([c64, s64], axis=1) * norm
    k2 = cs[:, None, :, None, :] * np.eye(nd)[None, :, None, :, None]
    return (_bf16_const(k1.reshape(w // nb, 2 * w * nb, w * nb)),
            _bf16_const(k2.reshape(w * nd, 2 * nd * w)))


def _no_rope_tables(seq):
    return jnp.ones((seq, LANES), F32), jnp.zeros((seq, LANES), F32), jnp.zeros((seq, LANES), F32)


def kernel(x, c, ctx, c_ctx, ada_w, ada_b, ln_g, ln_b, mlp_w1, mlp_w2, w_in, conv_w, lambda_qk,
           subln_g, w_out_mix, w_out_fourier):
    batch, seq, d = x.shape
    ctx_len = ctx.shape[1]
    depth = ada_w.shape[0]
    alpha = (2.0 * depth) ** 0.25
    last_attn = 2 * ((depth - 1) // 2)
    tm = 512
    assert seq == GRID_W * GRID_W and seq % tm == 0 and batch < SUBLANES and ctx_len % SUBLANES == 0

    cond8 = jnp.zeros((SUBLANES, d), F32).at[:batch].set(c).at[batch].set(c_ctx)
    mod = _Mod(_modulation(cond8, ada_w, ada_b), d)
    lat_row = lambda i: i // (seq // tm)
    ctx_row = lambda i: batch

    w_in_bf = w_in.astype(BF16)
    w_mix_bf = w_out_mix.astype(BF16)
    w_four_bf = w_out_fourier.astype(BF16)
    w1_bf = mlp_w1.astype(BF16)
    w2_bf = mlp_w2.astype(BF16)

    rope = _rope_tables(seq)
    no_rope = _no_rope_tables(ctx_len)

    gc, gs = _dft_cos_sin(FOURIER_GROUP, FOURIER_GROUP)
    gcs = _bf16_const(np.concatenate([gc, -gs], axis=1))
    gmat = _bf16_const(np.block([[gc, -gs], [gs, gc]]))
    k1, k2 = _fourier_tables(seq, (seq * FOURIER_GROUP) ** -0.5)
    cc, sc_ = _dft_cos_sin(ctx_len, ctx_len)
    cs_ctx = _bf16_const(np.concatenate([cc, sc_], axis=1))

    h = x.reshape(batch * seq, d)
    hc = ctx.reshape(batch * ctx_len, d)

    for i in range(depth):
        j = i // 2
        update_ctx = i < last_attn
        w1, w2 = _Stacked(w1_bf, i), _Stacked(w2_bf, i)
        mlp_ctx = _MlpTail(w1, w2, mod, i, ctx_row, ln_g[i, 1], ln_b[i, 1])
        if i % 2 == 0:
            lam_init = 0.8 - 0.6 * math.exp(-0.3 * i)
            w_in_j, w_mix_j = _Stacked(w_in_bf, j), _Stacked(w_mix_bf, j)
            mix_row = lambda t: t // (seq // (2 * tm))
            gb, g, q, k, v = _inproj(h, mod, i, mix_row, w_in_j, rope, 2 * tm, tm, seq)
            gbc, gcx, qc, kc, vc = _inproj(hc, mod, i, ctx_row, w_in_j, no_rope, ctx_len, ctx_len, ctx_len)
            attn = _attention_stream(lambda_qk[j], q, [k, kc], [v, vc], subln_g[j], lam_init, 512, 256)
            mlp_lat = _MlpTail(w1, w2, mod, i, mix_row, ln_g[i, 1], ln_b[i, 1])
            h = _mix_out(h, gb, g, attn, w_mix_j, conv_w[j], mod, i, mix_row,
                         ln_g[i, 0], ln_b[i, 0], mlp_lat, 2 * tm, tm, seq, alpha)
            if update_ctx:
                attn_c = _attention(lambda_qk[j], qc, [kc], [vc], subln_g[j], lam_init, ctx_len, ctx_len)
                hc = _mix_out(hc, gbc, gcx, attn_c, w_mix_j, conv_w[j], mod, i, ctx_row,
                              ln_g[i, 0], ln_b[i, 0], mlp_ctx, ctx_len, ctx_len, ctx_len, alpha)
        else:
            w_four_j = _Stacked(w_four_bf, j)
            z = _fourier_rows(h, mod, i, k1, gmat, batch)
            mlp_slab = _MlpTail(w1, w2, mod, i, lambda bi, sl: bi, ln_g[i, 1], ln_b[i, 1])
            h = _fourier_cols(z, h, k2, w_four_j, mod, i, ln_g[i, 0], ln_b[i, 0], mlp_slab, alpha)
            if update_ctx:
                zc = _fourier_in(hc, mod, i, ctx_row, gcs, ctx_len, ctx_len)
                fc = _fourier_dense(zc, cs_ctx, (ctx_len * FOURIER_GROUP) ** -0.5)
                hc1 = _proj_out(hc, fc.reshape(batch * ctx_len, d), w_four_j, mod, i, ctx_row,
                                ln_g[i, 0], ln_b[i, 0], ctx_len, alpha)
                hc = _mlp(hc1, mlp_ctx, ctx_len, alpha)
    return h.reshape(batch, seq, d)
```

```python
import functools
import math

import jax
import jax.numpy as jnp
import numpy as np
from jax import lax
from jax.experimental import pallas as pl
from jax.experimental.pallas import tpu as pltpu

F32 = jnp.float32
BF16 = jnp.bfloat16

GRID_W = 64
CONV_WIDTH = 3
D_CONV = 512
N_DIFF_HEADS = 4
DIFF_QK_DIM = 64
DIFF_V_DIM = 128
D_ATTN = N_DIFF_HEADS * DIFF_V_DIM
FOURIER_GROUP = 128
ROPE_BASE = 10000.0
LN_EPS = 1e-6
SUBLN_EPS = 1e-5
COL_CONV = 3 * D_CONV
COL_QK = N_DIFF_HEADS * 2 * DIFF_QK_DIM
Q0 = COL_CONV
K0 = Q0 + COL_QK
V0 = K0 + COL_QK
Q_SCALE = math.log2(math.e) * DIFF_QK_DIM ** -0.5

LANES = 128
SUBLANES = 8
VMEM_LIMIT = 56 * 1024 * 1024


def _params(n_axes):
    return pltpu.CompilerParams(
        dimension_semantics=("arbitrary",) * n_axes, vmem_limit_bytes=VMEM_LIMIT)


def _const_spec(shape):
    nd = len(shape)
    return pl.BlockSpec(shape, lambda *_: (0,) * nd, pipeline_mode=pl.Buffered(1))


class _Stacked:
    def __init__(self, array, index):
        self.array, self.index = array, index

    def spec(self):
        k, n = self.array.shape[1:]
        index = self.index
        return pl.BlockSpec((None, k, n), lambda *_: (index, 0, 0), pipeline_mode=pl.Buffered(1))


def _layer_norm(x):
    mu = jnp.mean(x, axis=-1, keepdims=True)
    xc = x - mu
    var = jnp.mean(xc * xc, axis=-1, keepdims=True)
    return xc * lax.rsqrt(var + LN_EPS)


def _modulated(x, shift_ref, scale_ref):
    return (_layer_norm(x) * (1.0 + scale_ref[0]) + shift_ref[0]).astype(BF16)


def _post_norm(h, y, gate_ref, g_ref, b_ref, alpha):
    return _layer_norm(alpha * h + gate_ref[0] * y) * g_ref[...] + b_ref[...]


def _mod_kernel(cond_ref, w_ref, b_ref, o_ref):
    c = cond_ref[...]
    a = (c * (1.0 / (1.0 + jnp.exp(-c)))).astype(BF16)
    o_ref[0] = jnp.dot(a, w_ref[0].astype(BF16), preferred_element_type=F32) + b_ref[0]


def _modulation(cond8, ada_w, ada_b):
    depth, d, n6 = ada_w.shape
    tn = 1536
    return pl.pallas_call(
        _mod_kernel,
        grid=(depth, n6 // tn),
        in_specs=[
            pl.BlockSpec((SUBLANES, d), lambda l, j: (0, 0)),
            pl.BlockSpec((1, d, tn), lambda l, j: (l, 0, j)),
            pl.BlockSpec((1, 1, tn), lambda l, j: (l, 0, j)),
        ],
        out_specs=pl.BlockSpec((1, SUBLANES, tn), lambda l, j: (l, 0, j)),
        out_shape=jax.ShapeDtypeStruct((depth, SUBLANES, n6), F32),
        compiler_params=_params(2),
        name="modulation",
    )(cond8, ada_w, ada_b.reshape(depth, 1, n6))


class _Mod:
    def __init__(self, table, d):
        depth = table.shape[0]
        self.rows = table.reshape(depth, SUBLANES, 6, d).transpose(0, 2, 1, 3).reshape(
            depth * 6 * SUBLANES, 1, d)
        self.d = d

    def spec(self, layer, which, row_of_step):
        base = (layer * 6 + which) * SUBLANES
        return pl.BlockSpec((1, 1, self.d), lambda *idx: (base + row_of_step(*idx), 0, 0))


def _rope_store(dst_ref, rows, p, cos_ref, sin_lo_ref, sin_hi_ref, scale):
    cos, sin_lo, sin_hi = cos_ref[rows, :], sin_lo_ref[rows, :], sin_hi_ref[rows, :]
    half = DIFF_QK_DIM // 4
    for c in range(p.shape[1] // LANES):
        x = p[:, c * LANES:(c + 1) * LANES]
        ahead = pltpu.roll(x, LANES - half, 1)
        behind = pltpu.roll(x, half, 1)
        r = x * cos + ahead * sin_lo + behind * sin_hi
        if scale != 1.0:
            r = r * scale
        dst_ref[rows, c * LANES:(c + 1) * LANES] = r.astype(dst_ref.dtype)


def _inproj_kernel(x_ref, sh_ref, sc_ref, w_ref, cos_ref, sin_lo_ref, sin_hi_ref,
                   gb_ref, g_ref, q_ref, k_ref, v_ref, *, sub):
    rope = (cos_ref, sin_lo_ref, sin_hi_ref)
    for lo in range(0, x_ref.shape[0], sub):
        rows = pl.ds(lo, sub)
        u = _modulated(x_ref[rows, :], sh_ref, sc_ref)
        pc = jnp.dot(u, w_ref[:, 0:COL_CONV], preferred_element_type=F32)
        gb_ref[rows, :] = pc[:, 0:D_CONV]
        g_ref[rows, :] = pc[:, D_CONV:2 * D_CONV] * pc[:, 2 * D_CONV:3 * D_CONV]
        pq = jnp.dot(u, w_ref[:, Q0:K0], preferred_element_type=F32)
        _rope_store(q_ref, rows, pq, *rope, Q_SCALE)
        pk = jnp.dot(u, w_ref[:, K0:V0], preferred_element_type=F32)
        _rope_store(k_ref.at[0], rows, pk, *rope, 1.0)
        v_ref[0, rows, :] = jnp.dot(u, w_ref[:, V0:V0 + D_ATTN], preferred_element_type=F32).astype(BF16)


def _inproj(x, mod, layer, row_of_step, w_bf, rope, tm, sub, seq):
    t, d = x.shape
    steps_per_seq = seq // tm
    tok = lambda i: (i, 0)
    kvi = lambda i: (i // steps_per_seq, i % steps_per_seq, 0)
    tab = pl.BlockSpec((tm, LANES), lambda i: (i % steps_per_seq, 0))
    out = lambda n: pl.BlockSpec((tm, n), tok)
    return pl.pallas_call(
        functools.partial(_inproj_kernel, sub=sub),
        grid=(t // tm,),
        in_specs=[pl.BlockSpec((tm, d), tok), mod.spec(layer, 0, row_of_step),
                  mod.spec(layer, 1, row_of_step), w_bf.spec(), tab, tab, tab],
        out_specs=[out(D_CONV), out(D_CONV), out(COL_QK),
                   pl.BlockSpec((1, tm, COL_QK), kvi), pl.BlockSpec((1, tm, D_ATTN), kvi)],
        out_shape=[jax.ShapeDtypeStruct((t, D_CONV), F32), jax.ShapeDtypeStruct((t, D_CONV), F32),
                   jax.ShapeDtypeStruct((t, COL_QK), BF16),
                   jax.ShapeDtypeStruct((t // seq, seq, COL_QK), BF16),
                   jax.ShapeDtypeStruct((t // seq, seq, D_ATTN), BF16)],
        compiler_params=_params(1),
        name="inproj",
    )(x, mod.rows, mod.rows, w_bf.array, *rope)


def _attn_kernel(lq_ref, q_ref, *refs, lam_init, n_src, sub):
    k_refs, v_refs = refs[:n_src], refs[n_src:2 * n_src]
    sg_ref, o_ref = refs[2 * n_src], refs[2 * n_src + 1]
    lq = lq_ref[...]
    lam = (jnp.exp(jnp.sum(lq[0:1] * lq[1:2], axis=1, keepdims=True))
           - jnp.exp(jnp.sum(lq[2:3] * lq[3:4], axis=1, keepdims=True)) + lam_init)
    nt = (((1,), (1,)), ((), ()))

    def scores(qc):
        return [lax.dot_general(qc, k[0], nt, preferred_element_type=F32) for k in k_refs]

    def softmax_parts(s):
        m = functools.reduce(jnp.maximum, [jnp.max(x, axis=1, keepdims=True) for x in s])
        e = [jnp.exp2(x - m) for x in s]
        return e, sum(jnp.sum(x, axis=1, keepdims=True) for x in e)

    def score_tile(r):
        q = q_ref[r * sub:(r + 1) * sub, :]
        lane = lax.broadcasted_iota(jnp.int32, q.shape, 1)
        zero = jnp.zeros_like(q)
        return (scores(jnp.where(lane < DIFF_QK_DIM, q, zero)),
                scores(jnp.where(lane >= DIFF_QK_DIM, q, zero)))

    def project(r, e1, l1, e2, l2):
        ratio = lam * l1 / l2
        o = sum(jnp.dot((a - ratio * b).astype(BF16), v[0], preferred_element_type=F32)
                for a, b, v in zip(e1, e2, v_refs)) / l1
        y = o * lax.rsqrt(jnp.mean(o * o, axis=1, keepdims=True) + SUBLN_EPS)
        o_ref[r * sub:(r + 1) * sub, :] = (y * sg_ref[...] * (1.0 - lam_init)).astype(o_ref.dtype)

    n_sub = q_ref.shape[0] // sub
    s_next = score_tile(0)
    parts = None
    for r in range(n_sub):
        s1, s2 = s_next
        cur = softmax_parts(s1) + softmax_parts(s2)
        if r + 1 < n_sub:
            s_next = score_tile(r + 1)
        if parts is not None:
            project(r - 1, *parts)
        parts = cur
    project(n_sub - 1, *parts)


def _attention(lq, q, ks, vs, sg, lam_init, tq, sub):
    t = q.shape[0]
    b = ks[0].shape[0]
    nq = t // b // tq
    qo = pl.BlockSpec((tq, DIFF_V_DIM), lambda bi, h, i: (bi * nq + i, h))
    src = lambda a: pl.BlockSpec((1, a.shape[1], DIFF_V_DIM), lambda bi, h, i: (bi, 0, h))
    return pl.pallas_call(
        functools.partial(_attn_kernel, lam_init=lam_init, n_src=len(ks), sub=sub),
        grid=(b, N_DIFF_HEADS, nq),
        in_specs=[pl.BlockSpec(lq.shape, lambda bi, h, i: (0, 0)), qo] + [src(a) for a in ks + vs]
                 + [pl.BlockSpec((1, DIFF_V_DIM), lambda bi, h, i: (0, 0))],
        out_specs=qo,
        out_shape=jax.ShapeDtypeStruct((t, D_ATTN), BF16),
        compiler_params=_params(3),
        name="diff_attention",
    )(lq, q, *ks, *vs, sg.reshape(1, DIFF_V_DIM))


STREAM_CHUNK = 256
SUM_FLOOR = 2.0 ** -80


def _attn_stream_kernel(lq_ref, q_ref, *refs, lam_init, n_src, sub):
    k_refs, v_refs = refs[:n_src], refs[n_src:2 * n_src]
    sg_ref, o_ref = refs[2 * n_src], refs[2 * n_src + 1]
    vaug_refs = refs[2 * n_src + 2:2 * n_src + 2 + n_src]
    kmax_ref = refs[-1]
    lq = lq_ref[...]
    lam = (jnp.exp(jnp.sum(lq[0:1] * lq[1:2], axis=1, keepdims=True))
           - jnp.exp(jnp.sum(lq[2:3] * lq[3:4], axis=1, keepdims=True)) + lam_init)
    nt = (((1,), (1,)), ((), ()))
    half = lax.broadcasted_iota(jnp.int32, (1, 2 * DIFF_QK_DIM), 1) < DIFF_QK_DIM

    @pl.when(pl.program_id(2) == 0)
    def _():
        best = jnp.zeros((1, 2 * DIFF_QK_DIM), F32)
        for k, v, va in zip(k_refs, v_refs, vaug_refs):
            va[:, 0:DIFF_V_DIM] = v[0]
            va[:, DIFF_V_DIM:2 * DIFF_V_DIM] = jnp.ones((va.shape[0], DIFF_V_DIM), BF16)
            k2 = jnp.square(k[0].astype(F32))
            n1 = jnp.sum(jnp.where(half, k2, 0.0), axis=1, keepdims=True)
            n2 = jnp.sum(jnp.where(half, 0.0, k2), axis=1, keepdims=True)
            both = jnp.where(half, jnp.max(n1, axis=0, keepdims=True), jnp.max(n2, axis=0, keepdims=True))
            best = jnp.maximum(best, both)
        kmax_ref[...] = jnp.broadcast_to(jnp.sqrt(best), kmax_ref.shape)

    chunks = []
    for j, k in enumerate(k_refs):
        size = min(k.shape[1], STREAM_CHUNK)
        chunks += [(j, c0, size) for c0 in range(0, k.shape[1], size)]

    def component(q, first):
        mask = half if first else jnp.logical_not(half)
        qc = jnp.where(mask, q, jnp.zeros_like(q))
        qf = qc.astype(F32)
        kmax = kmax_ref[0:1, 0:1] if first else kmax_ref[0:1, DIFF_QK_DIM:DIFF_QK_DIM + 1]
        bound = jnp.sqrt(jnp.sum(qf * qf, axis=1, keepdims=True)) * kmax
        score = lambda ch: lax.dot_general(qc, k_refs[ch[0]][0, ch[1]:ch[1] + ch[2], :], nt,
                                           preferred_element_type=F32)
        acc = None
        s_next = score(chunks[0])
        for idx, (j, c0, size) in enumerate(chunks):
            s = s_next
            if idx + 1 < len(chunks):
                s_next = score(chunks[idx + 1])
            e = jnp.exp2(s - bound).astype(BF16)
            part = jnp.dot(e, vaug_refs[j][c0:c0 + size, :], preferred_element_type=F32)
            acc = part if acc is None else acc + part
        return acc[:, 0:DIFF_V_DIM], acc[:, DIFF_V_DIM:2 * DIFF_V_DIM]

    def exact_component(q, first):
        mask = half if first else jnp.logical_not(half)
        qc = jnp.where(mask, q, jnp.zeros_like(q))
        s = [lax.dot_general(qc, k[0], nt, preferred_element_type=F32) for k in k_refs]
        m = functools.reduce(jnp.maximum, [jnp.max(x, axis=1, keepdims=True) for x in s])
        acc = sum(jnp.dot(jnp.exp2(x - m).astype(BF16), va[...], preferred_element_type=F32)
                  for x, va in zip(s, vaug_refs))
        return acc[:, 0:DIFF_V_DIM], acc[:, DIFF_V_DIM:2 * DIFF_V_DIM]

    def emit(r, parts):
        (o1, l1), (o2, l2) = parts
        o = o1 / l1 - lam * (o2 / l2)
        y = o * lax.rsqrt(jnp.mean(o * o, axis=1, keepdims=True) + SUBLN_EPS)
        o_ref[r * sub:(r + 1) * sub, :] = (y * sg_ref[...] * (1.0 - lam_init)).astype(o_ref.dtype)
        return jnp.minimum(jnp.min(l1), jnp.min(l2))

    n_sub = q_ref.shape[0] // sub
    tiles = [q_ref[r * sub:(r + 1) * sub, :] for r in range(n_sub)]
    smallest_sum = functools.reduce(
        jnp.minimum, [emit(r, (component(q, True), component(q, False))) for r, q in enumerate(tiles)])

    @pl.when(jnp.logical_not(smallest_sum >= SUM_FLOOR))
    def _():
        for r, q in enumerate(tiles):
            emit(r, (exact_component(q, True), exact_component(q, False)))


def _attention_stream(lq, q, ks, vs, sg, lam_init, tq, sub):
    t = q.shape[0]
    b = ks[0].shape[0]
    nq = t // b // tq
    qo = pl.BlockSpec((tq, DIFF_V_DIM), lambda bi, h, i: (bi * nq + i, h))
    src = lambda a: pl.BlockSpec((1, a.shape[1], DIFF_V_DIM), lambda bi, h, i: (bi, 0, h))
    return pl.pallas_call(
        functools.partial(_attn_stream_kernel, lam_init=lam_init, n_src=len(ks), sub=sub),
        grid=(b, N_DIFF_HEADS, nq),
        in_specs=[pl.BlockSpec(lq.shape, lambda bi, h, i: (0, 0)), qo] + [src(a) for a in ks + vs]
                 + [pl.BlockSpec((1, DIFF_V_DIM), lambda bi, h, i: (0, 0))],
        out_specs=qo,
        out_shape=jax.ShapeDtypeStruct((t, D_ATTN), BF16),
        scratch_shapes=[pltpu.VMEM((v.shape[1], 2 * DIFF_V_DIM), BF16) for v in vs]
                       + [pltpu.VMEM((SUBLANES, 2 * DIFF_QK_DIM), F32)],
        compiler_params=_params(3),
        name="diff_attention_stream",
    )(lq, q, *ks, *vs, sg.reshape(1, DIFF_V_DIM))


MLP_CHUNK = 512


def _mlp_residual(h, mlp_refs, alpha):
    sh_ref, sc_ref, w1_ref, w2_ref, gate_ref, lg_ref, lb_ref = mlp_refs
    u = _modulated(h, sh_ref, sc_ref)
    y = jnp.zeros(h.shape, F32)
    for c in range(w1_ref.shape[1] // MLP_CHUNK):
        hid = jnp.dot(u, w1_ref[:, c * MLP_CHUNK:(c + 1) * MLP_CHUNK], preferred_element_type=F32)
        hid = jnp.square(jnp.maximum(hid, 0.0)).astype(BF16)
        y = y + jnp.dot(hid, w2_ref[c * MLP_CHUNK:(c + 1) * MLP_CHUNK, :], preferred_element_type=F32)
    return _post_norm(h, y, gate_ref, lg_ref, lb_ref, alpha)


class _MlpTail:
    def __init__(self, w1, w2, mod, layer, row_of_step, ln_g, ln_b):
        d = mod.d
        vec = pl.BlockSpec((1, d), lambda *_: (0, 0))
        self.specs = [mod.spec(layer, 3, row_of_step), mod.spec(layer, 4, row_of_step), w1.spec(),
                      w2.spec(), mod.spec(layer, 5, row_of_step), vec, vec]
        self.args = [mod.rows, mod.rows, w1.array, w2.array, mod.rows, ln_g.reshape(1, d),
                     ln_b.reshape(1, d)]


def _mix_out_kernel(h_ref, gb_ref, g_ref, gprev_ref, gnext_ref, a_ref, w_ref, cw_ref,
                    gate_ref, lg_ref, lb_ref, *rest, steps_per_seq, sub, alpha):
    mlp_refs, o_ref = rest[:-1], rest[-1]
    i = pl.program_id(0)
    g = g_ref[...]
    tm = g.shape[0]
    row = lax.broadcasted_iota(jnp.int32, g.shape, 0)
    pos = i % steps_per_seq
    keep_prev = jnp.where(pos == 0, 0.0, 1.0)
    keep_next = jnp.where(pos == steps_per_seq - 1, 0.0, 1.0)
    before = jnp.where(row == 0, gprev_ref[SUBLANES - 1:SUBLANES, :] * keep_prev, pltpu.roll(g, 1, 0))
    after = jnp.where(row == tm - 1, gnext_ref[0:1, :] * keep_next, pltpu.roll(g, tm - 1, 0))
    cw = cw_ref[...]
    conv = before * cw[0:1] + g * cw[1:2] + after * cw[2:3]
    yc = (gb_ref[...] * conv).astype(BF16)
    subs = range(0, tm, sub)
    ys = [jnp.dot(yc[lo:lo + sub], w_ref[0:D_CONV, :], preferred_element_type=F32)
          + jnp.dot(a_ref[lo:lo + sub, :], w_ref[D_CONV:D_CONV + D_ATTN, :], preferred_element_type=F32)
          for lo in subs]
    h1s = [_post_norm(h_ref[lo:lo + sub, :], y, gate_ref, lg_ref, lb_ref, alpha) for lo, y in zip(subs, ys)]
    for lo, h1 in zip(subs, h1s):
        o_ref[lo:lo + sub, :] = _mlp_residual(h1, mlp_refs, alpha)


def _mix_out(h, gb, g, attn, w_bf, conv_w, mod, layer, row_of_step, ln_g, ln_b, mlp, tm, sub, seq, alpha):
    t, d = h.shape
    steps_per_seq = seq // tm
    blocks8 = tm // SUBLANES
    last8 = t // SUBLANES - 1
    tok = lambda i: (i, 0)
    vec = pl.BlockSpec((1, d), lambda i: (0, 0))
    return pl.pallas_call(
        functools.partial(_mix_out_kernel, steps_per_seq=steps_per_seq, sub=sub, alpha=alpha),
        grid=(t // tm,),
        in_specs=[pl.BlockSpec((tm, d), tok), pl.BlockSpec((tm, D_CONV), tok),
                  pl.BlockSpec((tm, D_CONV), tok),
                  pl.BlockSpec((SUBLANES, D_CONV), lambda i: (jnp.maximum(i * blocks8 - 1, 0), 0)),
                  pl.BlockSpec((SUBLANES, D_CONV), lambda i: (jnp.minimum((i + 1) * blocks8, last8), 0)),
                  pl.BlockSpec((tm, D_ATTN), tok), w_bf.spec(),
                  pl.BlockSpec(conv_w.shape, lambda i: (0, 0)),
                  mod.spec(layer, 2, row_of_step), vec, vec] + mlp.specs,
        out_specs=pl.BlockSpec((tm, d), tok),
        out_shape=jax.ShapeDtypeStruct((t, d), F32),
        compiler_params=_params(1),
        name="mix_out_mlp",
    )(h, gb, g, g, g, attn, w_bf.array, conv_w, mod.rows, ln_g.reshape(1, d), ln_b.reshape(1, d),
      *mlp.args)


def _mlp_kernel(h_ref, *rest, alpha):
    rest[-1][...] = _mlp_residual(h_ref[...], rest[:-1], alpha)


def _mlp(h, mlp, tm, alpha):
    t, d = h.shape
    tok = lambda i: (i, 0)
    return pl.pallas_call(
        functools.partial(_mlp_kernel, alpha=alpha),
        grid=(t // tm,),
        in_specs=[pl.BlockSpec((tm, d), tok)] + mlp.specs,
        out_specs=pl.BlockSpec((tm, d), tok),
        out_shape=jax.ShapeDtypeStruct((t, d), F32),
        compiler_params=_params(1),
        name="mlp",
    )(h, *mlp.args)


def _fourier_in_kernel(x_ref, sh_ref, sc_ref, gcs_ref, z_ref):
    u = _modulated(x_ref[...], sh_ref, sc_ref)
    gcs = gcs_ref[...]
    for g in range(u.shape[1] // FOURIER_GROUP):
        lo, hi = g * FOURIER_GROUP, (g + 1) * FOURIER_GROUP
        zz = jnp.dot(u[:, lo:hi], gcs, preferred_element_type=F32)
        z_ref[0, 0, :, lo:hi] = zz[:, 0:FOURIER_GROUP].astype(BF16)
        z_ref[0, 1, :, lo:hi] = zz[:, FOURIER_GROUP:2 * FOURIER_GROUP].astype(BF16)


def _fourier_in(x, mod, layer, row_of_step, gcs, tm, seq):
    t, d = x.shape
    steps_per_seq = seq // tm
    return pl.pallas_call(
        _fourier_in_kernel,
        grid=(t // tm,),
        in_specs=[pl.BlockSpec((tm, d), lambda i: (i, 0)), mod.spec(layer, 0, row_of_step),
                  mod.spec(layer, 1, row_of_step), _const_spec(gcs.shape)],
        out_specs=pl.BlockSpec((1, 2, tm, d), lambda i: (i // steps_per_seq, 0, i % steps_per_seq, 0)),
        out_shape=jax.ShapeDtypeStruct((t // seq, 2, seq, d), BF16),
        compiler_params=_params(1),
        name="fourier_in",
    )(x, mod.rows, mod.rows, gcs)


FOURIER_SLAB = 16
FOURIER_OUT_SLAB = 8


def _fourier_rows_kernel(x_ref, sh_ref, sc_ref, k1_ref, gm_ref, z_ref):
    w, nb, d = x_ref.shape[1], x_ref.shape[2], x_ref.shape[3]
    rows = w * nb
    u = _modulated(x_ref[0].reshape(rows, d), sh_ref, sc_ref)
    y = jnp.dot(k1_ref[0], u, preferred_element_type=F32).astype(BF16)
    gm = gm_ref[...]
    for g in range(d // FOURIER_GROUP):
        lo, hi = g * FOURIER_GROUP, (g + 1) * FOURIER_GROUP
        yy = jnp.concatenate([y[0:rows, lo:hi], y[rows:2 * rows, lo:hi]], axis=1)
        zz = jnp.dot(yy, gm, preferred_element_type=F32).astype(BF16)
        z_ref[0, 0, :, :, lo:hi] = zz[:, 0:FOURIER_GROUP].reshape(w, nb, FOURIER_GROUP)
        z_ref[0, 1, :, :, lo:hi] = zz[:, FOURIER_GROUP:2 * FOURIER_GROUP].reshape(w, nb, FOURIER_GROUP)


def _fourier_rows(h, mod, layer, k1, gm, batch):
    t, d = h.shape
    w, nb = GRID_W, FOURIER_SLAB
    h4 = h.reshape(batch, w, w, d)
    return pl.pallas_call(
        _fourier_rows_kernel,
        grid=(w // nb, batch),
        in_specs=[pl.BlockSpec((1, w, nb, d), lambda j, bi: (bi, 0, j, 0)),
                  mod.spec(layer, 0, lambda j, bi: bi), mod.spec(layer, 1, lambda j, bi: bi),
                  pl.BlockSpec((1,) + k1.shape[1:], lambda j, bi: (j, 0, 0)), _const_spec(gm.shape)],
        out_specs=pl.BlockSpec((1, 2, w, nb, d), lambda j, bi: (bi, 0, 0, j, 0)),
        out_shape=jax.ShapeDtypeStruct((batch, 2, w, w, d), BF16),
        compiler_params=_params(2),
        name="fourier_rows",
    )(h4, mod.rows, mod.rows, k1, gm)


def _fourier_cols_kernel(z_ref, h_ref, k2_ref, w_ref, gate_ref, lg_ref, lb_ref, *rest, alpha):
    mlp_refs, o_ref = rest[:-1], rest[-1]
    w, d = z_ref.shape[3], z_ref.shape[4]
    nd = FOURIER_OUT_SLAB
    slabs = range(0, z_ref.shape[2], nd)
    ys = []
    for s in slabs:
        zz = z_ref[0, :, s:s + nd].reshape(2 * nd * w, d)
        f = jnp.dot(k2_ref[...], zz, preferred_element_type=F32).astype(BF16)
        ys.append(jnp.dot(f, w_ref[...], preferred_element_type=F32))
    h1s = [_post_norm(h_ref[0, :, s:s + nd, :].reshape(w * nd, d), y, gate_ref, lg_ref, lb_ref, alpha)
           for s, y in zip(slabs, ys)]
    for s, h1 in zip(slabs, h1s):
        o_ref[0, :, s:s + nd, :] = _mlp_residual(h1, mlp_refs, alpha).reshape(w, nd, d)


def _fourier_cols(z, h, k2, w_bf, mod, layer, ln_g, ln_b, mlp, alpha):
    batch, _, w, _, d = z.shape
    nd = 2 * FOURIER_OUT_SLAB
    h4 = h.reshape(batch, w, w, d)
    vec = pl.BlockSpec((1, d), lambda bi, j: (0, 0))
    hblk = pl.BlockSpec((1, w, nd, d), lambda bi, j: (bi, 0, j, 0))
    out = pl.pallas_call(
        functools.partial(_fourier_cols_kernel, alpha=alpha),
        grid=(batch, w // nd),
        in_specs=[pl.BlockSpec((1, 2, nd, w, d), lambda bi, j: (bi, 0, j, 0, 0)), hblk,
                  _const_spec(k2.shape), w_bf.spec(),
                  mod.spec(layer, 2, lambda bi, j: bi), vec, vec] + mlp.specs,
        out_specs=hblk,
        out_shape=jax.ShapeDtypeStruct((batch, w, w, d), F32),
        compiler_params=_params(2),
        name="fourier_cols_mlp",
    )(z, h4, k2, w_bf.array, mod.rows, ln_g.reshape(1, d), ln_b.reshape(1, d), *mlp.args)
    return out.reshape(batch * w * w, d)


def _fourier_dense_kernel(z_ref, cs_ref, o_ref, *, norm):
    zc = jnp.concatenate([z_ref[0, 0], z_ref[0, 1]], axis=0)
    o_ref[0] = (jnp.dot(cs_ref[...], zc, preferred_element_type=F32) * norm).astype(BF16)


def _fourier_dense(z, cs, norm):
    b, _, seq, d = z.shape
    return pl.pallas_call(
        functools.partial(_fourier_dense_kernel, norm=norm),
        grid=(b,),
        in_specs=[pl.BlockSpec((1, 2, seq, d), lambda bi: (bi, 0, 0, 0)), _const_spec(cs.shape)],
        out_specs=pl.BlockSpec((1, seq, d), lambda bi: (bi, 0, 0)),
        out_shape=jax.ShapeDtypeStruct((b, seq, d), BF16),
        compiler_params=_params(1),
        name="fourier_dense",
    )(z, cs)


def _proj_out_kernel(h_ref, a_ref, w_ref, gate_ref, lg_ref, lb_ref, o_ref, *, alpha):
    y = jnp.dot(a_ref[...], w_ref[...], preferred_element_type=F32)
    o_ref[...] = _post_norm(h_ref[...], y, gate_ref, lg_ref, lb_ref, alpha)


def _proj_out(h, a, w_bf, mod, layer, row_of_step, ln_g, ln_b, tm, alpha):
    t, d = h.shape
    tok = lambda i: (i, 0)
    vec = pl.BlockSpec((1, d), lambda i: (0, 0))
    return pl.pallas_call(
        functools.partial(_proj_out_kernel, alpha=alpha),
        grid=(t // tm,),
        in_specs=[pl.BlockSpec((tm, d), tok), pl.BlockSpec((tm, a.shape[1]), tok),
                  w_bf.spec(), mod.spec(layer, 2, row_of_step), vec, vec],
        out_specs=pl.BlockSpec((tm, d), tok),
        out_shape=jax.ShapeDtypeStruct((t, d), F32),
        compiler_params=_params(1),
        name="proj_out",
    )(h, a, w_bf.array, mod.rows, ln_g.reshape(1, d), ln_b.reshape(1, d))


def _dft_cos_sin(n, denom):
    idx = np.arange(n, dtype=np.int64)
    ang = ((idx[:, None] * idx[None, :]) % denom) * (2.0 * math.pi / denom)
    return np.cos(ang), np.sin(ang)


def _bf16_const(a):
    return jnp.asarray(np.asarray(a, np.float32)).astype(BF16)


def _rope_tables(seq):
    half = DIFF_QK_DIM // 4
    t = np.arange(seq)
    inv = ROPE_BASE ** (-np.arange(half, dtype=np.float64) / half)
    lane = np.arange(LANES)
    use_col = (lane % DIFF_QK_DIM) >= DIFF_QK_DIM // 2
    second = (lane % (2 * half)) >= half
    pos = np.where(use_col[None, :], (t % GRID_W)[:, None], (t // GRID_W)[:, None])
    ang = pos * inv[lane % half][None, :]
    cos, sin = np.cos(ang), np.sin(ang)
    tabs = (cos, np.where(second[None, :], 0.0, -sin), np.where(second[None, :], sin, 0.0))
    return tuple(jnp.asarray(a, F32) for a in tabs)


def _fourier_tables(seq, norm):
    w, nb, nd = GRID_W, FOURIER_SLAB, FOURIER_OUT_SLAB
    idx = np.arange(w, dtype=np.int64)
    slab = np.arange(w // nb, dtype=np.int64)
    col = np.arange(nb, dtype=np.int64)
    t = idx[None, None, None, :] * w + slab[:, None, None, None] * nb + col[None, None, :, None]
    ang = ((idx[None, :, None, None] * t) % seq) * (2.0 * math.pi / seq)
    trig = np.stack([np.cos(ang), -np.sin(ang)], axis=1)
    k1 = trig[..., None] * np.eye(nb)[None, None, None, :, None, :]
    c64, s64 = _dft_cos_sin(w, w)
    cs = np.stack([c64, s64], axis=1) * norm
    k2 = cs[:, None, :, None, :] * np.eye(nd)[None, :, None, :, None]
    return (_bf16_const(k1.reshape(w // nb, 2 * w * nb, w * nb)),
            _bf16_const(k2.reshape(w * nd, 2 * nd * w)))


def _no_rope_tables(seq):
    return jnp.ones((seq, LANES), F32), jnp.zeros((seq, LANES), F32), jnp.zeros((seq, LANES), F32)


def kernel(x, c, ctx, c_ctx, ada_w, ada_b, ln_g, ln_b, mlp_w1, mlp_w2, w_in, conv_w, lambda_qk,
           subln_g, w_out_mix, w_out_fourier):
    batch, seq, d = x.shape
    ctx_len = ctx.shape[1]
    depth = ada_w.shape[0]
    alpha = (2.0 * depth) ** 0.25
    last_attn = 2 * ((depth - 1) // 2)
    tm = 512
    assert seq == GRID_W * GRID_W and seq % tm == 0 and batch < SUBLANES and ctx_len % SUBLANES == 0

    cond8 = jnp.zeros((SUBLANES, d), F32).at[:batch].set(c).at[batch].set(c_ctx)
    mod = _Mod(_modulation(cond8, ada_w, ada_b), d)
    lat_row = lambda i: i // (seq // tm)
    ctx_row = lambda i: batch

    w_in_bf = w_in.astype(BF16)
    w_mix_bf = w_out_mix.astype(BF16)
    w_four_bf = w_out_fourier.astype(BF16)
    w1_bf = mlp_w1.astype(BF16)
    w2_bf = mlp_w2.astype(BF16)

    rope = _rope_tables(seq)
    no_rope = _no_rope_tables(ctx_len)

    gc, gs = _dft_cos_sin(FOURIER_GROUP, FOURIER_GROUP)
    gcs = _bf16_const(np.concatenate([gc, -gs], axis=1))
    gmat = _bf16_const(np.block([[gc, -gs], [gs, gc]]))
    k1, k2 = _fourier_tables(seq, (seq * FOURIER_GROUP) ** -0.5)
    cc, sc_ = _dft_cos_sin(ctx_len, ctx_len)
    cs_ctx = _bf16_const(np.concatenate([cc, sc_], axis=1))

    h = x.reshape(batch * seq, d)
    hc = ctx.reshape(batch * ctx_len, d)

    for i in range(depth):
        j = i // 2
        update_ctx = i < last_attn
        w1, w2 = _Stacked(w1_bf, i), _Stacked(w2_bf, i)
        mlp_ctx = _MlpTail(w1, w2, mod, i, ctx_row, ln_g[i, 1], ln_b[i, 1])
        if i % 2 == 0:
            lam_init = 0.8 - 0.6 * math.exp(-0.3 * i)
            w_in_j, w_mix_j = _Stacked(w_in_bf, j), _Stacked(w_mix_bf, j)
            mix_row = lambda t: t // (seq // (2 * tm))
            gb, g, q, k, v = _inproj(h, mod, i, mix_row, w_in_j, rope, 2 * tm, tm, seq)
            gbc, gcx, qc, kc, vc = _inproj(hc, mod, i, ctx_row, w_in_j, no_rope, ctx_len, ctx_len, ctx_len)
            attn = _attention_stream(lambda_qk[j], q, [k, kc], [v, vc], subln_g[j], lam_init, 1024, 256)
            mlp_lat = _MlpTail(w1, w2, mod, i, mix_row, ln_g[i, 1], ln_b[i, 1])
            h = _mix_out(h, gb, g, attn, w_mix_j, conv_w[j], mod, i, mix_row,
                         ln_g[i, 0], ln_b[i, 0], mlp_lat, 2 * tm, tm, seq, alpha)
            if update_ctx:
                attn_c = _attention(lambda_qk[j], qc, [kc], [vc], subln_g[j], lam_init, ctx_len, ctx_len)
                hc = _mix_out(hc, gbc, gcx, attn_c, w_mix_j, conv_w[j], mod, i, ctx_row,
                              ln_g[i, 0], ln_b[i, 0], mlp_ctx, ctx_len, ctx_len, ctx_len, alpha)
        else:
            w_four_j = _Stacked(w_four_bf, j)
            z = _fourier_rows(h, mod, i, k1, gmat, batch)
            mlp_slab = _MlpTail(w1, w2, mod, i, lambda bi, sl: bi, ln_g[i, 1], ln_b[i, 1])
            h = _fourier_cols(z, h, k2, w_four_j, mod, i, ln_g[i, 0], ln_b[i, 0], mlp_slab, alpha)
            if update_ctx:
                zc = _fourier_in(hc, mod, i, ctx_row, gcs, ctx_len, ctx_len)
                fc = _fourier_dense(zc, cs_ctx, (ctx_len * FOURIER_GROUP) ** -0.5)
                hc1 = _proj_out(hc, fc.reshape(batch * ctx_len, d), w_four_j, mod, i, ctx_row,
                                ln_g[i, 0], ln_b[i, 0], ctx_len, alpha)
                hc = _mlp(hc1, mlp_ctx, ctx_len, alpha)
    return h.reshape(batch, seq, d)
```

```python
import functools
import math

import jax
import jax.numpy as jnp
import numpy as np
from jax import lax
from jax.experimental import pallas as pl
from jax.experimental.pallas import tpu as pltpu

F32 = jnp.float32
BF16 = jnp.bfloat16

GRID_W = 64
CONV_WIDTH = 3
D_CONV = 512
N_DIFF_HEADS = 4
DIFF_QK_DIM = 64
DIFF_V_DIM = 128
D_ATTN = N_DIFF_HEADS * DIFF_V_DIM
FOURIER_GROUP = 128
ROPE_BASE = 10000.0
LN_EPS = 1e-6
SUBLN_EPS = 1e-5
COL_CONV = 3 * D_CONV
COL_QK = N_DIFF_HEADS * 2 * DIFF_QK_DIM
Q0 = COL_CONV
K0 = Q0 + COL_QK
V0 = K0 + COL_QK
Q_SCALE = math.log2(math.e) * DIFF_QK_DIM ** -0.5

LANES = 128
SUBLANES = 8
VMEM_LIMIT = 56 * 1024 * 1024


def _params(n_axes):
    return pltpu.CompilerParams(
        dimension_semantics=("arbitrary",) * n_axes, vmem_limit_bytes=VMEM_LIMIT)


def _const_spec(shape):
    nd = len(shape)
    return pl.BlockSpec(shape, lambda *_: (0,) * nd, pipeline_mode=pl.Buffered(1))


class _Stacked:
    def __init__(self, array, index):
        self.array, self.index = array, index

    def spec(self):
        k, n = self.array.shape[1:]
        index = self.index
        return pl.BlockSpec((None, k, n), lambda *_: (index, 0, 0), pipeline_mode=pl.Buffered(1))


def _layer_norm(x):
    mu = jnp.mean(x, axis=-1, keepdims=True)
    xc = x - mu
    var = jnp.mean(xc * xc, axis=-1, keepdims=True)
    return xc * lax.rsqrt(var + LN_EPS)


def _modulated(x, shift_ref, scale_ref):
    return (_layer_norm(x) * (1.0 + scale_ref[0]) + shift_ref[0]).astype(BF16)


def _post_norm(h, y, gate_ref, g_ref, b_ref, alpha):
    return _layer_norm(alpha * h + gate_ref[0] * y) * g_ref[...] + b_ref[...]


def _mod_kernel(cond_ref, w_ref, b_ref, o_ref):
    c = cond_ref[...]
    a = (c * (1.0 / (1.0 + jnp.exp(-c)))).astype(BF16)
    o_ref[0] = jnp.dot(a, w_ref[0].astype(BF16), preferred_element_type=F32) + b_ref[0]


def _modulation(cond8, ada_w, ada_b):
    depth, d, n6 = ada_w.shape
    tn = 1536
    return pl.pallas_call(
        _mod_kernel,
        grid=(depth, n6 // tn),
        in_specs=[
            pl.BlockSpec((SUBLANES, d), lambda l, j: (0, 0)),
            pl.BlockSpec((1, d, tn), lambda l, j: (l, 0, j)),
            pl.BlockSpec((1, 1, tn), lambda l, j: (l, 0, j)),
        ],
        out_specs=pl.BlockSpec((1, SUBLANES, tn), lambda l, j: (l, 0, j)),
        out_shape=jax.ShapeDtypeStruct((depth, SUBLANES, n6), F32),
        compiler_params=_params(2),
        name="modulation",
    )(cond8, ada_w, ada_b.reshape(depth, 1, n6))


class _Mod:
    def __init__(self, table, d):
        depth = table.shape[0]
        self.rows = table.reshape(depth, SUBLANES, 6, d).transpose(0, 2, 1, 3).reshape(
            depth * 6 * SUBLANES, 1, d)
        self.d = d

    def spec(self, layer, which, row_of_step):
        base = (layer * 6 + which) * SUBLANES
        return pl.BlockSpec((1, 1, self.d), lambda *idx: (base + row_of_step(*idx), 0, 0))


def _rope_store(dst_ref, rows, p, cos_ref, sin_lo_ref, sin_hi_ref, scale):
    cos, sin_lo, sin_hi = cos_ref[rows, :], sin_lo_ref[rows, :], sin_hi_ref[rows, :]
    half = DIFF_QK_DIM // 4
    for c in range(p.shape[1] // LANES):
        x = p[:, c * LANES:(c + 1) * LANES]
        ahead = pltpu.roll(x, LANES - half, 1)
        behind = pltpu.roll(x, half, 1)
        r = x * cos + ahead * sin_lo + behind * sin_hi
        if scale != 1.0:
            r = r * scale
        dst_ref[rows, c * LANES:(c + 1) * LANES] = r.astype(dst_ref.dtype)


def _inproj_kernel(x_ref, sh_ref, sc_ref, w_ref, cos_ref, sin_lo_ref, sin_hi_ref,
                   gb_ref, g_ref, q_ref, k_ref, v_ref, *, sub):
    rope = (cos_ref, sin_lo_ref, sin_hi_ref)
    for lo in range(0, x_ref.shape[0], sub):
        rows = pl.ds(lo, sub)
        u = _modulated(x_ref[rows, :], sh_ref, sc_ref)
        pc = jnp.dot(u, w_ref[:, 0:COL_CONV], preferred_element_type=F32)
        gb_ref[rows, :] = pc[:, 0:D_CONV]
        g_ref[rows, :] = pc[:, D_CONV:2 * D_CONV] * pc[:, 2 * D_CONV:3 * D_CONV]
        pq = jnp.dot(u, w_ref[:, Q0:K0], preferred_element_type=F32)
        _rope_store(q_ref, rows, pq, *rope, Q_SCALE)
        pk = jnp.dot(u, w_ref[:, K0:V0], preferred_element_type=F32)
        _rope_store(k_ref.at[0], rows, pk, *rope, 1.0)
        v_ref[0, rows, :] = jnp.dot(u, w_ref[:, V0:V0 + D_ATTN], preferred_element_type=F32).astype(BF16)


def _inproj(x, mod, layer, row_of_step, w_bf, rope, tm, sub, seq):
    t, d = x.shape
    steps_per_seq = seq // tm
    tok = lambda i: (i, 0)
    kvi = lambda i: (i // steps_per_seq, i % steps_per_seq, 0)
    tab = pl.BlockSpec((tm, LANES), lambda i: (i % steps_per_seq, 0))
    out = lambda n: pl.BlockSpec((tm, n), tok)
    return pl.pallas_call(
        functools.partial(_inproj_kernel, sub=sub),
        grid=(t // tm,),
        in_specs=[pl.BlockSpec((tm, d), tok), mod.spec(layer, 0, row_of_step),
                  mod.spec(layer, 1, row_of_step), w_bf.spec(), tab, tab, tab],
        out_specs=[out(D_CONV), out(D_CONV), out(COL_QK),
                   pl.BlockSpec((1, tm, COL_QK), kvi), pl.BlockSpec((1, tm, D_ATTN), kvi)],
        out_shape=[jax.ShapeDtypeStruct((t, D_CONV), F32), jax.ShapeDtypeStruct((t, D_CONV), F32),
                   jax.ShapeDtypeStruct((t, COL_QK), BF16),
                   jax.ShapeDtypeStruct((t // seq, seq, COL_QK), BF16),
                   jax.ShapeDtypeStruct((t // seq, seq, D_ATTN), BF16)],
        compiler_params=_params(1),
        name="inproj",
    )(x, mod.rows, mod.rows, w_bf.array, *rope)


def _attn_kernel(lq_ref, q_ref, *refs, lam_init, n_src, sub):
    k_refs, v_refs = refs[:n_src], refs[n_src:2 * n_src]
    sg_ref, o_ref = refs[2 * n_src], refs[2 * n_src + 1]
    lq = lq_ref[...]
    lam = (jnp.exp(jnp.sum(lq[0:1] * lq[1:2], axis=1, keepdims=True))
           - jnp.exp(jnp.sum(lq[2:3] * lq[3:4], axis=1, keepdims=True)) + lam_init)
    nt = (((1,), (1,)), ((), ()))

    def scores(qc):
        return [lax.dot_general(qc, k[0], nt, preferred_element_type=F32) for k in k_refs]

    def softmax_parts(s):
        m = functools.reduce(jnp.maximum, [jnp.max(x, axis=1, keepdims=True) for x in s])
        e = [jnp.exp2(x - m) for x in s]
        return e, sum(jnp.sum(x, axis=1, keepdims=True) for x in e)

    def score_tile(r):
        q = q_ref[r * sub:(r + 1) * sub, :]
        lane = lax.broadcasted_iota(jnp.int32, q.shape, 1)
        zero = jnp.zeros_like(q)
        return (scores(jnp.where(lane < DIFF_QK_DIM, q, zero)),
                scores(jnp.where(lane >= DIFF_QK_DIM, q, zero)))

    def project(r, e1, l1, e2, l2):
        ratio = lam * l1 / l2
        o = sum(jnp.dot((a - ratio * b).astype(BF16), v[0], preferred_element_type=F32)
                for a, b, v in zip(e1, e2, v_refs)) / l1
        y = o * lax.rsqrt(jnp.mean(o * o, axis=1, keepdims=True) + SUBLN_EPS)
        o_ref[r * sub:(r + 1) * sub, :] = (y * sg_ref[...] * (1.0 - lam_init)).astype(o_ref.dtype)

    n_sub = q_ref.shape[0] // sub
    s_next = score_tile(0)
    parts = None
    for r in range(n_sub):
        s1, s2 = s_next
        cur = softmax_parts(s1) + softmax_parts(s2)
        if r + 1 < n_sub:
            s_next = score_tile(r + 1)
        if parts is not None:
            project(r - 1, *parts)
        parts = cur
    project(n_sub - 1, *parts)


def _attention(lq, q, ks, vs, sg, lam_init, tq, sub):
    t = q.shape[0]
    b = ks[0].shape[0]
    nq = t // b // tq
    qo = pl.BlockSpec((tq, DIFF_V_DIM), lambda bi, h, i: (bi * nq + i, h))
    src = lambda a: pl.BlockSpec((1, a.shape[1], DIFF_V_DIM), lambda bi, h, i: (bi, 0, h))
    return pl.pallas_call(
        functools.partial(_attn_kernel, lam_init=lam_init, n_src=len(ks), sub=sub),
        grid=(b, N_DIFF_HEADS, nq),
        in_specs=[pl.BlockSpec(lq.shape, lambda bi, h, i: (0, 0)), qo] + [src(a) for a in ks + vs]
                 + [pl.BlockSpec((1, DIFF_V_DIM), lambda bi, h, i: (0, 0))],
        out_specs=qo,
        out_shape=jax.ShapeDtypeStruct((t, D_ATTN), BF16),
        compiler_params=_params(3),
        name="diff_attention",
    )(lq, q, *ks, *vs, sg.reshape(1, DIFF_V_DIM))


STREAM_CHUNK = 256
SUM_FLOOR = 2.0 ** -80


def _attn_stream_kernel(lq_ref, q_ref, *refs, lam_init, n_src, sub):
    k_refs, v_refs = refs[:n_src], refs[n_src:2 * n_src]
    sg_ref, o_ref = refs[2 * n_src], refs[2 * n_src + 3]
    cast_in, cast_out = refs[2 * n_src + 1:2 * n_src + 3], refs[2 * n_src + 4:2 * n_src + 6]
    vaug_refs = refs[2 * n_src + 6:2 * n_src + 6 + n_src]
    kmax_ref = refs[-1]
    for src_ref, dst_ref in zip(cast_in, cast_out):
        dst_ref[...] = src_ref[...].astype(BF16)
    lq = lq_ref[...]
    lam = (jnp.exp(jnp.sum(lq[0:1] * lq[1:2], axis=1, keepdims=True))
           - jnp.exp(jnp.sum(lq[2:3] * lq[3:4], axis=1, keepdims=True)) + lam_init)
    nt = (((1,), (1,)), ((), ()))
    half = lax.broadcasted_iota(jnp.int32, (1, 2 * DIFF_QK_DIM), 1) < DIFF_QK_DIM

    @pl.when(pl.program_id(2) == 0)
    def _():
        best = jnp.zeros((1, 2 * DIFF_QK_DIM), F32)
        for k, v, va in zip(k_refs, v_refs, vaug_refs):
            va[:, 0:DIFF_V_DIM] = v[0]
            va[:, DIFF_V_DIM:2 * DIFF_V_DIM] = jnp.ones((va.shape[0], DIFF_V_DIM), BF16)
            k2 = jnp.square(k[0].astype(F32))
            n1 = jnp.sum(jnp.where(half, k2, 0.0), axis=1, keepdims=True)
            n2 = jnp.sum(jnp.where(half, 0.0, k2), axis=1, keepdims=True)
            both = jnp.where(half, jnp.max(n1, axis=0, keepdims=True), jnp.max(n2, axis=0, keepdims=True))
            best = jnp.maximum(best, both)
        kmax_ref[...] = jnp.broadcast_to(jnp.sqrt(best), kmax_ref.shape)

    chunks = []
    for j, k in enumerate(k_refs):
        size = min(k.shape[1], STREAM_CHUNK)
        chunks += [(j, c0, size) for c0 in range(0, k.shape[1], size)]

    def component(q, first):
        mask = half if first else jnp.logical_not(half)
        qc = jnp.where(mask, q, jnp.zeros_like(q))
        qf = qc.astype(F32)
        kmax = kmax_ref[0:1, 0:1] if first else kmax_ref[0:1, DIFF_QK_DIM:DIFF_QK_DIM + 1]
        bound = jnp.sqrt(jnp.sum(qf * qf, axis=1, keepdims=True)) * kmax
        score = lambda ch: lax.dot_general(qc, k_refs[ch[0]][0, ch[1]:ch[1] + ch[2], :], nt,
                                           preferred_element_type=F32)
        acc = None
        s_next = score(chunks[0])
        for idx, (j, c0, size) in enumerate(chunks):
            s = s_next
            if idx + 1 < len(chunks):
                s_next = score(chunks[idx + 1])
            e = jnp.exp2(s - bound).astype(BF16)
            part = jnp.dot(e, vaug_refs[j][c0:c0 + size, :], preferred_element_type=F32)
            acc = part if acc is None else acc + part
        return acc[:, 0:DIFF_V_DIM], acc[:, DIFF_V_DIM:2 * DIFF_V_DIM]

    def exact_component(q, first):
        mask = half if first else jnp.logical_not(half)
        qc = jnp.where(mask, q, jnp.zeros_like(q))
        s = [lax.dot_general(qc, k[0], nt, preferred_element_type=F32) for k in k_refs]
        m = functools.reduce(jnp.maximum, [jnp.max(x, axis=1, keepdims=True) for x in s])
        acc = sum(jnp.dot(jnp.exp2(x - m).astype(BF16), va[...], preferred_element_type=F32)
                  for x, va in zip(s, vaug_refs))
        return acc[:, 0:DIFF_V_DIM], acc[:, DIFF_V_DIM:2 * DIFF_V_DIM]

    def emit(r, parts):
        (o1, l1), (o2, l2) = parts
        o = o1 / l1 - lam * (o2 / l2)
        y = o * lax.rsqrt(jnp.mean(o * o, axis=1, keepdims=True) + SUBLN_EPS)
        o_ref[r * sub:(r + 1) * sub, :] = (y * sg_ref[...] * (1.0 - lam_init)).astype(o_ref.dtype)
        return jnp.minimum(jnp.min(l1), jnp.min(l2))

    n_sub = q_ref.shape[0] // sub
    tiles = [q_ref[r * sub:(r + 1) * sub, :] for r in range(n_sub)]
    smallest_sum = functools.reduce(
        jnp.minimum, [emit(r, (component(q, True), component(q, False))) for r, q in enumerate(tiles)])

    @pl.when(jnp.logical_not(smallest_sum >= SUM_FLOOR))
    def _():
        for r, q in enumerate(tiles):
            emit(r, (exact_component(q, True), exact_component(q, False)))


def _attention_stream(lq, q, ks, vs, sg, lam_init, tq, sub, weights, first_layer):
    t = q.shape[0]
    b = ks[0].shape[0]
    nq = t // b // tq
    steps = b * N_DIFF_HEADS * nq
    qo = pl.BlockSpec((tq, DIFF_V_DIM), lambda bi, h, i: (bi * nq + i, h))
    src = lambda a: pl.BlockSpec((1, a.shape[1], DIFF_V_DIM), lambda bi, h, i: (bi, 0, h))
    cast_in, cast_out, cast_shapes = [], [], []
    for w in weights:
        _, rows, cols = w.shape
        slab = 2 * rows // steps
        assert slab * steps == 2 * rows and slab % (2 * SUBLANES) == 0
        per_layer = rows // slab

        def index(bi, h, i, per_layer=per_layer, base=0):
            flat = (bi * N_DIFF_HEADS + h) * nq + i
            return base + flat // per_layer, flat % per_layer, 0

        cast_in.append(pl.BlockSpec((1, slab, cols), functools.partial(index, base=first_layer)))
        cast_out.append(pl.BlockSpec((1, slab, cols), index))
        cast_shapes.append(jax.ShapeDtypeStruct((2, rows, cols), BF16))
    return pl.pallas_call(
        functools.partial(_attn_stream_kernel, lam_init=lam_init, n_src=len(ks), sub=sub),
        grid=(b, N_DIFF_HEADS, nq),
        in_specs=[pl.BlockSpec(lq.shape, lambda bi, h, i: (0, 0)), qo] + [src(a) for a in ks + vs]
                 + [pl.BlockSpec((1, DIFF_V_DIM), lambda bi, h, i: (0, 0))] + cast_in,
        out_specs=[qo] + cast_out,
        out_shape=[jax.ShapeDtypeStruct((t, D_ATTN), BF16)] + cast_shapes,
        scratch_shapes=[pltpu.VMEM((v.shape[1], 2 * DIFF_V_DIM), BF16) for v in vs]
                       + [pltpu.VMEM((SUBLANES, 2 * DIFF_QK_DIM), F32)],
        compiler_params=_params(3),
        name="diff_attention_stream",
    )(lq, q, *ks, *vs, sg.reshape(1, DIFF_V_DIM), *weights)


MLP_CHUNK = 512


def _mlp_residual(h, mlp_refs, alpha):
    sh_ref, sc_ref, w1_ref, w2_ref, gate_ref, lg_ref, lb_ref = mlp_refs
    u = _modulated(h, sh_ref, sc_ref)
    y = jnp.zeros(h.shape, F32)
    for c in range(w1_ref.shape[1] // MLP_CHUNK):
        hid = jnp.dot(u, w1_ref[:, c * MLP_CHUNK:(c + 1) * MLP_CHUNK], preferred_element_type=F32)
        hid = jnp.square(jnp.maximum(hid, 0.0)).astype(BF16)
        y = y + jnp.dot(hid, w2_ref[c * MLP_CHUNK:(c + 1) * MLP_CHUNK, :], preferred_element_type=F32)
    return _post_norm(h, y, gate_ref, lg_ref, lb_ref, alpha)


class _MlpTail:
    def __init__(self, w1, w2, mod, layer, row_of_step, ln_g, ln_b):
        d = mod.d
        vec = pl.BlockSpec((1, d), lambda *_: (0, 0))
        self.specs = [mod.spec(layer, 3, row_of_step), mod.spec(layer, 4, row_of_step), w1.spec(),
                      w2.spec(), mod.spec(layer, 5, row_of_step), vec, vec]
        self.args = [mod.rows, mod.rows, w1.array, w2.array, mod.rows, ln_g.reshape(1, d),
                     ln_b.reshape(1, d)]


def _mix_out_kernel(h_ref, gb_ref, g_ref, gprev_ref, gnext_ref, a_ref, w_ref, cw_ref,
                    gate_ref, lg_ref, lb_ref, *rest, steps_per_seq, sub, alpha):
    mlp_refs, o_ref = rest[:-1], rest[-1]
    i = pl.program_id(0)
    g = g_ref[...]
    tm = g.shape[0]
    row = lax.broadcasted_iota(jnp.int32, g.shape, 0)
    pos = i % steps_per_seq
    keep_prev = jnp.where(pos == 0, 0.0, 1.0)
    keep_next = jnp.where(pos == steps_per_seq - 1, 0.0, 1.0)
    before = jnp.where(row == 0, gprev_ref[SUBLANES - 1:SUBLANES, :] * keep_prev, pltpu.roll(g, 1, 0))
    after = jnp.where(row == tm - 1, gnext_ref[0:1, :] * keep_next, pltpu.roll(g, tm - 1, 0))
    cw = cw_ref[...]
    conv = before * cw[0:1] + g * cw[1:2] + after * cw[2:3]
    yc = (gb_ref[...] * conv).astype(BF16)
    subs = range(0, tm, sub)
    ys = [jnp.dot(yc[lo:lo + sub], w_ref[0:D_CONV, :], preferred_element_type=F32)
          + jnp.dot(a_ref[lo:lo + sub, :], w_ref[D_CONV:D_CONV + D_ATTN, :], preferred_element_type=F32)
          for lo in subs]
    h1s = [_post_norm(h_ref[lo:lo + sub, :], y, gate_ref, lg_ref, lb_ref, alpha) for lo, y in zip(subs, ys)]
    for lo, h1 in zip(subs, h1s):
        o_ref[lo:lo + sub, :] = _mlp_residual(h1, mlp_refs, alpha)


def _mix_out(h, gb, g, attn, w_bf, conv_w, mod, layer, row_of_step, ln_g, ln_b, mlp, tm, sub, seq, alpha):
    t, d = h.shape
    steps_per_seq = seq // tm
    blocks8 = tm // SUBLANES
    last8 = t // SUBLANES - 1
    tok = lambda i: (i, 0)
    vec = pl.BlockSpec((1, d), lambda i: (0, 0))
    return pl.pallas_call(
        functools.partial(_mix_out_kernel, steps_per_seq=steps_per_seq, sub=sub, alpha=alpha),
        grid=(t // tm,),
        in_specs=[pl.BlockSpec((tm, d), tok), pl.BlockSpec((tm, D_CONV), tok),
                  pl.BlockSpec((tm, D_CONV), tok),
                  pl.BlockSpec((SUBLANES, D_CONV), lambda i: (jnp.maximum(i * blocks8 - 1, 0), 0)),
                  pl.BlockSpec((SUBLANES, D_CONV), lambda i: (jnp.minimum((i + 1) * blocks8, last8), 0)),
                  pl.BlockSpec((tm, D_ATTN), tok), w_bf.spec(),
                  pl.BlockSpec(conv_w.shape, lambda i: (0, 0)),
                  mod.spec(layer, 2, row_of_step), vec, vec] + mlp.specs,
        out_specs=pl.BlockSpec((tm, d), tok),
        out_shape=jax.ShapeDtypeStruct((t, d), F32),
        compiler_params=_params(1),
        name="mix_out_mlp",
    )(h, gb, g, g, g, attn, w_bf.array, conv_w, mod.rows, ln_g.reshape(1, d), ln_b.reshape(1, d),
      *mlp.args)


def _mlp_kernel(h_ref, *rest, alpha):
    rest[-1][...] = _mlp_residual(h_ref[...], rest[:-1], alpha)


def _mlp(h, mlp, tm, alpha):
    t, d = h.shape
    tok = lambda i: (i, 0)
    return pl.pallas_call(
        functools.partial(_mlp_kernel, alpha=alpha),
        grid=(t // tm,),
        in_specs=[pl.BlockSpec((tm, d), tok)] + mlp.specs,
        out_specs=pl.BlockSpec((tm, d), tok),
        out_shape=jax.ShapeDtypeStruct((t, d), F32),
        compiler_params=_params(1),
        name="mlp",
    )(h, *mlp.args)


def _fourier_in_kernel(x_ref, sh_ref, sc_ref, gcs_ref, z_ref):
    u = _modulated(x_ref[...], sh_ref, sc_ref)
    gcs = gcs_ref[...]
    for g in range(u.shape[1] // FOURIER_GROUP):
        lo, hi = g * FOURIER_GROUP, (g + 1) * FOURIER_GROUP
        zz = jnp.dot(u[:, lo:hi], gcs, preferred_element_type=F32)
        z_ref[0, 0, :, lo:hi] = zz[:, 0:FOURIER_GROUP].astype(BF16)
        z_ref[0, 1, :, lo:hi] = zz[:, FOURIER_GROUP:2 * FOURIER_GROUP].astype(BF16)


def _fourier_in(x, mod, layer, row_of_step, gcs, tm, seq):
    t, d = x.shape
    steps_per_seq = seq // tm
    return pl.pallas_call(
        _fourier_in_kernel,
        grid=(t // tm,),
        in_specs=[pl.BlockSpec((tm, d), lambda i: (i, 0)), mod.spec(layer, 0, row_of_step),
                  mod.spec(layer, 1, row_of_step), _const_spec(gcs.shape)],
        out_specs=pl.BlockSpec((1, 2, tm, d), lambda i: (i // steps_per_seq, 0, i % steps_per_seq, 0)),
        out_shape=jax.ShapeDtypeStruct((t // seq, 2, seq, d), BF16),
        compiler_params=_params(1),
        name="fourier_in",
    )(x, mod.rows, mod.rows, gcs)


FOURIER_SLAB = 16
FOURIER_OUT_SLAB = 8


def _fourier_rows_kernel(x_ref, sh_ref, sc_ref, k1_ref, gm_ref, z_ref):
    w, nb, d = x_ref.shape[1], x_ref.shape[2], x_ref.shape[3]
    rows = w * nb
    u = _modulated(x_ref[0].reshape(rows, d), sh_ref, sc_ref)
    y = jnp.dot(k1_ref[0], u, preferred_element_type=F32).astype(BF16)
    gm = gm_ref[...]
    for g in range(d // FOURIER_GROUP):
        lo, hi = g * FOURIER_GROUP, (g + 1) * FOURIER_GROUP
        yy = jnp.concatenate([y[0:rows, lo:hi], y[rows:2 * rows, lo:hi]], axis=1)
        zz = jnp.dot(yy, gm, preferred_element_type=F32).astype(BF16)
        z_ref[0, 0, :, :, lo:hi] = zz[:, 0:FOURIER_GROUP].reshape(w, nb, FOURIER_GROUP)
        z_ref[0, 1, :, :, lo:hi] = zz[:, FOURIER_GROUP:2 * FOURIER_GROUP].reshape(w, nb, FOURIER_GROUP)


def _fourier_rows(h, mod, layer, k1, gm, batch):
    t, d = h.shape
    w, nb = GRID_W, FOURIER_SLAB
    h4 = h.reshape(batch, w, w, d)
    return pl.pallas_call(
        _fourier_rows_kernel,
        grid=(w // nb, batch),
        in_specs=[pl.BlockSpec((1, w, nb, d), lambda j, bi: (bi, 0, j, 0)),
                  mod.spec(layer, 0, lambda j, bi: bi), mod.spec(layer, 1, lambda j, bi: bi),
                  pl.BlockSpec((1,) + k1.shape[1:], lambda j, bi: (j, 0, 0)), _const_spec(gm.shape)],
        out_specs=pl.BlockSpec((1, 2, w, nb, d), lambda j, bi: (bi, 0, 0, j, 0)),
        out_shape=jax.ShapeDtypeStruct((batch, 2, w, w, d), BF16),
        compiler_params=_params(2),
        name="fourier_rows",
    )(h4, mod.rows, mod.rows, k1, gm)


def _fourier_cols_kernel(z_ref, h_ref, k2_ref, w_ref, gate_ref, lg_ref, lb_ref, *rest, alpha):
    mlp_refs, o_ref = rest[:-1], rest[-1]
    w, d = z_ref.shape[3], z_ref.shape[4]
    nd = FOURIER_OUT_SLAB
    slabs = range(0, z_ref.shape[2], nd)
    ys = []
    for s in slabs:
        zz = z_ref[0, :, s:s + nd].reshape(2 * nd * w, d)
        f = jnp.dot(k2_ref[...], zz, preferred_element_type=F32).astype(BF16)
        ys.append(jnp.dot(f, w_ref[...], preferred_element_type=F32))
    h1s = [_post_norm(h_ref[0, :, s:s + nd, :].reshape(w * nd, d), y, gate_ref, lg_ref, lb_ref, alpha)
           for s, y in zip(slabs, ys)]
    for s, h1 in zip(slabs, h1s):
        o_ref[0, :, s:s + nd, :] = _mlp_residual(h1, mlp_refs, alpha).reshape(w, nd, d)


def _fourier_cols(z, h, k2, w_bf, mod, layer, ln_g, ln_b, mlp, alpha):
    batch, _, w, _, d = z.shape
    nd = 2 * FOURIER_OUT_SLAB
    h4 = h.reshape(batch, w, w, d)
    vec = pl.BlockSpec((1, d), lambda bi, j: (0, 0))
    hblk = pl.BlockSpec((1, w, nd, d), lambda bi, j: (bi, 0, j, 0))
    out = pl.pallas_call(
        functools.partial(_fourier_cols_kernel, alpha=alpha),
        grid=(batch, w // nd),
        in_specs=[pl.BlockSpec((1, 2, nd, w, d), lambda bi, j: (bi, 0, j, 0, 0)), hblk,
                  _const_spec(k2.shape), w_bf.spec(),
                  mod.spec(layer, 2, lambda bi, j: bi), vec, vec] + mlp.specs,
        out_specs=hblk,
        out_shape=jax.ShapeDtypeStruct((batch, w, w, d), F32),
        compiler_params=_params(2),
        name="fourier_cols_mlp",
    )(z, h4, k2, w_bf.array, mod.rows, ln_g.reshape(1, d), ln_b.reshape(1, d), *mlp.args)
    return out.reshape(batch * w * w, d)


def _fourier_dense_kernel(z_ref, cs_ref, o_ref, *, norm):
    zc = jnp.concatenate([z_ref[0, 0], z_ref[0, 1]], axis=0)
    o_ref[0] = (jnp.dot(cs_ref[...], zc, preferred_element_type=F32) * norm).astype(BF16)


def _fourier_dense(z, cs, norm):
    b, _, seq, d = z.shape
    return pl.pallas_call(
        functools.partial(_fourier_dense_kernel, norm=norm),
        grid=(b,),
        in_specs=[pl.BlockSpec((1, 2, seq, d), lambda bi: (bi, 0, 0, 0)), _const_spec(cs.shape)],
        out_specs=pl.BlockSpec((1, seq, d), lambda bi: (bi, 0, 0)),
        out_shape=jax.ShapeDtypeStruct((b, seq, d), BF16),
        compiler_params=_params(1),
        name="fourier_dense",
    )(z, cs)


def _proj_out_kernel(h_ref, a_ref, w_ref, gate_ref, lg_ref, lb_ref, o_ref, *, alpha):
    y = jnp.dot(a_ref[...], w_ref[...], preferred_element_type=F32)
    o_ref[...] = _post_norm(h_ref[...], y, gate_ref, lg_ref, lb_ref, alpha)


def _proj_out(h, a, w_bf, mod, layer, row_of_step, ln_g, ln_b, tm, alpha):
    t, d = h.shape
    tok = lambda i: (i, 0)
    vec = pl.BlockSpec((1, d), lambda i: (0, 0))
    return pl.pallas_call(
        functools.partial(_proj_out_kernel, alpha=alpha),
        grid=(t // tm,),
        in_specs=[pl.BlockSpec((tm, d), tok), pl.BlockSpec((tm, a.shape[1]), tok),
                  w_bf.spec(), mod.spec(layer, 2, row_of_step), vec, vec],
        out_specs=pl.BlockSpec((tm, d), tok),
        out_shape=jax.ShapeDtypeStruct((t, d), F32),
        compiler_params=_params(1),
        name="proj_out",
    )(h, a, w_bf.array, mod.rows, ln_g.reshape(1, d), ln_b.reshape(1, d))


def _dft_cos_sin(n, denom):
    idx = np.arange(n, dtype=np.int64)
    ang = ((idx[:, None] * idx[None, :]) % denom) * (2.0 * math.pi / denom)
    return np.cos(ang), np.sin(ang)


def _bf16_const(a):
    return jnp.asarray(np.asarray(a, np.float32)).astype(BF16)


def _rope_tables(seq):
    half = DIFF_QK_DIM // 4
    t = np.arange(seq)
    inv = ROPE_BASE ** (-np.arange(half, dtype=np.float64) / half)
    lane = np.arange(LANES)
    use_col = (lane % DIFF_QK_DIM) >= DIFF_QK_DIM // 2
    second = (lane % (2 * half)) >= half
    pos = np.where(use_col[None, :], (t % GRID_W)[:, None], (t // GRID_W)[:, None])
    ang = pos * inv[lane % half][None, :]
    cos, sin = np.cos(ang), np.sin(ang)
    tabs = (cos, np.where(second[None, :], 0.0, -sin), np.where(second[None, :], sin, 0.0))
    return tuple(jnp.asarray(a, F32) for a in tabs)


def _fourier_tables(seq, norm):
    w, nb, nd = GRID_W, FOURIER_SLAB, FOURIER_OUT_SLAB
    idx = np.arange(w, dtype=np.int64)
    slab = np.arange(w // nb, dtype=np.int64)
    col = np.arange(nb, dtype=np.int64)
    t = idx[None, None, None, :] * w + slab[:, None, None, None] * nb + col[None, None, :, None]
    ang = ((idx[None, :, None, None] * t) % seq) * (2.0 * math.pi / seq)
    trig = np.stack([np.cos(ang), -np.sin(ang)], axis=1)
    k1 = trig[..., None] * np.eye(nb)[None, None, None, :, None, :]
    c64, s64 = _dft_cos_sin(w, w)
    cs = np.stack([c64, s64], axis=1) * norm
    k2 = cs[:, None, :, None, :] * np.eye(nd)[None, :, None, :, None]
    return (_bf16_const(k1.reshape(w // nb, 2 * w * nb, w * nb)),
            _bf16_const(k2.reshape(w * nd, 2 * nd * w)))


def _no_rope_tables(seq):
    return jnp.ones((seq, LANES), F32), jnp.zeros((seq, LANES), F32), jnp.zeros((seq, LANES), F32)


def kernel(x, c, ctx, c_ctx, ada_w, ada_b, ln_g, ln_b, mlp_w1, mlp_w2, w_in, conv_w, lambda_qk,
           subln_g, w_out_mix, w_out_fourier):
    batch, seq, d = x.shape
    ctx_len = ctx.shape[1]
    depth = ada_w.shape[0]
    alpha = (2.0 * depth) ** 0.25
    last_attn = 2 * ((depth - 1) // 2)
    tm = 512
    assert seq == GRID_W * GRID_W and seq % tm == 0 and batch < SUBLANES and ctx_len % SUBLANES == 0
    assert depth % 2 == 0

    cond8 = jnp.zeros((SUBLANES, d), F32).at[:batch].set(c).at[batch].set(c_ctx)
    mod = _Mod(_modulation(cond8, ada_w, ada_b), d)
    lat_row = lambda i: i // (seq // tm)
    ctx_row = lambda i: batch

    w_in_bf = w_in.astype(BF16)
    w_mix_bf = w_out_mix.astype(BF16)
    w_four_bf = w_out_fourier.astype(BF16)

    rope = _rope_tables(seq)
    no_rope = _no_rope_tables(ctx_len)

    gc, gs = _dft_cos_sin(FOURIER_GROUP, FOURIER_GROUP)
    gcs = _bf16_const(np.concatenate([gc, -gs], axis=1))
    gmat = _bf16_const(np.block([[gc, -gs], [gs, gc]]))
    k1, k2 = _fourier_tables(seq, (seq * FOURIER_GROUP) ** -0.5)
    cc, sc_ = _dft_cos_sin(ctx_len, ctx_len)
    cs_ctx = _bf16_const(np.concatenate([cc, sc_], axis=1))

    h = x.reshape(batch * seq, d)
    hc = ctx.reshape(batch * ctx_len, d)

    for i in range(depth):
        j = i // 2
        update_ctx = i < last_attn
        if i % 2 == 0:
            lam_init = 0.8 - 0.6 * math.exp(-0.3 * i)
            w_in_j, w_mix_j = _Stacked(w_in_bf, j), _Stacked(w_mix_bf, j)
            mix_row = lambda t: t // (seq // (2 * tm))
            gb, g, q, k, v = _inproj(h, mod, i, mix_row, w_in_j, rope, 2 * tm, tm, seq)
            gbc, gcx, qc, kc, vc = _inproj(hc, mod, i, ctx_row, w_in_j, no_rope, ctx_len, ctx_len, ctx_len)
            attn, w1_pair, w2_pair = _attention_stream(lambda_qk[j], q, [k, kc], [v, vc], subln_g[j],
                                                       lam_init, 1024, 256, (mlp_w1, mlp_w2), i)
            w1, w2 = _Stacked(w1_pair, 0), _Stacked(w2_pair, 0)
            mlp_ctx = _MlpTail(w1, w2, mod, i, ctx_row, ln_g[i, 1], ln_b[i, 1])
            mlp_lat = _MlpTail(w1, w2, mod, i, mix_row, ln_g[i, 1], ln_b[i, 1])
            h = _mix_out(h, gb, g, attn, w_mix_j, conv_w[j], mod, i, mix_row,
                         ln_g[i, 0], ln_b[i, 0], mlp_lat, 2 * tm, tm, seq, alpha)
            if update_ctx:
                attn_c = _attention(lambda_qk[j], qc, [kc], [vc], subln_g[j], lam_init, ctx_len, ctx_len)
                hc = _mix_out(hc, gbc, gcx, attn_c, w_mix_j, conv_w[j], mod, i, ctx_row,
                              ln_g[i, 0], ln_b[i, 0], mlp_ctx, ctx_len, ctx_len, ctx_len, alpha)
        else:
            w_four_j = _Stacked(w_four_bf, j)
            w1, w2 = _Stacked(w1_pair, 1), _Stacked(w2_pair, 1)
            mlp_ctx = _MlpTail(w1, w2, mod, i, ctx_row, ln_g[i, 1], ln_b[i, 1])
            z = _fourier_rows(h, mod, i, k1, gmat, batch)
            mlp_slab = _MlpTail(w1, w2, mod, i, lambda bi, sl: bi, ln_g[i, 1], ln_b[i, 1])
            h = _fourier_cols(z, h, k2, w_four_j, mod, i, ln_g[i, 0], ln_b[i, 0], mlp_slab, alpha)
            if update_ctx:
                zc = _fourier_in(hc, mod, i, ctx_row, gcs, ctx_len, ctx_len)
                fc = _fourier_dense(zc, cs_ctx, (ctx_len * FOURIER_GROUP) ** -0.5)
                hc1 = _proj_out(hc, fc.reshape(batch * ctx_len, d), w_four_j, mod, i, ctx_row,
                                ln_g[i, 0], ln_b[i, 0], ctx_len, alpha)
                hc = _mlp(hc1, mlp_ctx, ctx_len, alpha)
    return h.reshape(batch, seq, d)
```

```python
import functools
import math

import jax
import jax.numpy as jnp
import numpy as np
from jax import lax
from jax.experimental import pallas as pl
from jax.experimental.pallas import tpu as pltpu

F32 = jnp.float32
BF16 = jnp.bfloat16

GRID_W = 64
CONV_WIDTH = 3
D_CONV = 512
N_DIFF_HEADS = 4
DIFF_QK_DIM = 64
DIFF_V_DIM = 128
D_ATTN = N_DIFF_HEADS * DIFF_V_DIM
FOURIER_GROUP = 128
ROPE_BASE = 10000.0
LN_EPS = 1e-6
SUBLN_EPS = 1e-5
COL_CONV = 3 * D_CONV
COL_QK = N_DIFF_HEADS * 2 * DIFF_QK_DIM
Q0 = COL_CONV
K0 = Q0 + COL_QK
V0 = K0 + COL_QK
Q_SCALE = math.log2(math.e) * DIFF_QK_DIM ** -0.5

LANES = 128
SUBLANES = 8
VMEM_LIMIT = 56 * 1024 * 1024


def _params(n_axes):
    return pltpu.CompilerParams(
        dimension_semantics=("arbitrary",) * n_axes, vmem_limit_bytes=VMEM_LIMIT)


def _const_spec(shape):
    nd = len(shape)
    return pl.BlockSpec(shape, lambda *_: (0,) * nd, pipeline_mode=pl.Buffered(1))


class _Stacked:
    def __init__(self, array, index):
        self.array, self.index = array, index

    def spec(self):
        k, n = self.array.shape[1:]
        index = self.index
        return pl.BlockSpec((None, k, n), lambda *_: (index, 0, 0), pipeline_mode=pl.Buffered(1))


def _layer_norm(x):
    mu = jnp.mean(x, axis=-1, keepdims=True)
    xc = x - mu
    var = jnp.mean(xc * xc, axis=-1, keepdims=True)
    return xc * lax.rsqrt(var + LN_EPS)


def _modulated(x, shift_ref, scale_ref):
    return (_layer_norm(x) * (1.0 + scale_ref[0]) + shift_ref[0]).astype(BF16)


def _post_norm(h, y, gate_ref, g_ref, b_ref, alpha):
    return _layer_norm(alpha * h + gate_ref[0] * y) * g_ref[...] + b_ref[...]


def _mod_kernel(cond_ref, w_ref, b_ref, o_ref):
    c = cond_ref[...]
    a = (c * (1.0 / (1.0 + jnp.exp(-c)))).astype(BF16)
    o_ref[0] = jnp.dot(a, w_ref[0].astype(BF16), preferred_element_type=F32) + b_ref[0]


def _modulation(cond8, ada_w, ada_b):
    depth, d, n6 = ada_w.shape
    tn = 1536
    return pl.pallas_call(
        _mod_kernel,
        grid=(depth, n6 // tn),
        in_specs=[
            pl.BlockSpec((SUBLANES, d), lambda l, j: (0, 0)),
            pl.BlockSpec((1, d, tn), lambda l, j: (l, 0, j)),
            pl.BlockSpec((1, 1, tn), lambda l, j: (l, 0, j)),
        ],
        out_specs=pl.BlockSpec((1, SUBLANES, tn), lambda l, j: (l, 0, j)),
        out_shape=jax.ShapeDtypeStruct((depth, SUBLANES, n6), F32),
        compiler_params=_params(2),
        name="modulation",
    )(cond8, ada_w, ada_b.reshape(depth, 1, n6))


class _Mod:
    def __init__(self, table, d):
        depth = table.shape[0]
        self.rows = table.reshape(depth, SUBLANES, 6, d).transpose(0, 2, 1, 3).reshape(
            depth * 6 * SUBLANES, 1, d)
        self.d = d

    def spec(self, layer, which, row_of_step):
        base = (layer * 6 + which) * SUBLANES
        return pl.BlockSpec((1, 1, self.d), lambda *idx: (base + row_of_step(*idx), 0, 0))


def _rope_store(dst_ref, rows, p, cos_ref, sin_lo_ref, sin_hi_ref, scale):
    cos, sin_lo, sin_hi = cos_ref[rows, :], sin_lo_ref[rows, :], sin_hi_ref[rows, :]
    half = DIFF_QK_DIM // 4
    for c in range(p.shape[1] // LANES):
        x = p[:, c * LANES:(c + 1) * LANES]
        ahead = pltpu.roll(x, LANES - half, 1)
        behind = pltpu.roll(x, half, 1)
        r = x * cos + ahead * sin_lo + behind * sin_hi
        if scale != 1.0:
            r = r * scale
        dst_ref[rows, c * LANES:(c + 1) * LANES] = r.astype(dst_ref.dtype)


def _inproj_kernel(x_ref, sh_ref, sc_ref, w_ref, cos_ref, sin_lo_ref, sin_hi_ref,
                   gb_ref, g_ref, q_ref, k_ref, v_ref, *, sub):
    rope = (cos_ref, sin_lo_ref, sin_hi_ref)
    for lo in range(0, x_ref.shape[0], sub):
        rows = pl.ds(lo, sub)
        u = _modulated(x_ref[rows, :], sh_ref, sc_ref)
        pc = jnp.dot(u, w_ref[:, 0:COL_CONV], preferred_element_type=F32)
        gb_ref[rows, :] = pc[:, 0:D_CONV]
        g_ref[rows, :] = pc[:, D_CONV:2 * D_CONV] * pc[:, 2 * D_CONV:3 * D_CONV]
        pq = jnp.dot(u, w_ref[:, Q0:K0], preferred_element_type=F32)
        _rope_store(q_ref, rows, pq, *rope, Q_SCALE)
        pk = jnp.dot(u, w_ref[:, K0:V0], preferred_element_type=F32)
        _rope_store(k_ref.at[0], rows, pk, *rope, 1.0)
        v_ref[0, rows, :] = jnp.dot(u, w_ref[:, V0:V0 + D_ATTN], preferred_element_type=F32).astype(BF16)


def _inproj(x, mod, layer, row_of_step, w_bf, rope, tm, sub, seq):
    t, d = x.shape
    steps_per_seq = seq // tm
    tok = lambda i: (i, 0)
    kvi = lambda i: (i // steps_per_seq, i % steps_per_seq, 0)
    tab = pl.BlockSpec((tm, LANES), lambda i: (i % steps_per_seq, 0))
    out = lambda n: pl.BlockSpec((tm, n), tok)
    return pl.pallas_call(
        functools.partial(_inproj_kernel, sub=sub),
        grid=(t // tm,),
        in_specs=[pl.BlockSpec((tm, d), tok), mod.spec(layer, 0, row_of_step),
                  mod.spec(layer, 1, row_of_step), w_bf.spec(), tab, tab, tab],
        out_specs=[out(D_CONV), out(D_CONV), out(COL_QK),
                   pl.BlockSpec((1, tm, COL_QK), kvi), pl.BlockSpec((1, tm, D_ATTN), kvi)],
        out_shape=[jax.ShapeDtypeStruct((t, D_CONV), F32), jax.ShapeDtypeStruct((t, D_CONV), F32),
                   jax.ShapeDtypeStruct((t, COL_QK), BF16),
                   jax.ShapeDtypeStruct((t // seq, seq, COL_QK), BF16),
                   jax.ShapeDtypeStruct((t // seq, seq, D_ATTN), BF16)],
        compiler_params=_params(1),
        name="inproj",
    )(x, mod.rows, mod.rows, w_bf.array, *rope)


def _attn_kernel(lq_ref, q_ref, *refs, lam_init, n_src, sub):
    k_refs, v_refs = refs[:n_src], refs[n_src:2 * n_src]
    sg_ref, o_ref = refs[2 * n_src], refs[2 * n_src + 1]
    lq = lq_ref[...]
    lam = (jnp.exp(jnp.sum(lq[0:1] * lq[1:2], axis=1, keepdims=True))
           - jnp.exp(jnp.sum(lq[2:3] * lq[3:4], axis=1, keepdims=True)) + lam_init)
    nt = (((1,), (1,)), ((), ()))

    def scores(qc):
        return [lax.dot_general(qc, k[0], nt, preferred_element_type=F32) for k in k_refs]

    def softmax_parts(s):
        m = functools.reduce(jnp.maximum, [jnp.max(x, axis=1, keepdims=True) for x in s])
        e = [jnp.exp2(x - m) for x in s]
        return e, sum(jnp.sum(x, axis=1, keepdims=True) for x in e)

    def score_tile(r):
        q = q_ref[r * sub:(r + 1) * sub, :]
        lane = lax.broadcasted_iota(jnp.int32, q.shape, 1)
        zero = jnp.zeros_like(q)
        return (scores(jnp.where(lane < DIFF_QK_DIM, q, zero)),
                scores(jnp.where(lane >= DIFF_QK_DIM, q, zero)))

    def project(r, e1, l1, e2, l2):
        ratio = lam * l1 / l2
        o = sum(jnp.dot((a - ratio * b).astype(BF16), v[0], preferred_element_type=F32)
                for a, b, v in zip(e1, e2, v_refs)) / l1
        y = o * lax.rsqrt(jnp.mean(o * o, axis=1, keepdims=True) + SUBLN_EPS)
        o_ref[r * sub:(r + 1) * sub, :] = (y * sg_ref[...] * (1.0 - lam_init)).astype(o_ref.dtype)

    n_sub = q_ref.shape[0] // sub
    s_next = score_tile(0)
    parts = None
    for r in range(n_sub):
        s1, s2 = s_next
        cur = softmax_parts(s1) + softmax_parts(s2)
        if r + 1 < n_sub:
            s_next = score_tile(r + 1)
        if parts is not None:
            project(r - 1, *parts)
        parts = cur
    project(n_sub - 1, *parts)


def _attention(lq, q, ks, vs, sg, lam_init, tq, sub):
    t = q.shape[0]
    b = ks[0].shape[0]
    nq = t // b // tq
    qo = pl.BlockSpec((tq, DIFF_V_DIM), lambda bi, h, i: (bi * nq + i, h))
    src = lambda a: pl.BlockSpec((1, a.shape[1], DIFF_V_DIM), lambda bi, h, i: (bi, 0, h))
    return pl.pallas_call(
        functools.partial(_attn_kernel, lam_init=lam_init, n_src=len(ks), sub=sub),
        grid=(b, N_DIFF_HEADS, nq),
        in_specs=[pl.BlockSpec(lq.shape, lambda bi, h, i: (0, 0)), qo] + [src(a) for a in ks + vs]
                 + [pl.BlockSpec((1, DIFF_V_DIM), lambda bi, h, i: (0, 0))],
        out_specs=qo,
        out_shape=jax.ShapeDtypeStruct((t, D_ATTN), BF16),
        compiler_params=_params(3),
        name="diff_attention",
    )(lq, q, *ks, *vs, sg.reshape(1, DIFF_V_DIM))


STREAM_CHUNK = 256
SUM_FLOOR = 2.0 ** -80


def _attn_stream_kernel(lq_ref, q_ref, *refs, lam_init, n_src, n_cast, sub):
    k_refs, v_refs = refs[:n_src], refs[n_src:2 * n_src]
    sg_ref, o_ref = refs[2 * n_src], refs[2 * n_src + 1 + n_cast]
    cast_in = refs[2 * n_src + 1:2 * n_src + 1 + n_cast]
    cast_out = refs[2 * n_src + 2 + n_cast:2 * n_src + 2 + 2 * n_cast]
    vaug_refs = refs[2 * n_src + 2 + 2 * n_cast:2 * n_src + 2 + 2 * n_cast + n_src]
    kmax_ref = refs[-1]
    for src_ref, dst_ref in zip(cast_in, cast_out):
        dst_ref[...] = src_ref[...].astype(BF16)
    lq = lq_ref[...]
    lam = (jnp.exp(jnp.sum(lq[0:1] * lq[1:2], axis=1, keepdims=True))
           - jnp.exp(jnp.sum(lq[2:3] * lq[3:4], axis=1, keepdims=True)) + lam_init)
    nt = (((1,), (1,)), ((), ()))
    half = lax.broadcasted_iota(jnp.int32, (1, 2 * DIFF_QK_DIM), 1) < DIFF_QK_DIM

    @pl.when(pl.program_id(2) == 0)
    def _():
        best = jnp.zeros((1, 2 * DIFF_QK_DIM), F32)
        for k, v, va in zip(k_refs, v_refs, vaug_refs):
            va[:, 0:DIFF_V_DIM] = v[0]
            va[:, DIFF_V_DIM:2 * DIFF_V_DIM] = jnp.ones((va.shape[0], DIFF_V_DIM), BF16)
            k2 = jnp.square(k[0].astype(F32))
            n1 = jnp.sum(jnp.where(half, k2, 0.0), axis=1, keepdims=True)
            n2 = jnp.sum(jnp.where(half, 0.0, k2), axis=1, keepdims=True)
            both = jnp.where(half, jnp.max(n1, axis=0, keepdims=True), jnp.max(n2, axis=0, keepdims=True))
            best = jnp.maximum(best, both)
        kmax_ref[...] = jnp.broadcast_to(jnp.sqrt(best), kmax_ref.shape)

    chunks = []
    for j, k in enumerate(k_refs):
        size = min(k.shape[1], STREAM_CHUNK)
        chunks += [(j, c0, size) for c0 in range(0, k.shape[1], size)]

    def component(q, first):
        mask = half if first else jnp.logical_not(half)
        qc = jnp.where(mask, q, jnp.zeros_like(q))
        qf = qc.astype(F32)
        kmax = kmax_ref[0:1, 0:1] if first else kmax_ref[0:1, DIFF_QK_DIM:DIFF_QK_DIM + 1]
        bound = jnp.sqrt(jnp.sum(qf * qf, axis=1, keepdims=True)) * kmax
        score = lambda ch: lax.dot_general(qc, k_refs[ch[0]][0, ch[1]:ch[1] + ch[2], :], nt,
                                           preferred_element_type=F32)
        acc = None
        s_next = score(chunks[0])
        for idx, (j, c0, size) in enumerate(chunks):
            s = s_next
            if idx + 1 < len(chunks):
                s_next = score(chunks[idx + 1])
            e = jnp.exp2(s - bound).astype(BF16)
            part = jnp.dot(e, vaug_refs[j][c0:c0 + size, :], preferred_element_type=F32)
            acc = part if acc is None else acc + part
        return acc[:, 0:DIFF_V_DIM], acc[:, DIFF_V_DIM:2 * DIFF_V_DIM]

    def exact_component(q, first):
        mask = half if first else jnp.logical_not(half)
        qc = jnp.where(mask, q, jnp.zeros_like(q))
        s = [lax.dot_general(qc, k[0], nt, preferred_element_type=F32) for k in k_refs]
        m = functools.reduce(jnp.maximum, [jnp.max(x, axis=1, keepdims=True) for x in s])
        acc = sum(jnp.dot(jnp.exp2(x - m).astype(BF16), va[...], preferred_element_type=F32)
                  for x, va in zip(s, vaug_refs))
        return acc[:, 0:DIFF_V_DIM], acc[:, DIFF_V_DIM:2 * DIFF_V_DIM]

    def emit(r, parts):
        (o1, l1), (o2, l2) = parts
        o = o1 / l1 - lam * (o2 / l2)
        y = o * lax.rsqrt(jnp.mean(o * o, axis=1, keepdims=True) + SUBLN_EPS)
        o_ref[r * sub:(r + 1) * sub, :] = (y * sg_ref[...] * (1.0 - lam_init)).astype(o_ref.dtype)
        return jnp.minimum(jnp.min(l1), jnp.min(l2))

    n_sub = q_ref.shape[0] // sub
    tiles = [q_ref[r * sub:(r + 1) * sub, :] for r in range(n_sub)]
    smallest_sum = functools.reduce(
        jnp.minimum, [emit(r, (component(q, True), component(q, False))) for r, q in enumerate(tiles)])

    @pl.when(jnp.logical_not(smallest_sum >= SUM_FLOOR))
    def _():
        for r, q in enumerate(tiles):
            emit(r, (exact_component(q, True), exact_component(q, False)))


def _attention_stream(lq, q, ks, vs, sg, lam_init, tq, sub, casts):
    t = q.shape[0]
    b = ks[0].shape[0]
    nq = t // b // tq
    steps = b * N_DIFF_HEADS * nq
    qo = pl.BlockSpec((tq, DIFF_V_DIM), lambda bi, h, i: (bi * nq + i, h))
    src = lambda a: pl.BlockSpec((1, a.shape[1], DIFF_V_DIM), lambda bi, h, i: (bi, 0, h))
    cast_in, cast_out, cast_shapes = [], [], []
    for w, first_layer, count in casts:
        _, rows, cols = w.shape
        slab = count * rows // steps
        assert slab * steps == count * rows and slab % (2 * SUBLANES) == 0
        per_layer = rows // slab

        def index(bi, h, i, per_layer=per_layer, base=0):
            flat = (bi * N_DIFF_HEADS + h) * nq + i
            return base + flat // per_layer, flat % per_layer, 0

        cast_in.append(pl.BlockSpec((1, slab, cols), functools.partial(index, base=first_layer)))
        cast_out.append(pl.BlockSpec((1, slab, cols), index))
        cast_shapes.append(jax.ShapeDtypeStruct((count, rows, cols), BF16))
    return pl.pallas_call(
        functools.partial(_attn_stream_kernel, lam_init=lam_init, n_src=len(ks), n_cast=len(casts), sub=sub),
        grid=(b, N_DIFF_HEADS, nq),
        in_specs=[pl.BlockSpec(lq.shape, lambda bi, h, i: (0, 0)), qo] + [src(a) for a in ks + vs]
                 + [pl.BlockSpec((1, DIFF_V_DIM), lambda bi, h, i: (0, 0))] + cast_in,
        out_specs=[qo] + cast_out,
        out_shape=[jax.ShapeDtypeStruct((t, D_ATTN), BF16)] + cast_shapes,
        scratch_shapes=[pltpu.VMEM((v.shape[1], 2 * DIFF_V_DIM), BF16) for v in vs]
                       + [pltpu.VMEM((SUBLANES, 2 * DIFF_QK_DIM), F32)],
        compiler_params=_params(3),
        name="diff_attention_stream",
    )(lq, q, *ks, *vs, sg.reshape(1, DIFF_V_DIM), *[c[0] for c in casts])


MLP_CHUNK = 512


def _mlp_residual(h, mlp_refs, alpha):
    sh_ref, sc_ref, w1_ref, w2_ref, gate_ref, lg_ref, lb_ref = mlp_refs
    u = _modulated(h, sh_ref, sc_ref)
    y = jnp.zeros(h.shape, F32)
    for c in range(w1_ref.shape[1] // MLP_CHUNK):
        hid = jnp.dot(u, w1_ref[:, c * MLP_CHUNK:(c + 1) * MLP_CHUNK], preferred_element_type=F32)
        hid = jnp.square(jnp.maximum(hid, 0.0)).astype(BF16)
        y = y + jnp.dot(hid, w2_ref[c * MLP_CHUNK:(c + 1) * MLP_CHUNK, :], preferred_element_type=F32)
    return _post_norm(h, y, gate_ref, lg_ref, lb_ref, alpha)


class _MlpTail:
    def __init__(self, w1, w2, mod, layer, row_of_step, ln_g, ln_b):
        d = mod.d
        vec = pl.BlockSpec((1, d), lambda *_: (0, 0))
        self.specs = [mod.spec(layer, 3, row_of_step), mod.spec(layer, 4, row_of_step), w1.spec(),
                      w2.spec(), mod.spec(layer, 5, row_of_step), vec, vec]
        self.args = [mod.rows, mod.rows, w1.array, w2.array, mod.rows, ln_g.reshape(1, d),
                     ln_b.reshape(1, d)]


def _mix_out_kernel(h_ref, gb_ref, g_ref, gprev_ref, gnext_ref, a_ref, w_ref, cw_ref,
                    gate_ref, lg_ref, lb_ref, *rest, steps_per_seq, sub, alpha):
    mlp_refs, o_ref = rest[:-1], rest[-1]
    i = pl.program_id(0)
    g = g_ref[...]
    tm = g.shape[0]
    row = lax.broadcasted_iota(jnp.int32, g.shape, 0)
    pos = i % steps_per_seq
    keep_prev = jnp.where(pos == 0, 0.0, 1.0)
    keep_next = jnp.where(pos == steps_per_seq - 1, 0.0, 1.0)
    before = jnp.where(row == 0, gprev_ref[SUBLANES - 1:SUBLANES, :] * keep_prev, pltpu.roll(g, 1, 0))
    after = jnp.where(row == tm - 1, gnext_ref[0:1, :] * keep_next, pltpu.roll(g, tm - 1, 0))
    cw = cw_ref[...]
    conv = before * cw[0:1] + g * cw[1:2] + after * cw[2:3]
    yc = (gb_ref[...] * conv).astype(BF16)
    subs = range(0, tm, sub)
    ys = [jnp.dot(yc[lo:lo + sub], w_ref[0:D_CONV, :], preferred_element_type=F32)
          + jnp.dot(a_ref[lo:lo + sub, :], w_ref[D_CONV:D_CONV + D_ATTN, :], preferred_element_type=F32)
          for lo in subs]
    h1s = [_post_norm(h_ref[lo:lo + sub, :], y, gate_ref, lg_ref, lb_ref, alpha) for lo, y in zip(subs, ys)]
    for lo, h1 in zip(subs, h1s):
        o_ref[lo:lo + sub, :] = _mlp_residual(h1, mlp_refs, alpha)


def _mix_out(h, gb, g, attn, w_bf, conv_w, mod, layer, row_of_step, ln_g, ln_b, mlp, tm, sub, seq, alpha):
    t, d = h.shape
    steps_per_seq = seq // tm
    blocks8 = tm // SUBLANES
    last8 = t // SUBLANES - 1
    tok = lambda i: (i, 0)
    vec = pl.BlockSpec((1, d), lambda i: (0, 0))
    return pl.pallas_call(
        functools.partial(_mix_out_kernel, steps_per_seq=steps_per_seq, sub=sub, alpha=alpha),
        grid=(t // tm,),
        in_specs=[pl.BlockSpec((tm, d), tok), pl.BlockSpec((tm, D_CONV), tok),
                  pl.BlockSpec((tm, D_CONV), tok),
                  pl.BlockSpec((SUBLANES, D_CONV), lambda i: (jnp.maximum(i * blocks8 - 1, 0), 0)),
                  pl.BlockSpec((SUBLANES, D_CONV), lambda i: (jnp.minimum((i + 1) * blocks8, last8), 0)),
                  pl.BlockSpec((tm, D_ATTN), tok), w_bf.spec(),
                  pl.BlockSpec(conv_w.shape, lambda i: (0, 0)),
                  mod.spec(layer, 2, row_of_step), vec, vec] + mlp.specs,
        out_specs=pl.BlockSpec((tm, d), tok),
        out_shape=jax.ShapeDtypeStruct((t, d), F32),
        compiler_params=_params(1),
        name="mix_out_mlp",
    )(h, gb, g, g, g, attn, w_bf.array, conv_w, mod.rows, ln_g.reshape(1, d), ln_b.reshape(1, d),
      *mlp.args)


def _mlp_kernel(h_ref, *rest, alpha):
    rest[-1][...] = _mlp_residual(h_ref[...], rest[:-1], alpha)


def _mlp(h, mlp, tm, alpha):
    t, d = h.shape
    tok = lambda i: (i, 0)
    return pl.pallas_call(
        functools.partial(_mlp_kernel, alpha=alpha),
        grid=(t // tm,),
        in_specs=[pl.BlockSpec((tm, d), tok)] + mlp.specs,
        out_specs=pl.BlockSpec((tm, d), tok),
        out_shape=jax.ShapeDtypeStruct((t, d), F32),
        compiler_params=_params(1),
        name="mlp",
    )(h, *mlp.args)


def _fourier_in_kernel(x_ref, sh_ref, sc_ref, gcs_ref, z_ref):
    u = _modulated(x_ref[...], sh_ref, sc_ref)
    gcs = gcs_ref[...]
    for g in range(u.shape[1] // FOURIER_GROUP):
        lo, hi = g * FOURIER_GROUP, (g + 1) * FOURIER_GROUP
        zz = jnp.dot(u[:, lo:hi], gcs, preferred_element_type=F32)
        z_ref[0, 0, :, lo:hi] = zz[:, 0:FOURIER_GROUP].astype(BF16)
        z_ref[0, 1, :, lo:hi] = zz[:, FOURIER_GROUP:2 * FOURIER_GROUP].astype(BF16)


def _fourier_in(x, mod, layer, row_of_step, gcs, tm, seq):
    t, d = x.shape
    steps_per_seq = seq // tm
    return pl.pallas_call(
        _fourier_in_kernel,
        grid=(t // tm,),
        in_specs=[pl.BlockSpec((tm, d), lambda i: (i, 0)), mod.spec(layer, 0, row_of_step),
                  mod.spec(layer, 1, row_of_step), _const_spec(gcs.shape)],
        out_specs=pl.BlockSpec((1, 2, tm, d), lambda i: (i // steps_per_seq, 0, i % steps_per_seq, 0)),
        out_shape=jax.ShapeDtypeStruct((t // seq, 2, seq, d), BF16),
        compiler_params=_params(1),
        name="fourier_in",
    )(x, mod.rows, mod.rows, gcs)


FOURIER_SLAB = 16
FOURIER_OUT_SLAB = 8


def _fourier_rows_kernel(x_ref, sh_ref, sc_ref, k1_ref, gm_ref, z_ref):
    w, nb, d = x_ref.shape[1], x_ref.shape[2], x_ref.shape[3]
    rows = w * nb
    u = _modulated(x_ref[0].reshape(rows, d), sh_ref, sc_ref)
    y = jnp.dot(k1_ref[0], u, preferred_element_type=F32).astype(BF16)
    gm = gm_ref[...]
    for g in range(d // FOURIER_GROUP):
        lo, hi = g * FOURIER_GROUP, (g + 1) * FOURIER_GROUP
        yy = jnp.concatenate([y[0:rows, lo:hi], y[rows:2 * rows, lo:hi]], axis=1)
        zz = jnp.dot(yy, gm, preferred_element_type=F32).astype(BF16)
        z_ref[0, 0, :, :, lo:hi] = zz[:, 0:FOURIER_GROUP].reshape(w, nb, FOURIER_GROUP)
        z_ref[0, 1, :, :, lo:hi] = zz[:, FOURIER_GROUP:2 * FOURIER_GROUP].reshape(w, nb, FOURIER_GROUP)


def _fourier_rows(h, mod, layer, k1, gm, batch):
    t, d = h.shape
    w, nb = GRID_W, FOURIER_SLAB
    h4 = h.reshape(batch, w, w, d)
    return pl.pallas_call(
        _fourier_rows_kernel,
        grid=(w // nb, batch),
        in_specs=[pl.BlockSpec((1, w, nb, d), lambda j, bi: (bi, 0, j, 0)),
                  mod.spec(layer, 0, lambda j, bi: bi), mod.spec(layer, 1, lambda j, bi: bi),
                  pl.BlockSpec((1,) + k1.shape[1:], lambda j, bi: (j, 0, 0)), _const_spec(gm.shape)],
        out_specs=pl.BlockSpec((1, 2, w, nb, d), lambda j, bi: (bi, 0, 0, j, 0)),
        out_shape=jax.ShapeDtypeStruct((batch, 2, w, w, d), BF16),
        compiler_params=_params(2),
        name="fourier_rows",
    )(h4, mod.rows, mod.rows, k1, gm)


def _fourier_cols_kernel(z_ref, h_ref, k2_ref, w_ref, gate_ref, lg_ref, lb_ref, *rest, alpha):
    mlp_refs, o_ref = rest[:-1], rest[-1]
    w, d = z_ref.shape[3], z_ref.shape[4]
    nd = FOURIER_OUT_SLAB
    slabs = range(0, z_ref.shape[2], nd)
    ys = []
    for s in slabs:
        zz = z_ref[0, :, s:s + nd].reshape(2 * nd * w, d)
        f = jnp.dot(k2_ref[...], zz, preferred_element_type=F32).astype(BF16)
        ys.append(jnp.dot(f, w_ref[...], preferred_element_type=F32))
    h1s = [_post_norm(h_ref[0, :, s:s + nd, :].reshape(w * nd, d), y, gate_ref, lg_ref, lb_ref, alpha)
           for s, y in zip(slabs, ys)]
    for s, h1 in zip(slabs, h1s):
        o_ref[0, :, s:s + nd, :] = _mlp_residual(h1, mlp_refs, alpha).reshape(w, nd, d)


def _fourier_cols(z, h, k2, w_bf, mod, layer, ln_g, ln_b, mlp, alpha):
    batch, _, w, _, d = z.shape
    nd = 2 * FOURIER_OUT_SLAB
    h4 = h.reshape(batch, w, w, d)
    vec = pl.BlockSpec((1, d), lambda bi, j: (0, 0))
    hblk = pl.BlockSpec((1, w, nd, d), lambda bi, j: (bi, 0, j, 0))
    out = pl.pallas_call(
        functools.partial(_fourier_cols_kernel, alpha=alpha),
        grid=(batch, w // nd),
        in_specs=[pl.BlockSpec((1, 2, nd, w, d), lambda bi, j: (bi, 0, j, 0, 0)), hblk,
                  _const_spec(k2.shape), w_bf.spec(),
                  mod.spec(layer, 2, lambda bi, j: bi), vec, vec] + mlp.specs,
        out_specs=hblk,
        out_shape=jax.ShapeDtypeStruct((batch, w, w, d), F32),
        compiler_params=_params(2),
        name="fourier_cols_mlp",
    )(z, h4, k2, w_bf.array, mod.rows, ln_g.reshape(1, d), ln_b.reshape(1, d), *mlp.args)
    return out.reshape(batch * w * w, d)


def _fourier_dense_kernel(z_ref, cs_ref, o_ref, *, norm):
    zc = jnp.concatenate([z_ref[0, 0], z_ref[0, 1]], axis=0)
    o_ref[0] = (jnp.dot(cs_ref[...], zc, preferred_element_type=F32) * norm).astype(BF16)


def _fourier_dense(z, cs, norm):
    b, _, seq, d = z.shape
    return pl.pallas_call(
        functools.partial(_fourier_dense_kernel, norm=norm),
        grid=(b,),
        in_specs=[pl.BlockSpec((1, 2, seq, d), lambda bi: (bi, 0, 0, 0)), _const_spec(cs.shape)],
        out_specs=pl.BlockSpec((1, seq, d), lambda bi: (bi, 0, 0)),
        out_shape=jax.ShapeDtypeStruct((b, seq, d), BF16),
        compiler_params=_params(1),
        name="fourier_dense",
    )(z, cs)


def _proj_out_kernel(h_ref, a_ref, w_ref, gate_ref, lg_ref, lb_ref, o_ref, *, alpha):
    y = jnp.dot(a_ref[...], w_ref[...], preferred_element_type=F32)
    o_ref[...] = _post_norm(h_ref[...], y, gate_ref, lg_ref, lb_ref, alpha)


def _proj_out(h, a, w_bf, mod, layer, row_of_step, ln_g, ln_b, tm, alpha):
    t, d = h.shape
    tok = lambda i: (i, 0)
    vec = pl.BlockSpec((1, d), lambda i: (0, 0))
    return pl.pallas_call(
        functools.partial(_proj_out_kernel, alpha=alpha),
        grid=(t // tm,),
        in_specs=[pl.BlockSpec((tm, d), tok), pl.BlockSpec((tm, a.shape[1]), tok),
                  w_bf.spec(), mod.spec(layer, 2, row_of_step), vec, vec],
        out_specs=pl.BlockSpec((tm, d), tok),
        out_shape=jax.ShapeDtypeStruct((t, d), F32),
        compiler_params=_params(1),
        name="proj_out",
    )(h, a, w_bf.array, mod.rows, ln_g.reshape(1, d), ln_b.reshape(1, d))


def _dft_cos_sin(n, denom):
    idx = np.arange(n, dtype=np.int64)
    ang = ((idx[:, None] * idx[None, :]) % denom) * (2.0 * math.pi / denom)
    return np.cos(ang), np.sin(ang)


def _bf16_const(a):
    return jnp.asarray(np.asarray(a, np.float32)).astype(BF16)


def _rope_tables(seq):
    half = DIFF_QK_DIM // 4
    t = np.arange(seq)
    inv = ROPE_BASE ** (-np.arange(half, dtype=np.float64) / half)
    lane = np.arange(LANES)
    use_col = (lane % DIFF_QK_DIM) >= DIFF_QK_DIM // 2
    second = (lane % (2 * half)) >= half
    pos = np.where(use_col[None, :], (t % GRID_W)[:, None], (t // GRID_W)[:, None])
    ang = pos * inv[lane % half][None, :]
    cos, sin = np.cos(ang), np.sin(ang)
    tabs = (cos, np.where(second[None, :], 0.0, -sin), np.where(second[None, :], sin, 0.0))
    return tuple(jnp.asarray(a, F32) for a in tabs)


def _fourier_tables(seq, norm):
    w, nb, nd = GRID_W, FOURIER_SLAB, FOURIER_OUT_SLAB
    idx = np.arange(w, dtype=np.int64)
    slab = np.arange(w // nb, dtype=np.int64)
    col = np.arange(nb, dtype=np.int64)
    t = idx[None, None, None, :] * w + slab[:, None, None, None] * nb + col[None, None, :, None]
    ang = ((idx[None, :, None, None] * t) % seq) * (2.0 * math.pi / seq)
    trig = np.stack([np.cos(ang), -np.sin(ang)], axis=1)
    k1 = trig[..., None] * np.eye(nb)[None, None, None, :, None, :]
    c64, s64 = _dft_cos_sin(w, w)
    cs = np.stack([c64, s64], axis=1) * norm
    k2 = cs[:, None, :, None, :] * np.eye(nd)[None, :, None, :, None]
    return (_bf16_const(k1.reshape(w // nb, 2 * w * nb, w * nb)),
            _bf16_const(k2.reshape(w * nd, 2 * nd * w)))


def _no_rope_tables(seq):
    return jnp.ones((seq, LANES), F32), jnp.zeros((seq, LANES), F32), jnp.zeros((seq, LANES), F32)


def kernel(x, c, ctx, c_ctx, ada_w, ada_b, ln_g, ln_b, mlp_w1, mlp_w2, w_in, conv_w, lambda_qk,
           subln_g, w_out_mix, w_out_fourier):
    batch, seq, d = x.shape
    ctx_len = ctx.shape[1]
    depth = ada_w.shape[0]
    alpha = (2.0 * depth) ** 0.25
    last_attn = 2 * ((depth - 1) // 2)
    tm = 512
    assert seq == GRID_W * GRID_W and seq % tm == 0 and batch < SUBLANES and ctx_len % SUBLANES == 0
    assert depth % 2 == 0

    cond8 = jnp.zeros((SUBLANES, d), F32).at[:batch].set(c).at[batch].set(c_ctx)
    mod = _Mod(_modulation(cond8, ada_w, ada_b), d)
    lat_row = lambda i: i // (seq // tm)
    ctx_row = lambda i: batch

    w_in_first = w_in[0:1].astype(BF16)

    rope = _rope_tables(seq)
    no_rope = _no_rope_tables(ctx_len)

    gc, gs = _dft_cos_sin(FOURIER_GROUP, FOURIER_GROUP)
    gcs = _bf16_const(np.concatenate([gc, -gs], axis=1))
    gmat = _bf16_const(np.block([[gc, -gs], [gs, gc]]))
    k1, k2 = _fourier_tables(seq, (seq * FOURIER_GROUP) ** -0.5)
    cc, sc_ = _dft_cos_sin(ctx_len, ctx_len)
    cs_ctx = _bf16_const(np.concatenate([cc, sc_], axis=1))

    h = x.reshape(batch * seq, d)
    hc = ctx.reshape(batch * ctx_len, d)

    for i in range(depth):
        j = i // 2
        update_ctx = i < last_attn
        if i % 2 == 0:
            lam_init = 0.8 - 0.6 * math.exp(-0.3 * i)
            w_in_j = _Stacked(w_in_first, 0) if j == 0 else _Stacked(w_in_next, 0)
            mix_row = lambda t: t // (seq // (2 * tm))
            gb, g, q, k, v = _inproj(h, mod, i, mix_row, w_in_j, rope, 2 * tm, tm, seq)
            gbc, gcx, qc, kc, vc = _inproj(hc, mod, i, ctx_row, w_in_j, no_rope, ctx_len, ctx_len, ctx_len)
            casts = [(mlp_w1, i, 2), (mlp_w2, i, 2), (w_out_mix, j, 1), (w_out_fourier, j, 1)]
            if j + 1 < w_in.shape[0]:
                casts.append((w_in, j + 1, 1))
            attn, w1_pair, w2_pair, w_mix_one, w_four_one, *rest = _attention_stream(
                lambda_qk[j], q, [k, kc], [v, vc], subln_g[j], lam_init, 1024, 256, casts)
            w_in_next = rest[0] if rest else None
            w_mix_j = _Stacked(w_mix_one, 0)
            w1, w2 = _Stacked(w1_pair, 0), _Stacked(w2_pair, 0)
            mlp_ctx = _MlpTail(w1, w2, mod, i, ctx_row, ln_g[i, 1], ln_b[i, 1])
            mlp_lat = _MlpTail(w1, w2, mod, i, mix_row, ln_g[i, 1], ln_b[i, 1])
            h = _mix_out(h, gb, g, attn, w_mix_j, conv_w[j], mod, i, mix_row,
                         ln_g[i, 0], ln_b[i, 0], mlp_lat, 2 * tm, tm, seq, alpha)
            if update_ctx:
                attn_c = _attention(lambda_qk[j], qc, [kc], [vc], subln_g[j], lam_init, ctx_len, ctx_len)
                hc = _mix_out(hc, gbc, gcx, attn_c, w_mix_j, conv_w[j], mod, i, ctx_row,
                              ln_g[i, 0], ln_b[i, 0], mlp_ctx, ctx_len, ctx_len, ctx_len, alpha)
        else:
            w_four_j = _Stacked(w_four_one, 0)
            w1, w2 = _Stacked(w1_pair, 1), _Stacked(w2_pair, 1)
            mlp_ctx = _MlpTail(w1, w2, mod, i, ctx_row, ln_g[i, 1], ln_b[i, 1])
            z = _fourier_rows(h, mod, i, k1, gmat, batch)
            mlp_slab = _MlpTail(w1, w2, mod, i, lambda bi, sl: bi, ln_g[i, 1], ln_b[i, 1])
            h = _fourier_cols(z, h, k2, w_four_j, mod, i, ln_g[i, 0], ln_b[i, 0], mlp_slab, alpha)
            if update_ctx:
                zc = _fourier_in(hc, mod, i, ctx_row, gcs, ctx_len, ctx_len)
                fc = _fourier_dense(zc, cs_ctx, (ctx_len * FOURIER_GROUP) ** -0.5)
                hc1 = _proj_out(hc, fc.reshape(batch * ctx_len, d), w_four_j, mod, i, ctx_row,
                                ln_g[i, 0], ln_b[i, 0], ctx_len, alpha)
                hc = _mlp(hc1, mlp_ctx, ctx_len, alpha)
    return h.reshape(batch, seq, d)
```

```python
import functools
import math

import jax
import jax.numpy as jnp
import numpy as np
from jax import lax
from jax.experimental import pallas as pl
from jax.experimental.pallas import tpu as pltpu

F32 = jnp.float32
BF16 = jnp.bfloat16

GRID_W = 64
CONV_WIDTH = 3
D_CONV = 512
N_DIFF_HEADS = 4
DIFF_QK_DIM = 64
DIFF_V_DIM = 128
D_ATTN = N_DIFF_HEADS * DIFF_V_DIM
FOURIER_GROUP = 128
ROPE_BASE = 10000.0
LN_EPS = 1e-6
SUBLN_EPS = 1e-5
COL_CONV = 3 * D_CONV
COL_QK = N_DIFF_HEADS * 2 * DIFF_QK_DIM
Q0 = COL_CONV
K0 = Q0 + COL_QK
V0 = K0 + COL_QK
Q_SCALE = math.log2(math.e) * DIFF_QK_DIM ** -0.5

LANES = 128
SUBLANES = 8
VMEM_LIMIT = 56 * 1024 * 1024


def _params(n_axes):
    return pltpu.CompilerParams(
        dimension_semantics=("arbitrary",) * n_axes, vmem_limit_bytes=VMEM_LIMIT)


def _const_spec(shape):
    nd = len(shape)
    return pl.BlockSpec(shape, lambda *_: (0,) * nd, pipeline_mode=pl.Buffered(1))


class _Stacked:
    def __init__(self, array, index):
        self.array, self.index = array, index

    def spec(self):
        k, n = self.array.shape[1:]
        index = self.index
        return pl.BlockSpec((None, k, n), lambda *_: (index, 0, 0), pipeline_mode=pl.Buffered(1))


def _layer_norm(x):
    mu = jnp.mean(x, axis=-1, keepdims=True)
    xc = x - mu
    var = jnp.mean(xc * xc, axis=-1, keepdims=True)
    return xc * lax.rsqrt(var + LN_EPS)


def _modulated(x, shift_ref, scale_ref):
    return (_layer_norm(x) * (1.0 + scale_ref[0]) + shift_ref[0]).astype(BF16)


def _post_norm(h, y, gate_ref, g_ref, b_ref, alpha):
    return _layer_norm(alpha * h + gate_ref[0] * y) * g_ref[...] + b_ref[...]


def _mod_kernel(cond_ref, w_ref, b_ref, o_ref):
    c = cond_ref[...]
    a = (c * (1.0 / (1.0 + jnp.exp(-c)))).astype(BF16)
    o_ref[0] = jnp.dot(a, w_ref[0].astype(BF16), preferred_element_type=F32) + b_ref[0]


def _modulation(cond8, ada_w, ada_b):
    depth, d, n6 = ada_w.shape
    tn = 1536
    return pl.pallas_call(
        _mod_kernel,
        grid=(depth, n6 // tn),
        in_specs=[
            pl.BlockSpec((SUBLANES, d), lambda l, j: (0, 0)),
            pl.BlockSpec((1, d, tn), lambda l, j: (l, 0, j)),
            pl.BlockSpec((1, 1, tn), lambda l, j: (l, 0, j)),
        ],
        out_specs=pl.BlockSpec((1, SUBLANES, tn), lambda l, j: (l, 0, j)),
        out_shape=jax.ShapeDtypeStruct((depth, SUBLANES, n6), F32),
        compiler_params=_params(2),
        name="modulation",
    )(cond8, ada_w, ada_b.reshape(depth, 1, n6))


class _Mod:
    def __init__(self, table, d):
        depth = table.shape[0]
        self.rows = table.reshape(depth, SUBLANES, 6, d).transpose(0, 2, 1, 3).reshape(
            depth * 6 * SUBLANES, 1, d)
        self.d = d

    def spec(self, layer, which, row_of_step):
        base = (layer * 6 + which) * SUBLANES
        return pl.BlockSpec((1, 1, self.d), lambda *idx: (base + row_of_step(*idx), 0, 0))


def _rope_store(dst_ref, rows, p, cos_ref, sin_lo_ref, sin_hi_ref, scale, transposed=False):
    cos, sin_lo, sin_hi = cos_ref[rows, :], sin_lo_ref[rows, :], sin_hi_ref[rows, :]
    half = DIFF_QK_DIM // 4
    for c in range(p.shape[1] // LANES):
        x = p[:, c * LANES:(c + 1) * LANES]
        ahead = pltpu.roll(x, LANES - half, 1)
        behind = pltpu.roll(x, half, 1)
        r = x * cos + ahead * sin_lo + behind * sin_hi
        if scale != 1.0:
            r = r * scale
        if transposed:
            dst_ref[c * LANES:(c + 1) * LANES, rows] = r.T.astype(dst_ref.dtype)
        else:
            dst_ref[rows, c * LANES:(c + 1) * LANES] = r.astype(dst_ref.dtype)


def _inproj_kernel(x_ref, sh_ref, sc_ref, w_ref, cos_ref, sin_lo_ref, sin_hi_ref,
                   gb_ref, g_ref, q_ref, k_ref, v_ref, *, sub):
    rope = (cos_ref, sin_lo_ref, sin_hi_ref)
    for lo in range(0, x_ref.shape[0], sub):
        rows = pl.ds(lo, sub)
        u = _modulated(x_ref[rows, :], sh_ref, sc_ref)
        pc = jnp.dot(u, w_ref[:, 0:COL_CONV], preferred_element_type=F32)
        gb_ref[rows, :] = pc[:, 0:D_CONV]
        g_ref[rows, :] = pc[:, D_CONV:2 * D_CONV] * pc[:, 2 * D_CONV:3 * D_CONV]
        pq = jnp.dot(u, w_ref[:, Q0:K0], preferred_element_type=F32)
        _rope_store(q_ref, rows, pq, *rope, Q_SCALE)
        pk = jnp.dot(u, w_ref[:, K0:V0], preferred_element_type=F32)
        _rope_store(k_ref.at[0], rows, pk, *rope, 1.0, transposed=True)
        v_ref[0, rows, :] = jnp.dot(u, w_ref[:, V0:V0 + D_ATTN], preferred_element_type=F32).astype(BF16)


def _inproj(x, mod, layer, row_of_step, w_bf, rope, tm, sub, seq):
    t, d = x.shape
    steps_per_seq = seq // tm
    tok = lambda i: (i, 0)
    kvi = lambda i: (i // steps_per_seq, i % steps_per_seq, 0)
    tab = pl.BlockSpec((tm, LANES), lambda i: (i % steps_per_seq, 0))
    out = lambda n: pl.BlockSpec((tm, n), tok)
    return pl.pallas_call(
        functools.partial(_inproj_kernel, sub=sub),
        grid=(t // tm,),
        in_specs=[pl.BlockSpec((tm, d), tok), mod.spec(layer, 0, row_of_step),
                  mod.spec(layer, 1, row_of_step), w_bf.spec(), tab, tab, tab],
        out_specs=[out(D_CONV), out(D_CONV), out(COL_QK),
                   pl.BlockSpec((1, COL_QK, tm), lambda i: (i // steps_per_seq, 0, i % steps_per_seq)),
                   pl.BlockSpec((1, tm, D_ATTN), kvi)],
        out_shape=[jax.ShapeDtypeStruct((t, D_CONV), F32), jax.ShapeDtypeStruct((t, D_CONV), F32),
                   jax.ShapeDtypeStruct((t, COL_QK), BF16),
                   jax.ShapeDtypeStruct((t // seq, COL_QK, seq), BF16),
                   jax.ShapeDtypeStruct((t // seq, seq, D_ATTN), BF16)],
        compiler_params=_params(1),
        name="inproj",
    )(x, mod.rows, mod.rows, w_bf.array, *rope)


def _attn_kernel(lq_ref, q_ref, *refs, lam_init, n_src, sub):
    k_refs, v_refs = refs[:n_src], refs[n_src:2 * n_src]
    sg_ref, o_ref = refs[2 * n_src], refs[2 * n_src + 1]
    lq = lq_ref[...]
    lam = (jnp.exp(jnp.sum(lq[0:1] * lq[1:2], axis=1, keepdims=True))
           - jnp.exp(jnp.sum(lq[2:3] * lq[3:4], axis=1, keepdims=True)) + lam_init)
    def scores(qc):
        return [jnp.dot(qc, k[0], preferred_element_type=F32) for k in k_refs]

    def softmax_parts(s):
        m = functools.reduce(jnp.maximum, [jnp.max(x, axis=1, keepdims=True) for x in s])
        e = [jnp.exp2(x - m) for x in s]
        return e, sum(jnp.sum(x, axis=1, keepdims=True) for x in e)

    def score_tile(r):
        q = q_ref[r * sub:(r + 1) * sub, :]
        lane = lax.broadcasted_iota(jnp.int32, q.shape, 1)
        zero = jnp.zeros_like(q)
        return (scores(jnp.where(lane < DIFF_QK_DIM, q, zero)),
                scores(jnp.where(lane >= DIFF_QK_DIM, q, zero)))

    def project(r, e1, l1, e2, l2):
        ratio = lam * l1 / l2
        o = sum(jnp.dot((a - ratio * b).astype(BF16), v[0], preferred_element_type=F32)
                for a, b, v in zip(e1, e2, v_refs)) / l1
        y = o * lax.rsqrt(jnp.mean(o * o, axis=1, keepdims=True) + SUBLN_EPS)
        o_ref[r * sub:(r + 1) * sub, :] = (y * sg_ref[...] * (1.0 - lam_init)).astype(o_ref.dtype)

    n_sub = q_ref.shape[0] // sub
    s_next = score_tile(0)
    parts = None
    for r in range(n_sub):
        s1, s2 = s_next
        cur = softmax_parts(s1) + softmax_parts(s2)
        if r + 1 < n_sub:
            s_next = score_tile(r + 1)
        if parts is not None:
            project(r - 1, *parts)
        parts = cur
    project(n_sub - 1, *parts)


def _attention(lq, q, ks, vs, sg, lam_init, tq, sub):
    t = q.shape[0]
    b = ks[0].shape[0]
    nq = t // b // tq
    qo = pl.BlockSpec((tq, DIFF_V_DIM), lambda bi, h, i: (bi * nq + i, h))
    src = lambda a: pl.BlockSpec((1, a.shape[1], DIFF_V_DIM), lambda bi, h, i: (bi, 0, h))
    ksrc = lambda a: pl.BlockSpec((1, 2 * DIFF_QK_DIM, a.shape[2]), lambda bi, h, i: (bi, h, 0))
    return pl.pallas_call(
        functools.partial(_attn_kernel, lam_init=lam_init, n_src=len(ks), sub=sub),
        grid=(b, N_DIFF_HEADS, nq),
        in_specs=[pl.BlockSpec(lq.shape, lambda bi, h, i: (0, 0)), qo]
                 + [ksrc(a) for a in ks] + [src(a) for a in vs]
                 + [pl.BlockSpec((1, DIFF_V_DIM), lambda bi, h, i: (0, 0))],
        out_specs=qo,
        out_shape=jax.ShapeDtypeStruct((t, D_ATTN), BF16),
        compiler_params=_params(3),
        name="diff_attention",
    )(lq, q, *ks, *vs, sg.reshape(1, DIFF_V_DIM))


STREAM_CHUNK = 256
SUM_FLOOR = 2.0 ** -80


def _attn_stream_kernel(lq_ref, q_ref, *refs, lam_init, n_src, n_cast, sub):
    k_refs, v_refs = refs[:n_src], refs[n_src:2 * n_src]
    sg_ref, o_ref = refs[2 * n_src], refs[2 * n_src + 1 + n_cast]
    cast_in = refs[2 * n_src + 1:2 * n_src + 1 + n_cast]
    cast_out = refs[2 * n_src + 2 + n_cast:2 * n_src + 2 + 2 * n_cast]
    vaug_refs = refs[2 * n_src + 2 + 2 * n_cast:2 * n_src + 2 + 2 * n_cast + n_src]
    kmax_ref = refs[-1]
    for src_ref, dst_ref in zip(cast_in, cast_out):
        dst_ref[...] = src_ref[...].astype(BF16)
    lq = lq_ref[...]
    lam = (jnp.exp(jnp.sum(lq[0:1] * lq[1:2], axis=1, keepdims=True))
           - jnp.exp(jnp.sum(lq[2:3] * lq[3:4], axis=1, keepdims=True)) + lam_init)
    half = lax.broadcasted_iota(jnp.int32, (1, 2 * DIFF_QK_DIM), 1) < DIFF_QK_DIM

    @pl.when(pl.program_id(2) == 0)
    def _():
        best = jnp.zeros((1, 2 * DIFF_QK_DIM), F32)
        for k, v, va in zip(k_refs, v_refs, vaug_refs):
            va[:, 0:DIFF_V_DIM] = v[0]
            va[:, DIFF_V_DIM:2 * DIFF_V_DIM] = jnp.ones((va.shape[0], DIFF_V_DIM), BF16)
            k2 = jnp.square(k[0].astype(F32))
            n1 = jnp.sum(k2[0:DIFF_QK_DIM], axis=0, keepdims=True)
            n2 = jnp.sum(k2[DIFF_QK_DIM:2 * DIFF_QK_DIM], axis=0, keepdims=True)
            both = jnp.where(half, jnp.max(n1, axis=1, keepdims=True), jnp.max(n2, axis=1, keepdims=True))
            best = jnp.maximum(best, both)
        kmax_ref[...] = jnp.broadcast_to(jnp.sqrt(best), kmax_ref.shape)

    chunks = []
    for j, k in enumerate(k_refs):
        size = min(k.shape[2], STREAM_CHUNK)
        chunks += [(j, c0, size) for c0 in range(0, k.shape[2], size)]

    def component(q, first):
        mask = half if first else jnp.logical_not(half)
        qc = jnp.where(mask, q, jnp.zeros_like(q))
        qf = qc.astype(F32)
        kmax = kmax_ref[0:1, 0:1] if first else kmax_ref[0:1, DIFF_QK_DIM:DIFF_QK_DIM + 1]
        bound = jnp.sqrt(jnp.sum(qf * qf, axis=1, keepdims=True)) * kmax
        score = lambda ch: jnp.dot(qc, k_refs[ch[0]][0, :, ch[1]:ch[1] + ch[2]],
                                   preferred_element_type=F32)
        acc = None
        s_next = score(chunks[0])
        for idx, (j, c0, size) in enumerate(chunks):
            s = s_next
            if idx + 1 < len(chunks):
                s_next = score(chunks[idx + 1])
            e = jnp.exp2(s - bound).astype(BF16)
            part = jnp.dot(e, vaug_refs[j][c0:c0 + size, :], preferred_element_type=F32)
            acc = part if acc is None else acc + part
        return acc[:, 0:DIFF_V_DIM], acc[:, DIFF_V_DIM:2 * DIFF_V_DIM]

    def exact_component(q, first):
        mask = half if first else jnp.logical_not(half)
        qc = jnp.where(mask, q, jnp.zeros_like(q))
        s = [jnp.dot(qc, k[0], preferred_element_type=F32) for k in k_refs]
        m = functools.reduce(jnp.maximum, [jnp.max(x, axis=1, keepdims=True) for x in s])
        acc = sum(jnp.dot(jnp.exp2(x - m).astype(BF16), va[...], preferred_element_type=F32)
                  for x, va in zip(s, vaug_refs))
        return acc[:, 0:DIFF_V_DIM], acc[:, DIFF_V_DIM:2 * DIFF_V_DIM]

    def emit(r, parts):
        (o1, l1), (o2, l2) = parts
        o = o1 / l1 - lam * (o2 / l2)
        y = o * lax.rsqrt(jnp.mean(o * o, axis=1, keepdims=True) + SUBLN_EPS)
        o_ref[r * sub:(r + 1) * sub, :] = (y * sg_ref[...] * (1.0 - lam_init)).astype(o_ref.dtype)
        return jnp.minimum(jnp.min(l1), jnp.min(l2))

    n_sub = q_ref.shape[0] // sub
    tiles = [q_ref[r * sub:(r + 1) * sub, :] for r in range(n_sub)]
    smallest_sum = functools.reduce(
        jnp.minimum, [emit(r, (component(q, True), component(q, False))) for r, q in enumerate(tiles)])

    @pl.when(jnp.logical_not(smallest_sum >= SUM_FLOOR))
    def _():
        for r, q in enumerate(tiles):
            emit(r, (exact_component(q, True), exact_component(q, False)))


def _attention_stream(lq, q, ks, vs, sg, lam_init, tq, sub, casts):
    t = q.shape[0]
    b = ks[0].shape[0]
    nq = t // b // tq
    steps = b * N_DIFF_HEADS * nq
    qo = pl.BlockSpec((tq, DIFF_V_DIM), lambda bi, h, i: (bi * nq + i, h))
    src = lambda a: pl.BlockSpec((1, a.shape[1], DIFF_V_DIM), lambda bi, h, i: (bi, 0, h))
    ksrc = lambda a: pl.BlockSpec((1, 2 * DIFF_QK_DIM, a.shape[2]), lambda bi, h, i: (bi, h, 0))
    cast_in, cast_out, cast_shapes = [], [], []
    for w, first_layer, count in casts:
        _, rows, cols = w.shape
        slab = count * rows // steps
        assert slab * steps == count * rows and slab % (2 * SUBLANES) == 0
        per_layer = rows // slab

        def index(bi, h, i, per_layer=per_layer, base=0):
            flat = (bi * N_DIFF_HEADS + h) * nq + i
            return base + flat // per_layer, flat % per_layer, 0

        cast_in.append(pl.BlockSpec((1, slab, cols), functools.partial(index, base=first_layer)))
        cast_out.append(pl.BlockSpec((1, slab, cols), index))
        cast_shapes.append(jax.ShapeDtypeStruct((count, rows, cols), BF16))
    return pl.pallas_call(
        functools.partial(_attn_stream_kernel, lam_init=lam_init, n_src=len(ks), n_cast=len(casts), sub=sub),
        grid=(b, N_DIFF_HEADS, nq),
        in_specs=[pl.BlockSpec(lq.shape, lambda bi, h, i: (0, 0)), qo]
                 + [ksrc(a) for a in ks] + [src(a) for a in vs]
                 + [pl.BlockSpec((1, DIFF_V_DIM), lambda bi, h, i: (0, 0))] + cast_in,
        out_specs=[qo] + cast_out,
        out_shape=[jax.ShapeDtypeStruct((t, D_ATTN), BF16)] + cast_shapes,
        scratch_shapes=[pltpu.VMEM((v.shape[1], 2 * DIFF_V_DIM), BF16) for v in vs]
                       + [pltpu.VMEM((SUBLANES, 2 * DIFF_QK_DIM), F32)],
        compiler_params=_params(3),
        name="diff_attention_stream",
    )(lq, q, *ks, *vs, sg.reshape(1, DIFF_V_DIM), *[c[0] for c in casts])


MLP_CHUNK = 512


def _mlp_residual(h, mlp_refs, alpha):
    sh_ref, sc_ref, w1_ref, w2_ref, gate_ref, lg_ref, lb_ref = mlp_refs
    u = _modulated(h, sh_ref, sc_ref)
    y = jnp.zeros(h.shape, F32)
    for c in range(w1_ref.shape[1] // MLP_CHUNK):
        hid = jnp.dot(u, w1_ref[:, c * MLP_CHUNK:(c + 1) * MLP_CHUNK], preferred_element_type=F32)
        hid = jnp.square(jnp.maximum(hid, 0.0)).astype(BF16)
        y = y + jnp.dot(hid, w2_ref[c * MLP_CHUNK:(c + 1) * MLP_CHUNK, :], preferred_element_type=F32)
    return _post_norm(h, y, gate_ref, lg_ref, lb_ref, alpha)


class _MlpTail:
    def __init__(self, w1, w2, mod, layer, row_of_step, ln_g, ln_b):
        d = mod.d
        vec = pl.BlockSpec((1, d), lambda *_: (0, 0))
        self.specs = [mod.spec(layer, 3, row_of_step), mod.spec(layer, 4, row_of_step), w1.spec(),
                      w2.spec(), mod.spec(layer, 5, row_of_step), vec, vec]
        self.args = [mod.rows, mod.rows, w1.array, w2.array, mod.rows, ln_g.reshape(1, d),
                     ln_b.reshape(1, d)]


def _mix_out_kernel(h_ref, gb_ref, g_ref, gprev_ref, gnext_ref, a_ref, w_ref, cw_ref,
                    gate_ref, lg_ref, lb_ref, *rest, steps_per_seq, sub, alpha):
    mlp_refs, o_ref = rest[:-1], rest[-1]
    i = pl.program_id(0)
    g = g_ref[...]
    tm = g.shape[0]
    row = lax.broadcasted_iota(jnp.int32, g.shape, 0)
    pos = i % steps_per_seq
    keep_prev = jnp.where(pos == 0, 0.0, 1.0)
    keep_next = jnp.where(pos == steps_per_seq - 1, 0.0, 1.0)
    before = jnp.where(row == 0, gprev_ref[SUBLANES - 1:SUBLANES, :] * keep_prev, pltpu.roll(g, 1, 0))
    after = jnp.where(row == tm - 1, gnext_ref[0:1, :] * keep_next, pltpu.roll(g, tm - 1, 0))
    cw = cw_ref[...]
    conv = before * cw[0:1] + g * cw[1:2] + after * cw[2:3]
    yc = (gb_ref[...] * conv).astype(BF16)
    subs = range(0, tm, sub)
    ys = [jnp.dot(yc[lo:lo + sub], w_ref[0:D_CONV, :], preferred_element_type=F32)
          + jnp.dot(a_ref[lo:lo + sub, :], w_ref[D_CONV:D_CONV + D_ATTN, :], preferred_element_type=F32)
          for lo in subs]
    h1s = [_post_norm(h_ref[lo:lo + sub, :], y, gate_ref, lg_ref, lb_ref, alpha) for lo, y in zip(subs, ys)]
    for lo, h1 in zip(subs, h1s):
        o_ref[lo:lo + sub, :] = _mlp_residual(h1, mlp_refs, alpha)


def _mix_out(h, gb, g, attn, w_bf, conv_w, mod, layer, row_of_step, ln_g, ln_b, mlp, tm, sub, seq, alpha):
    t, d = h.shape
    steps_per_seq = seq // tm
    blocks8 = tm // SUBLANES
    last8 = t // SUBLANES - 1
    tok = lambda i: (i, 0)
    vec = pl.BlockSpec((1, d), lambda i: (0, 0))
    return pl.pallas_call(
        functools.partial(_mix_out_kernel, steps_per_seq=steps_per_seq, sub=sub, alpha=alpha),
        grid=(t // tm,),
        in_specs=[pl.BlockSpec((tm, d), tok), pl.BlockSpec((tm, D_CONV), tok),
                  pl.BlockSpec((tm, D_CONV), tok),
                  pl.BlockSpec((SUBLANES, D_CONV), lambda i: (jnp.maximum(i * blocks8 - 1, 0), 0)),
                  pl.BlockSpec((SUBLANES, D_CONV), lambda i: (jnp.minimum((i + 1) * blocks8, last8), 0)),
                  pl.BlockSpec((tm, D_ATTN), tok), w_bf.spec(),
                  pl.BlockSpec(conv_w.shape, lambda i: (0, 0)),
                  mod.spec(layer, 2, row_of_step), vec, vec] + mlp.specs,
        out_specs=pl.BlockSpec((tm, d), tok),
        out_shape=jax.ShapeDtypeStruct((t, d), F32),
        compiler_params=_params(1),
        name="mix_out_mlp",
    )(h, gb, g, g, g, attn, w_bf.array, conv_w, mod.rows, ln_g.reshape(1, d), ln_b.reshape(1, d),
      *mlp.args)


def _mlp_kernel(h_ref, *rest, alpha):
    rest[-1][...] = _mlp_residual(h_ref[...], rest[:-1], alpha)


def _mlp(h, mlp, tm, alpha):
    t, d = h.shape
    tok = lambda i: (i, 0)
    return pl.pallas_call(
        functools.partial(_mlp_kernel, alpha=alpha),
        grid=(t // tm,),
        in_specs=[pl.BlockSpec((tm, d), tok)] + mlp.specs,
        out_specs=pl.BlockSpec((tm, d), tok),
        out_shape=jax.ShapeDtypeStruct((t, d), F32),
        compiler_params=_params(1),
        name="mlp",
    )(h, *mlp.args)


def _fourier_in_kernel(x_ref, sh_ref, sc_ref, gcs_ref, z_ref):
    u = _modulated(x_ref[...], sh_ref, sc_ref)
    gcs = gcs_ref[...]
    for g in range(u.shape[1] // FOURIER_GROUP):
        lo, hi = g * FOURIER_GROUP, (g + 1) * FOURIER_GROUP
        zz = jnp.dot(u[:, lo:hi], gcs, preferred_element_type=F32)
        z_ref[0, 0, :, lo:hi] = zz[:, 0:FOURIER_GROUP].astype(BF16)
        z_ref[0, 1, :, lo:hi] = zz[:, FOURIER_GROUP:2 * FOURIER_GROUP].astype(BF16)


def _fourier_in(x, mod, layer, row_of_step, gcs, tm, seq):
    t, d = x.shape
    steps_per_seq = seq // tm
    return pl.pallas_call(
        _fourier_in_kernel,
        grid=(t // tm,),
        in_specs=[pl.BlockSpec((tm, d), lambda i: (i, 0)), mod.spec(layer, 0, row_of_step),
                  mod.spec(layer, 1, row_of_step), _const_spec(gcs.shape)],
        out_specs=pl.BlockSpec((1, 2, tm, d), lambda i: (i // steps_per_seq, 0, i % steps_per_seq, 0)),
        out_shape=jax.ShapeDtypeStruct((t // seq, 2, seq, d), BF16),
        compiler_params=_params(1),
        name="fourier_in",
    )(x, mod.rows, mod.rows, gcs)


FOURIER_SLAB = 16
FOURIER_OUT_SLAB = 8


def _fourier_rows_kernel(x_ref, sh_ref, sc_ref, k1_ref, gm_ref, z_ref):
    w, nb, d = x_ref.shape[1], x_ref.shape[2], x_ref.shape[3]
    rows = w * nb
    u = _modulated(x_ref[0].reshape(rows, d), sh_ref, sc_ref)
    y = jnp.dot(k1_ref[0], u, preferred_element_type=F32).astype(BF16)
    gm = gm_ref[...]
    for g in range(d // FOURIER_GROUP):
        lo, hi = g * FOURIER_GROUP, (g + 1) * FOURIER_GROUP
        yy = jnp.concatenate([y[0:rows, lo:hi], y[rows:2 * rows, lo:hi]], axis=1)
        zz = jnp.dot(yy, gm, preferred_element_type=F32).astype(BF16)
        z_ref[0, 0, :, :, lo:hi] = zz[:, 0:FOURIER_GROUP].reshape(w, nb, FOURIER_GROUP)
        z_ref[0, 1, :, :, lo:hi] = zz[:, FOURIER_GROUP:2 * FOURIER_GROUP].reshape(w, nb, FOURIER_GROUP)


def _fourier_rows(h, mod, layer, k1, gm, batch):
    t, d = h.shape
    w, nb = GRID_W, FOURIER_SLAB
    h4 = h.reshape(batch, w, w, d)
    return pl.pallas_call(
        _fourier_rows_kernel,
        grid=(w // nb, batch),
        in_specs=[pl.BlockSpec((1, w, nb, d), lambda j, bi: (bi, 0, j, 0)),
                  mod.spec(layer, 0, lambda j, bi: bi), mod.spec(layer, 1, lambda j, bi: bi),
                  pl.BlockSpec((1,) + k1.shape[1:], lambda j, bi: (j, 0, 0)), _const_spec(gm.shape)],
        out_specs=pl.BlockSpec((1, 2, w, nb, d), lambda j, bi: (bi, 0, 0, j, 0)),
        out_shape=jax.ShapeDtypeStruct((batch, 2, w, w, d), BF16),
        compiler_params=_params(2),
        name="fourier_rows",
    )(h4, mod.rows, mod.rows, k1, gm)


def _fourier_cols_kernel(z_ref, h_ref, k2_ref, w_ref, gate_ref, lg_ref, lb_ref, *rest, alpha):
    mlp_refs, o_ref = rest[:-1], rest[-1]
    w, d = z_ref.shape[3], z_ref.shape[4]
    nd = FOURIER_OUT_SLAB
    slabs = range(0, z_ref.shape[2], nd)
    ys = []
    for s in slabs:
        zz = z_ref[0, :, s:s + nd].reshape(2 * nd * w, d)
        f = jnp.dot(k2_ref[...], zz, preferred_element_type=F32).astype(BF16)
        ys.append(jnp.dot(f, w_ref[...], preferred_element_type=F32))
    h1s = [_post_norm(h_ref[0, :, s:s + nd, :].reshape(w * nd, d), y, gate_ref, lg_ref, lb_ref, alpha)
           for s, y in zip(slabs, ys)]
    for s, h1 in zip(slabs, h1s):
        o_ref[0, :, s:s + nd, :] = _mlp_residual(h1, mlp_refs, alpha).reshape(w, nd, d)


def _fourier_cols(z, h, k2, w_bf, mod, layer, ln_g, ln_b, mlp, alpha):
    batch, _, w, _, d = z.shape
    nd = 2 * FOURIER_OUT_SLAB
    h4 = h.reshape(batch, w, w, d)
    vec = pl.BlockSpec((1, d), lambda bi, j: (0, 0))
    hblk = pl.BlockSpec((1, w, nd, d), lambda bi, j: (bi, 0, j, 0))
    out = pl.pallas_call(
        functools.partial(_fourier_cols_kernel, alpha=alpha),
        grid=(batch, w // nd),
        in_specs=[pl.BlockSpec((1, 2, nd, w, d), lambda bi, j: (bi, 0, j, 0, 0)), hblk,
                  _const_spec(k2.shape), w_bf.spec(),
                  mod.spec(layer, 2, lambda bi, j: bi), vec, vec] + mlp.specs,
        out_specs=hblk,
        out_shape=jax.ShapeDtypeStruct((batch, w, w, d), F32),
        compiler_params=_params(2),
        name="fourier_cols_mlp",
    )(z, h4, k2, w_bf.array, mod.rows, ln_g.reshape(1, d), ln_b.reshape(1, d), *mlp.args)
    return out.reshape(batch * w * w, d)


def _fourier_dense_kernel(z_ref, cs_ref, o_ref, *, norm):
    zc = jnp.concatenate([z_ref[0, 0], z_ref[0, 1]], axis=0)
    o_ref[0] = (jnp.dot(cs_ref[...], zc, preferred_element_type=F32) * norm).astype(BF16)


def _fourier_dense(z, cs, norm):
    b, _, seq, d = z.shape
    return pl.pallas_call(
        functools.partial(_fourier_dense_kernel, norm=norm),
        grid=(b,),
        in_specs=[pl.BlockSpec((1, 2, seq, d), lambda bi: (bi, 0, 0, 0)), _const_spec(cs.shape)],
        out_specs=pl.BlockSpec((1, seq, d), lambda bi: (bi, 0, 0)),
        out_shape=jax.ShapeDtypeStruct((b, seq, d), BF16),
        compiler_params=_params(1),
        name="fourier_dense",
    )(z, cs)


def _proj_out_kernel(h_ref, a_ref, w_ref, gate_ref, lg_ref, lb_ref, o_ref, *, alpha):
    y = jnp.dot(a_ref[...], w_ref[...], preferred_element_type=F32)
    o_ref[...] = _post_norm(h_ref[...], y, gate_ref, lg_ref, lb_ref, alpha)


def _proj_out(h, a, w_bf, mod, layer, row_of_step, ln_g, ln_b, tm, alpha):
    t, d = h.shape
    tok = lambda i: (i, 0)
    vec = pl.BlockSpec((1, d), lambda i: (0, 0))
    return pl.pallas_call(
        functools.partial(_proj_out_kernel, alpha=alpha),
        grid=(t // tm,),
        in_specs=[pl.BlockSpec((tm, d), tok), pl.BlockSpec((tm, a.shape[1]), tok),
                  w_bf.spec(), mod.spec(layer, 2, row_of_step), vec, vec],
        out_specs=pl.BlockSpec((tm, d), tok),
        out_shape=jax.ShapeDtypeStruct((t, d), F32),
        compiler_params=_params(1),
        name="proj_out",
    )(h, a, w_bf.array, mod.rows, ln_g.reshape(1, d), ln_b.reshape(1, d))


def _dft_cos_sin(n, denom):
    idx = np.arange(n, dtype=np.int64)
    ang = ((idx[:, None] * idx[None, :]) % denom) * (2.0 * math.pi / denom)
    return np.cos(ang), np.sin(ang)


def _bf16_const(a):
    return jnp.asarray(np.asarray(a, np.float32)).astype(BF16)


def _rope_tables(seq):
    half = DIFF_QK_DIM // 4
    t = np.arange(seq)
    inv = ROPE_BASE ** (-np.arange(half, dtype=np.float64) / half)
    lane = np.arange(LANES)
    use_col = (lane % DIFF_QK_DIM) >= DIFF_QK_DIM // 2
    second = (lane % (2 * half)) >= half
    pos = np.where(use_col[None, :], (t % GRID_W)[:, None], (t // GRID_W)[:, None])
    ang = pos * inv[lane % half][None, :]
    cos, sin = np.cos(ang), np.sin(ang)
    tabs = (cos, np.where(second[None, :], 0.0, -sin), np.where(second[None, :], sin, 0.0))
    return tuple(jnp.asarray(a, F32) for a in tabs)


def _fourier_tables(seq, norm):
    w, nb, nd = GRID_W, FOURIER_SLAB, FOURIER_OUT_SLAB
    idx = np.arange(w, dtype=np.int64)
    slab = np.arange(w // nb, dtype=np.int64)
    col = np.arange(nb, dtype=np.int64)
    t = idx[None, None, None, :] * w + slab[:, None, None, None] * nb + col[None, None, :, None]
    ang = ((idx[None, :, None, None] * t) % seq) * (2.0 * math.pi / seq)
    trig = np.stack([np.cos(ang), -np.sin(ang)], axis=1)
    k1 = trig[..., None] * np.eye(nb)[None, None, None, :, None, :]
    c64, s64 = _dft_cos_sin(w, w)
    cs = np.stack([c64, s64], axis=1) * norm
    k2 = cs[:, None, :, None, :] * np.eye(nd)[None, :, None, :, None]
    return (_bf16_const(k1.reshape(w // nb, 2 * w * nb, w * nb)),
            _bf16_const(k2.reshape(w * nd, 2 * nd * w)))


def _no_rope_tables(seq):
    return jnp.ones((seq, LANES), F32), jnp.zeros((seq, LANES), F32), jnp.zeros((seq, LANES), F32)


def kernel(x, c, ctx, c_ctx, ada_w, ada_b, ln_g, ln_b, mlp_w1, mlp_w2, w_in, conv_w, lambda_qk,
           subln_g, w_out_mix, w_out_fourier):
    batch, seq, d = x.shape
    ctx_len = ctx.shape[1]
    depth = ada_w.shape[0]
    alpha = (2.0 * depth) ** 0.25
    last_attn = 2 * ((depth - 1) // 2)
    tm = 512
    assert seq == GRID_W * GRID_W and seq % tm == 0 and batch < SUBLANES and ctx_len % SUBLANES == 0
    assert depth % 2 == 0

    cond8 = jnp.zeros((SUBLANES, d), F32).at[:batch].set(c).at[batch].set(c_ctx)
    mod = _Mod(_modulation(cond8, ada_w, ada_b), d)
    lat_row = lambda i: i // (seq // tm)
    ctx_row = lambda i: batch

    w_in_first = w_in[0:1].astype(BF16)

    rope = _rope_tables(seq)
    no_rope = _no_rope_tables(ctx_len)

    gc, gs = _dft_cos_sin(FOURIER_GROUP, FOURIER_GROUP)
    gcs = _bf16_const(np.concatenate([gc, -gs], axis=1))
    gmat = _bf16_const(np.block([[gc, -gs], [gs, gc]]))
    k1, k2 = _fourier_tables(seq, (seq * FOURIER_GROUP) ** -0.5)
    cc, sc_ = _dft_cos_sin(ctx_len, ctx_len)
    cs_ctx = _bf16_const(np.concatenate([cc, sc_], axis=1))

    h = x.reshape(batch * seq, d)
    hc = ctx.reshape(batch * ctx_len, d)

    for i in range(depth):
        j = i // 2
        update_ctx = i < last_attn
        if i % 2 == 0:
            lam_init = 0.8 - 0.6 * math.exp(-0.3 * i)
            w_in_j = _Stacked(w_in_first, 0) if j == 0 else _Stacked(w_in_next, 0)
            mix_row = lambda t: t // (seq // (2 * tm))
            gb, g, q, k, v = _inproj(h, mod, i, mix_row, w_in_j, rope, 2 * tm, tm, seq)
            gbc, gcx, qc, kc, vc = _inproj(hc, mod, i, ctx_row, w_in_j, no_rope, ctx_len, ctx_len, ctx_len)
            casts = [(mlp_w1, i, 2), (mlp_w2, i, 2), (w_out_mix, j, 1), (w_out_fourier, j, 1)]
            if j + 1 < w_in.shape[0]:
                casts.append((w_in, j + 1, 1))
            attn, w1_pair, w2_pair, w_mix_one, w_four_one, *rest = _attention_stream(
                lambda_qk[j], q, [k, kc], [v, vc], subln_g[j], lam_init, 1024, 256, casts)
            w_in_next = rest[0] if rest else None
            w_mix_j = _Stacked(w_mix_one, 0)
            w1, w2 = _Stacked(w1_pair, 0), _Stacked(w2_pair, 0)
            mlp_ctx = _MlpTail(w1, w2, mod, i, ctx_row, ln_g[i, 1], ln_b[i, 1])
            mlp_lat = _MlpTail(w1, w2, mod, i, mix_row, ln_g[i, 1], ln_b[i, 1])
            h = _mix_out(h, gb, g, attn, w_mix_j, conv_w[j], mod, i, mix_row,
                         ln_g[i, 0], ln_b[i, 0], mlp_lat, 2 * tm, tm, seq, alpha)
            if update_ctx:
                attn_c = _attention(lambda_qk[j], qc, [kc], [vc], subln_g[j], lam_init, ctx_len, ctx_len)
                hc = _mix_out(hc, gbc, gcx, attn_c, w_mix_j, conv_w[j], mod, i, ctx_row,
                              ln_g[i, 0], ln_b[i, 0], mlp_ctx, ctx_len, ctx_len, ctx_len, alpha)
        else:
            w_four_j = _Stacked(w_four_one, 0)
            w1, w2 = _Stacked(w1_pair, 1), _Stacked(w2_pair, 1)
            mlp_ctx = _MlpTail(w1, w2, mod, i, ctx_row, ln_g[i, 1], ln_b[i, 1])
            z = _fourier_rows(h, mod, i, k1, gmat, batch)
            mlp_slab = _MlpTail(w1, w2, mod, i, lambda bi, sl: bi, ln_g[i, 1], ln_b[i, 1])
            h = _fourier_cols(z, h, k2, w_four_j, mod, i, ln_g[i, 0], ln_b[i, 0], mlp_slab, alpha)
            if update_ctx:
                zc = _fourier_in(hc, mod, i, ctx_row, gcs, ctx_len, ctx_len)
                fc = _fourier_dense(zc, cs_ctx, (ctx_len * FOURIER_GROUP) ** -0.5)
                hc1 = _proj_out(hc, fc.reshape(batch * ctx_len, d), w_four_j, mod, i, ctx_row,
                                ln_g[i, 0], ln_b[i, 0], ctx_len, alpha)
                hc = _mlp(hc1, mlp_ctx, ctx_len, alpha)
    return h.reshape(batch, seq, d)
```

```python
import functools
import math

import jax
import jax.numpy as jnp
import numpy as np
from jax import lax
from jax.experimental import pallas as pl
from jax.experimental.pallas import tpu as pltpu

F32 = jnp.float32
BF16 = jnp.bfloat16

GRID_W = 64
CONV_WIDTH = 3
D_CONV = 512
N_DIFF_HEADS = 4
DIFF_QK_DIM = 64
DIFF_V_DIM = 128
D_ATTN = N_DIFF_HEADS * DIFF_V_DIM
FOURIER_GROUP = 128
ROPE_BASE = 10000.0
LN_EPS = 1e-6
SUBLN_EPS = 1e-5
COL_CONV = 3 * D_CONV
COL_QK = N_DIFF_HEADS * 2 * DIFF_QK_DIM
Q0 = COL_CONV
K0 = Q0 + COL_QK
V0 = K0 + COL_QK
Q_SCALE = math.log2(math.e) * DIFF_QK_DIM ** -0.5

LANES = 128
SUBLANES = 8
VMEM_LIMIT = 56 * 1024 * 1024


def _params(n_axes):
    return pltpu.CompilerParams(
        dimension_semantics=("arbitrary",) * n_axes, vmem_limit_bytes=VMEM_LIMIT)


def _const_spec(shape):
    nd = len(shape)
    return pl.BlockSpec(shape, lambda *_: (0,) * nd, pipeline_mode=pl.Buffered(1))


class _Stacked:
    def __init__(self, array, index):
        self.array, self.index = array, index

    def spec(self):
        k, n = self.array.shape[1:]
        index = self.index
        return pl.BlockSpec((None, k, n), lambda *_: (index, 0, 0), pipeline_mode=pl.Buffered(1))


def _layer_norm(x):
    mu = jnp.mean(x, axis=-1, keepdims=True)
    xc = x - mu
    var = jnp.mean(xc * xc, axis=-1, keepdims=True)
    return xc * lax.rsqrt(var + LN_EPS)


def _modulated(x, shift_ref, scale_ref):
    return (_layer_norm(x) * (1.0 + scale_ref[0]) + shift_ref[0]).astype(BF16)


def _post_norm(h, y, gate_ref, g_ref, b_ref, alpha):
    return _layer_norm(alpha * h + gate_ref[0] * y) * g_ref[...] + b_ref[...]


def _mod_kernel(cond_ref, w_ref, b_ref, o_ref):
    c = cond_ref[...]
    a = (c * (1.0 / (1.0 + jnp.exp(-c)))).astype(BF16)
    o_ref[0] = jnp.dot(a, w_ref[0].astype(BF16), preferred_element_type=F32) + b_ref[0]


def _modulation(cond8, ada_w, ada_b):
    depth, d, n6 = ada_w.shape
    tn = 1536
    return pl.pallas_call(
        _mod_kernel,
        grid=(depth, n6 // tn),
        in_specs=[
            pl.BlockSpec((SUBLANES, d), lambda l, j: (0, 0)),
            pl.BlockSpec((1, d, tn), lambda l, j: (l, 0, j)),
            pl.BlockSpec((1, 1, tn), lambda l, j: (l, 0, j)),
        ],
        out_specs=pl.BlockSpec((1, SUBLANES, tn), lambda l, j: (l, 0, j)),
        out_shape=jax.ShapeDtypeStruct((depth, SUBLANES, n6), F32),
        compiler_params=_params(2),
        name="modulation",
    )(cond8, ada_w, ada_b.reshape(depth, 1, n6))


class _Mod:
    def __init__(self, table, d):
        depth = table.shape[0]
        self.rows = table.reshape(depth, SUBLANES, 6, d).transpose(0, 2, 1, 3).reshape(
            depth * 6 * SUBLANES, 1, d)
        self.d = d

    def spec(self, layer, which, row_of_step):
        base = (layer * 6 + which) * SUBLANES
        return pl.BlockSpec((1, 1, self.d), lambda *idx: (base + row_of_step(*idx), 0, 0))


def _rope_store(dst_ref, rows, p, cos_ref, sin_lo_ref, sin_hi_ref, scale, transposed=False):
    cos, sin_lo, sin_hi = cos_ref[rows, :], sin_lo_ref[rows, :], sin_hi_ref[rows, :]
    half = DIFF_QK_DIM // 4
    for c in range(p.shape[1] // LANES):
        x = p[:, c * LANES:(c + 1) * LANES]
        ahead = pltpu.roll(x, LANES - half, 1)
        behind = pltpu.roll(x, half, 1)
        r = x * cos + ahead * sin_lo + behind * sin_hi
        if scale != 1.0:
            r = r * scale
        if transposed:
            dst_ref[c * LANES:(c + 1) * LANES, rows] = r.T.astype(dst_ref.dtype)
        else:
            dst_ref[rows, c * LANES:(c + 1) * LANES] = r.astype(dst_ref.dtype)


def _inproj_kernel(x_ref, sh_ref, sc_ref, w_ref, cos_ref, sin_lo_ref, sin_hi_ref,
                   gb_ref, g_ref, q_ref, k_ref, v_ref, *, sub):
    rope = (cos_ref, sin_lo_ref, sin_hi_ref)
    for lo in range(0, x_ref.shape[0], sub):
        rows = pl.ds(lo, sub)
        u = _modulated(x_ref[rows, :], sh_ref, sc_ref)
        pc = jnp.dot(u, w_ref[:, 0:COL_CONV], preferred_element_type=F32)
        gb_ref[rows, :] = pc[:, 0:D_CONV]
        g_ref[rows, :] = pc[:, D_CONV:2 * D_CONV] * pc[:, 2 * D_CONV:3 * D_CONV]
        pq = jnp.dot(u, w_ref[:, Q0:K0], preferred_element_type=F32)
        _rope_store(q_ref, rows, pq, *rope, Q_SCALE)
        pk = jnp.dot(u, w_ref[:, K0:V0], preferred_element_type=F32)
        _rope_store(k_ref.at[0], rows, pk, *rope, 1.0, transposed=True)
        v_ref[0, rows, :] = jnp.dot(u, w_ref[:, V0:V0 + D_ATTN], preferred_element_type=F32).astype(BF16)


def _inproj(x, mod, layer, row_of_step, w_bf, rope, tm, sub, seq):
    t, d = x.shape
    steps_per_seq = seq // tm
    tok = lambda i: (i, 0)
    kvi = lambda i: (i // steps_per_seq, i % steps_per_seq, 0)
    tab = pl.BlockSpec((tm, LANES), lambda i: (i % steps_per_seq, 0))
    out = lambda n: pl.BlockSpec((tm, n), tok)
    return pl.pallas_call(
        functools.partial(_inproj_kernel, sub=sub),
        grid=(t // tm,),
        in_specs=[pl.BlockSpec((tm, d), tok), mod.spec(layer, 0, row_of_step),
                  mod.spec(layer, 1, row_of_step), w_bf.spec(), tab, tab, tab],
        out_specs=[out(D_CONV), out(D_CONV), out(COL_QK),
                   pl.BlockSpec((1, COL_QK, tm), lambda i: (i // steps_per_seq, 0, i % steps_per_seq)),
                   pl.BlockSpec((1, tm, D_ATTN), kvi)],
        out_shape=[jax.ShapeDtypeStruct((t, D_CONV), F32), jax.ShapeDtypeStruct((t, D_CONV), F32),
                   jax.ShapeDtypeStruct((t, COL_QK), BF16),
                   jax.ShapeDtypeStruct((t // seq, COL_QK, seq), BF16),
                   jax.ShapeDtypeStruct((t // seq, seq, D_ATTN), BF16)],
        compiler_params=_params(1),
        name="inproj",
    )(x, mod.rows, mod.rows, w_bf.array, *rope)


def _attn_kernel(lq_ref, q_ref, *refs, lam_init, n_src, sub):
    k_refs, v_refs = refs[:n_src], refs[n_src:2 * n_src]
    sg_ref, o_ref = refs[2 * n_src], refs[2 * n_src + 1]
    lq = lq_ref[...]
    lam = (jnp.exp(jnp.sum(lq[0:1] * lq[1:2], axis=1, keepdims=True))
           - jnp.exp(jnp.sum(lq[2:3] * lq[3:4], axis=1, keepdims=True)) + lam_init)
    def scores(qc):
        return [jnp.dot(qc, k[0], preferred_element_type=F32) for k in k_refs]

    def softmax_parts(s):
        m = functools.reduce(jnp.maximum, [jnp.max(x, axis=1, keepdims=True) for x in s])
        e = [jnp.exp2(x - m) for x in s]
        return e, sum(jnp.sum(x, axis=1, keepdims=True) for x in e)

    def score_tile(r):
        q = q_ref[r * sub:(r + 1) * sub, :]
        lane = lax.broadcasted_iota(jnp.int32, q.shape, 1)
        zero = jnp.zeros_like(q)
        return (scores(jnp.where(lane < DIFF_QK_DIM, q, zero)),
                scores(jnp.where(lane >= DIFF_QK_DIM, q, zero)))

    def project(r, e1, l1, e2, l2):
        ratio = lam * l1 / l2
        o = sum(jnp.dot((a - ratio * b).astype(BF16), v[0], preferred_element_type=F32)
                for a, b, v in zip(e1, e2, v_refs)) / l1
        y = o * lax.rsqrt(jnp.mean(o * o, axis=1, keepdims=True) + SUBLN_EPS)
        o_ref[r * sub:(r + 1) * sub, :] = (y * sg_ref[...] * (1.0 - lam_init)).astype(o_ref.dtype)

    n_sub = q_ref.shape[0] // sub
    s_next = score_tile(0)
    parts = None
    for r in range(n_sub):
        s1, s2 = s_next
        cur = softmax_parts(s1) + softmax_parts(s2)
        if r + 1 < n_sub:
            s_next = score_tile(r + 1)
        if parts is not None:
            project(r - 1, *parts)
        parts = cur
    project(n_sub - 1, *parts)


def _attention(lq, q, ks, vs, sg, lam_init, tq, sub):
    t = q.shape[0]
    b = ks[0].shape[0]
    nq = t // b // tq
    qo = pl.BlockSpec((tq, DIFF_V_DIM), lambda bi, h, i: (bi * nq + i, h))
    src = lambda a: pl.BlockSpec((1, a.shape[1], DIFF_V_DIM), lambda bi, h, i: (bi, 0, h))
    ksrc = lambda a: pl.BlockSpec((1, 2 * DIFF_QK_DIM, a.shape[2]), lambda bi, h, i: (bi, h, 0))
    return pl.pallas_call(
        functools.partial(_attn_kernel, lam_init=lam_init, n_src=len(ks), sub=sub),
        grid=(b, N_DIFF_HEADS, nq),
        in_specs=[pl.BlockSpec(lq.shape, lambda bi, h, i: (0, 0)), qo]
                 + [ksrc(a) for a in ks] + [src(a) for a in vs]
                 + [pl.BlockSpec((1, DIFF_V_DIM), lambda bi, h, i: (0, 0))],
        out_specs=qo,
        out_shape=jax.ShapeDtypeStruct((t, D_ATTN), BF16),
        compiler_params=_params(3),
        name="diff_attention",
    )(lq, q, *ks, *vs, sg.reshape(1, DIFF_V_DIM))


STREAM_CHUNK = 256
SUM_FLOOR = 2.0 ** -80


def _attn_stream_kernel(lq_ref, q_ref, *refs, lam_init, n_src, n_cast, sub):
    k_refs, v_refs = refs[:n_src], refs[n_src:2 * n_src]
    sg_ref, o_ref = refs[2 * n_src], refs[2 * n_src + 1 + n_cast]
    cast_in = refs[2 * n_src + 1:2 * n_src + 1 + n_cast]
    cast_out = refs[2 * n_src + 2 + n_cast:2 * n_src + 2 + 2 * n_cast]
    vaug_refs = refs[2 * n_src + 2 + 2 * n_cast:2 * n_src + 2 + 2 * n_cast + n_src]
    kmax_ref = refs[-1]
    for src_ref, dst_ref in zip(cast_in, cast_out):
        dst_ref[...] = src_ref[...].astype(BF16)
    lq = lq_ref[...]
    lam = (jnp.exp(jnp.sum(lq[0:1] * lq[1:2], axis=1, keepdims=True))
           - jnp.exp(jnp.sum(lq[2:3] * lq[3:4], axis=1, keepdims=True)) + lam_init)
    half = lax.broadcasted_iota(jnp.int32, (1, 2 * DIFF_QK_DIM), 1) < DIFF_QK_DIM

    @pl.when(pl.program_id(2) == 0)
    def _():
        best = jnp.zeros((1, 2 * DIFF_QK_DIM), F32)
        for k, v, va in zip(k_refs, v_refs, vaug_refs):
            va[:, 0:DIFF_V_DIM] = v[0]
            va[:, DIFF_V_DIM:2 * DIFF_V_DIM] = jnp.ones((va.shape[0], DIFF_V_DIM), BF16)
            k2 = jnp.square(k[0].astype(F32))
            n1 = jnp.sum(k2[0:DIFF_QK_DIM], axis=0, keepdims=True)
            n2 = jnp.sum(k2[DIFF_QK_DIM:2 * DIFF_QK_DIM], axis=0, keepdims=True)
            both = jnp.where(half, jnp.max(n1, axis=1, keepdims=True), jnp.max(n2, axis=1, keepdims=True))
            best = jnp.maximum(best, both)
        kmax_ref[...] = jnp.broadcast_to(jnp.sqrt(best), kmax_ref.shape)

    chunks = []
    for j, k in enumerate(k_refs):
        size = min(k.shape[2], STREAM_CHUNK)
        chunks += [(j, c0, size) for c0 in range(0, k.shape[2], size)]

    def component(q, first):
        mask = half if first else jnp.logical_not(half)
        qc = jnp.where(mask, q, jnp.zeros_like(q))
        qf = qc.astype(F32)
        kmax = kmax_ref[0:1, 0:1] if first else kmax_ref[0:1, DIFF_QK_DIM:DIFF_QK_DIM + 1]
        bound = jnp.sqrt(jnp.sum(qf * qf, axis=1, keepdims=True)) * kmax
        score = lambda ch: jnp.dot(qc, k_refs[ch[0]][0, :, ch[1]:ch[1] + ch[2]],
                                   preferred_element_type=F32)
        acc = None
        s_next = score(chunks[0])
        for idx, (j, c0, size) in enumerate(chunks):
            s = s_next
            if idx + 1 < len(chunks):
                s_next = score(chunks[idx + 1])
            e = jnp.exp2(s - bound).astype(BF16)
            part = jnp.dot(e, vaug_refs[j][c0:c0 + size, :], preferred_element_type=F32)
            acc = part if acc is None else acc + part
        return acc[:, 0:DIFF_V_DIM], acc[:, DIFF_V_DIM:2 * DIFF_V_DIM]

    def exact_component(q, first):
        mask = half if first else jnp.logical_not(half)
        qc = jnp.where(mask, q, jnp.zeros_like(q))
        s = [jnp.dot(qc, k[0], preferred_element_type=F32) for k in k_refs]
        m = functools.reduce(jnp.maximum, [jnp.max(x, axis=1, keepdims=True) for x in s])
        acc = sum(jnp.dot(jnp.exp2(x - m).astype(BF16), va[...], preferred_element_type=F32)
                  for x, va in zip(s, vaug_refs))
        return acc[:, 0:DIFF_V_DIM], acc[:, DIFF_V_DIM:2 * DIFF_V_DIM]

    def emit(r, parts):
        (o1, l1), (o2, l2) = parts
        o = o1 / l1 - lam * (o2 / l2)
        y = o * lax.rsqrt(jnp.mean(o * o, axis=1, keepdims=True) + SUBLN_EPS)
        o_ref[r * sub:(r + 1) * sub, :] = (y * sg_ref[...] * (1.0 - lam_init)).astype(o_ref.dtype)
        return jnp.minimum(jnp.min(l1), jnp.min(l2))

    n_sub = q_ref.shape[0] // sub
    tiles = [q_ref[r * sub:(r + 1) * sub, :] for r in range(n_sub)]
    smallest_sum = functools.reduce(
        jnp.minimum, [emit(r, (component(q, True), component(q, False))) for r, q in enumerate(tiles)])

    @pl.when(jnp.logical_not(smallest_sum >= SUM_FLOOR))
    def _():
        for r, q in enumerate(tiles):
            emit(r, (exact_component(q, True), exact_component(q, False)))


def _attention_stream(lq, q, ks, vs, sg, lam_init, tq, sub, casts):
    t = q.shape[0]
    b = ks[0].shape[0]
    nq = t // b // tq
    steps = b * N_DIFF_HEADS * nq
    qo = pl.BlockSpec((tq, DIFF_V_DIM), lambda bi, h, i: (bi * nq + i, h))
    src = lambda a: pl.BlockSpec((1, a.shape[1], DIFF_V_DIM), lambda bi, h, i: (bi, 0, h))
    ksrc = lambda a: pl.BlockSpec((1, 2 * DIFF_QK_DIM, a.shape[2]), lambda bi, h, i: (bi, h, 0))
    cast_in, cast_out, cast_shapes = [], [], []
    for w, first_layer, count in casts:
        _, rows, cols = w.shape
        slab = count * rows // steps
        assert slab * steps == count * rows and slab % (2 * SUBLANES) == 0
        per_layer = rows // slab

        def index(bi, h, i, per_layer=per_layer, base=0):
            flat = (bi * N_DIFF_HEADS + h) * nq + i
            return base + flat // per_layer, flat % per_layer, 0

        cast_in.append(pl.BlockSpec((1, slab, cols), functools.partial(index, base=first_layer)))
        cast_out.append(pl.BlockSpec((1, slab, cols), index))
        cast_shapes.append(jax.ShapeDtypeStruct((count, rows, cols), BF16))
    return pl.pallas_call(
        functools.partial(_attn_stream_kernel, lam_init=lam_init, n_src=len(ks), n_cast=len(casts), sub=sub),
        grid=(b, N_DIFF_HEADS, nq),
        in_specs=[pl.BlockSpec(lq.shape, lambda bi, h, i: (0, 0)), qo]
                 + [ksrc(a) for a in ks] + [src(a) for a in vs]
                 + [pl.BlockSpec((1, DIFF_V_DIM), lambda bi, h, i: (0, 0))] + cast_in,
        out_specs=[qo] + cast_out,
        out_shape=[jax.ShapeDtypeStruct((t, D_ATTN), BF16)] + cast_shapes,
        scratch_shapes=[pltpu.VMEM((v.shape[1], 2 * DIFF_V_DIM), BF16) for v in vs]
                       + [pltpu.VMEM((SUBLANES, 2 * DIFF_QK_DIM), F32)],
        compiler_params=_params(3),
        name="diff_attention_stream",
    )(lq, q, *ks, *vs, sg.reshape(1, DIFF_V_DIM), *[c[0] for c in casts])


MLP_CHUNK = 512


def _mlp_residual(h, mlp_refs, alpha):
    sh_ref, sc_ref, w1_ref, w2_ref, gate_ref, lg_ref, lb_ref = mlp_refs
    u = _modulated(h, sh_ref, sc_ref)
    y = jnp.zeros(h.shape, F32)
    for c in range(w1_ref.shape[1] // MLP_CHUNK):
        hid = jnp.dot(u, w1_ref[:, c * MLP_CHUNK:(c + 1) * MLP_CHUNK], preferred_element_type=F32)
        hid = jnp.square(jnp.maximum(hid, 0.0)).astype(BF16)
        y = y + jnp.dot(hid, w2_ref[c * MLP_CHUNK:(c + 1) * MLP_CHUNK, :], preferred_element_type=F32)
    return _post_norm(h, y, gate_ref, lg_ref, lb_ref, alpha)


class _MlpTail:
    def __init__(self, w1, w2, mod, layer, row_of_step, ln_g, ln_b):
        d = mod.d
        vec = pl.BlockSpec((1, d), lambda *_: (0, 0))
        self.specs = [mod.spec(layer, 3, row_of_step), mod.spec(layer, 4, row_of_step), w1.spec(),
                      w2.spec(), mod.spec(layer, 5, row_of_step), vec, vec]
        self.args = [mod.rows, mod.rows, w1.array, w2.array, mod.rows, ln_g.reshape(1, d),
                     ln_b.reshape(1, d)]


def _mix_out_kernel(h_ref, gb_ref, g_ref, gprev_ref, gnext_ref, a_ref, w_ref, cw_ref,
                    gate_ref, lg_ref, lb_ref, *rest, steps_per_seq, sub, alpha):
    mlp_refs, o_ref = rest[:-1], rest[-1]
    i = pl.program_id(0)
    g = g_ref[...]
    tm = g.shape[0]
    row = lax.broadcasted_iota(jnp.int32, g.shape, 0)
    pos = i % steps_per_seq
    keep_prev = jnp.where(pos == 0, 0.0, 1.0)
    keep_next = jnp.where(pos == steps_per_seq - 1, 0.0, 1.0)
    before = jnp.where(row == 0, gprev_ref[SUBLANES - 1:SUBLANES, :] * keep_prev, pltpu.roll(g, 1, 0))
    after = jnp.where(row == tm - 1, gnext_ref[0:1, :] * keep_next, pltpu.roll(g, tm - 1, 0))
    cw = cw_ref[...]
    conv = before * cw[0:1] + g * cw[1:2] + after * cw[2:3]
    yc = (gb_ref[...] * conv).astype(BF16)
    subs = range(0, tm, sub)
    ys = [jnp.dot(yc[lo:lo + sub], w_ref[0:D_CONV, :], preferred_element_type=F32)
          + jnp.dot(a_ref[lo:lo + sub, :], w_ref[D_CONV:D_CONV + D_ATTN, :], preferred_element_type=F32)
          for lo in subs]
    h1s = [_post_norm(h_ref[lo:lo + sub, :], y, gate_ref, lg_ref, lb_ref, alpha) for lo, y in zip(subs, ys)]
    for lo, h1 in zip(subs, h1s):
        o_ref[lo:lo + sub, :] = _mlp_residual(h1, mlp_refs, alpha)


def _mix_out(h, gb, g, attn, w_bf, conv_w, mod, layer, row_of_step, ln_g, ln_b, mlp, tm, sub, seq, alpha):
    t, d = h.shape
    steps_per_seq = seq // tm
    blocks8 = tm // SUBLANES
    last8 = t // SUBLANES - 1
    tok = lambda i: (i, 0)
    vec = pl.BlockSpec((1, d), lambda i: (0, 0))
    return pl.pallas_call(
        functools.partial(_mix_out_kernel, steps_per_seq=steps_per_seq, sub=sub, alpha=alpha),
        grid=(t // tm,),
        in_specs=[pl.BlockSpec((tm, d), tok), pl.BlockSpec((tm, D_CONV), tok),
                  pl.BlockSpec((tm, D_CONV), tok),
                  pl.BlockSpec((SUBLANES, D_CONV), lambda i: (jnp.maximum(i * blocks8 - 1, 0), 0)),
                  pl.BlockSpec((SUBLANES, D_CONV), lambda i: (jnp.minimum((i + 1) * blocks8, last8), 0)),
                  pl.BlockSpec((tm, D_ATTN), tok), w_bf.spec(),
                  pl.BlockSpec(conv_w.shape, lambda i: (0, 0)),
                  mod.spec(layer, 2, row_of_step), vec, vec] + mlp.specs,
        out_specs=pl.BlockSpec((tm, d), tok),
        out_shape=jax.ShapeDtypeStruct((t, d), F32),
        compiler_params=_params(1),
        name="mix_out_mlp",
    )(h, gb, g, g, g, attn, w_bf.array, conv_w, mod.rows, ln_g.reshape(1, d), ln_b.reshape(1, d),
      *mlp.args)


def _mlp_kernel(h_ref, *rest, alpha):
    rest[-1][...] = _mlp_residual(h_ref[...], rest[:-1], alpha)


def _mlp(h, mlp, tm, alpha):
    t, d = h.shape
    tok = lambda i: (i, 0)
    return pl.pallas_call(
        functools.partial(_mlp_kernel, alpha=alpha),
        grid=(t // tm,),
        in_specs=[pl.BlockSpec((tm, d), tok)] + mlp.specs,
        out_specs=pl.BlockSpec((tm, d), tok),
        out_shape=jax.ShapeDtypeStruct((t, d), F32),
        compiler_params=_params(1),
        name="mlp",
    )(h, *mlp.args)


def _fourier_in_kernel(x_ref, sh_ref, sc_ref, gcs_ref, z_ref):
    u = _modulated(x_ref[...], sh_ref, sc_ref)
    gcs = gcs_ref[...]
    for g in range(u.shape[1] // FOURIER_GROUP):
        lo, hi = g * FOURIER_GROUP, (g + 1) * FOURIER_GROUP
        zz = jnp.dot(u[:, lo:hi], gcs, preferred_element_type=F32)
        z_ref[0, 0, :, lo:hi] = zz[:, 0:FOURIER_GROUP].astype(BF16)
        z_ref[0, 1, :, lo:hi] = zz[:, FOURIER_GROUP:2 * FOURIER_GROUP].astype(BF16)


def _fourier_in(x, mod, layer, row_of_step, gcs, tm, seq):
    t, d = x.shape
    steps_per_seq = seq // tm
    return pl.pallas_call(
        _fourier_in_kernel,
        grid=(t // tm,),
        in_specs=[pl.BlockSpec((tm, d), lambda i: (i, 0)), mod.spec(layer, 0, row_of_step),
                  mod.spec(layer, 1, row_of_step), _const_spec(gcs.shape)],
        out_specs=pl.BlockSpec((1, 2, tm, d), lambda i: (i // steps_per_seq, 0, i % steps_per_seq, 0)),
        out_shape=jax.ShapeDtypeStruct((t // seq, 2, seq, d), BF16),
        compiler_params=_params(1),
        name="fourier_in",
    )(x, mod.rows, mod.rows, gcs)


FOURIER_SLAB = 16
FOURIER_OUT_SLAB = 8


def _fourier_rows_kernel(x_ref, sh_ref, sc_ref, k1_ref, gm_ref, z_ref):
    w, nb, d = x_ref.shape[1], x_ref.shape[2], x_ref.shape[3]
    rows = w * nb
    u = _modulated(x_ref[0].reshape(rows, d), sh_ref, sc_ref)
    y = jnp.dot(k1_ref[0], u, preferred_element_type=F32).astype(BF16)
    gm = gm_ref[...]
    for g in range(d // FOURIER_GROUP):
        lo, hi = g * FOURIER_GROUP, (g + 1) * FOURIER_GROUP
        yy = jnp.concatenate([y[0:rows, lo:hi], y[rows:2 * rows, lo:hi]], axis=1)
        zz = jnp.dot(yy, gm, preferred_element_type=F32).astype(BF16)
        z_ref[0, 0, :, :, lo:hi] = zz[:, 0:FOURIER_GROUP].reshape(w, nb, FOURIER_GROUP)
        z_ref[0, 1, :, :, lo:hi] = zz[:, FOURIER_GROUP:2 * FOURIER_GROUP].reshape(w, nb, FOURIER_GROUP)


def _fourier_rows(h, mod, layer, k1, gm, batch):
    t, d = h.shape
    w, nb = GRID_W, FOURIER_SLAB
    h4 = h.reshape(batch, w, w, d)
    return pl.pallas_call(
        _fourier_rows_kernel,
        grid=(w // nb, batch),
        in_specs=[pl.BlockSpec((1, w, nb, d), lambda j, bi: (bi, 0, j, 0)),
                  mod.spec(layer, 0, lambda j, bi: bi), mod.spec(layer, 1, lambda j, bi: bi),
                  pl.BlockSpec((1,) + k1.shape[1:], lambda j, bi: (j, 0, 0)), _const_spec(gm.shape)],
        out_specs=pl.BlockSpec((1, 2, w, nb, d), lambda j, bi: (bi, 0, 0, j, 0)),
        out_shape=jax.ShapeDtypeStruct((batch, 2, w, w, d), BF16),
        compiler_params=_params(2),
        name="fourier_rows",
    )(h4, mod.rows, mod.rows, k1, gm)


def _fourier_cols_kernel(z_ref, h_ref, k2_ref, w_ref, gate_ref, lg_ref, lb_ref, *rest, alpha):
    mlp_refs, o_ref = rest[:-1], rest[-1]
    w, d = z_ref.shape[3], z_ref.shape[4]
    nd = FOURIER_OUT_SLAB
    slabs = range(0, z_ref.shape[2], nd)
    ys = []
    for s in slabs:
        zz = z_ref[0, :, s:s + nd].reshape(2 * nd * w, d)
        f = jnp.dot(k2_ref[...], zz, preferred_element_type=F32).astype(BF16)
        ys.append(jnp.dot(f, w_ref[...], preferred_element_type=F32))
    h1s = [_post_norm(h_ref[0, :, s:s + nd, :].reshape(w * nd, d), y, gate_ref, lg_ref, lb_ref, alpha)
           for s, y in zip(slabs, ys)]
    for s, h1 in zip(slabs, h1s):
        o_ref[0, :, s:s + nd, :] = _mlp_residual(h1, mlp_refs, alpha).reshape(w, nd, d)


def _fourier_cols(z, h, k2, w_bf, mod, layer, ln_g, ln_b, mlp, alpha):
    batch, _, w, _, d = z.shape
    nd = 2 * FOURIER_OUT_SLAB
    h4 = h.reshape(batch, w, w, d)
    vec = pl.BlockSpec((1, d), lambda bi, j: (0, 0))
    hblk = pl.BlockSpec((1, w, nd, d), lambda bi, j: (bi, 0, j, 0))
    out = pl.pallas_call(
        functools.partial(_fourier_cols_kernel, alpha=alpha),
        grid=(batch, w // nd),
        in_specs=[pl.BlockSpec((1, 2, nd, w, d), lambda bi, j: (bi, 0, j, 0, 0)), hblk,
                  _const_spec(k2.shape), w_bf.spec(),
                  mod.spec(layer, 2, lambda bi, j: bi), vec, vec] + mlp.specs,
        out_specs=hblk,
        out_shape=jax.ShapeDtypeStruct((batch, w, w, d), F32),
        compiler_params=_params(2),
        name="fourier_cols_mlp",
    )(z, h4, k2, w_bf.array, mod.rows, ln_g.reshape(1, d), ln_b.reshape(1, d), *mlp.args)
    return out.reshape(batch * w * w, d)


def _fourier_dense_kernel(z_ref, cs_ref, o_ref, *, norm):
    zc = jnp.concatenate([z_ref[0, 0], z_ref[0, 1]], axis=0)
    o_ref[0] = (jnp.dot(cs_ref[...], zc, preferred_element_type=F32) * norm).astype(BF16)


def _fourier_dense(z, cs, norm):
    b, _, seq, d = z.shape
    return pl.pallas_call(
        functools.partial(_fourier_dense_kernel, norm=norm),
        grid=(b,),
        in_specs=[pl.BlockSpec((1, 2, seq, d), lambda bi: (bi, 0, 0, 0)), _const_spec(cs.shape)],
        out_specs=pl.BlockSpec((1, seq, d), lambda bi: (bi, 0, 0)),
        out_shape=jax.ShapeDtypeStruct((b, seq, d), BF16),
        compiler_params=_params(1),
        name="fourier_dense",
    )(z, cs)


def _proj_out_kernel(h_ref, a_ref, w_ref, gate_ref, lg_ref, lb_ref, o_ref, *, alpha):
    y = jnp.dot(a_ref[...], w_ref[...], preferred_element_type=F32)
    o_ref[...] = _post_norm(h_ref[...], y, gate_ref, lg_ref, lb_ref, alpha)


def _proj_out(h, a, w_bf, mod, layer, row_of_step, ln_g, ln_b, tm, alpha):
    t, d = h.shape
    tok = lambda i: (i, 0)
    vec = pl.BlockSpec((1, d), lambda i: (0, 0))
    return pl.pallas_call(
        functools.partial(_proj_out_kernel, alpha=alpha),
        grid=(t // tm,),
        in_specs=[pl.BlockSpec((tm, d), tok), pl.BlockSpec((tm, a.shape[1]), tok),
                  w_bf.spec(), mod.spec(layer, 2, row_of_step), vec, vec],
        out_specs=pl.BlockSpec((tm, d), tok),
        out_shape=jax.ShapeDtypeStruct((t, d), F32),
        compiler_params=_params(1),
        name="proj_out",
    )(h, a, w_bf.array, mod.rows, ln_g.reshape(1, d), ln_b.reshape(1, d))


def _dft_cos_sin(n, denom):
    idx = np.arange(n, dtype=np.int64)
    ang = ((idx[:, None] * idx[None, :]) % denom) * (2.0 * math.pi / denom)
    return np.cos(ang), np.sin(ang)


def _bf16_const(a):
    return jnp.asarray(np.asarray(a, np.float32)).astype(BF16)


def _rope_tables(seq):
    half = DIFF_QK_DIM // 4
    t = np.arange(seq)
    inv = ROPE_BASE ** (-np.arange(half, dtype=np.float64) / half)
    lane = np.arange(LANES)
    use_col = (lane % DIFF_QK_DIM) >= DIFF_QK_DIM // 2
    second = (lane % (2 * half)) >= half
    pos = np.where(use_col[None, :], (t % GRID_W)[:, None], (t // GRID_W)[:, None])
    ang = pos * inv[lane % half][None, :]
    cos, sin = np.cos(ang), np.sin(ang)
    tabs = (cos, np.where(second[None, :], 0.0, -sin), np.where(second[None, :], sin, 0.0))
    return tuple(jnp.asarray(a, F32) for a in tabs)


def _fourier_tables(seq, norm):
    w, nb, nd = GRID_W, FOURIER_SLAB, FOURIER_OUT_SLAB
    idx = np.arange(w, dtype=np.int64)
    slab = np.arange(w // nb, dtype=np.int64)
    col = np.arange(nb, dtype=np.int64)
    t = idx[None, None, None, :] * w + slab[:, None, None, None] * nb + col[None, None, :, None]
    ang = ((idx[None, :, None, None] * t) % seq) * (2.0 * math.pi / seq)
    trig = np.stack([np.cos(ang), -np.sin(ang)], axis=1)
    k1 = trig[..., None] * np.eye(nb)[None, None, None, :, None, :]
    c64, s64 = _dft_cos_sin(w, w)
    cs = np.stack([c64, s64], axis=1) * norm
    k2 = cs[:, None, :, None, :] * np.eye(nd)[None, :, None, :, None]
    return (_bf16_const(k1.reshape(w // nb, 2 * w * nb, w * nb)),
            _bf16_const(k2.reshape(w * nd, 2 * nd * w)))


def _no_rope_tables(seq):
    return jnp.ones((seq, LANES), F32), jnp.zeros((seq, LANES), F32), jnp.zeros((seq, LANES), F32)


def kernel(x, c, ctx, c_ctx, ada_w, ada_b, ln_g, ln_b, mlp_w1, mlp_w2, w_in, conv_w, lambda_qk,
           subln_g, w_out_mix, w_out_fourier):
    batch, seq, d = x.shape
    ctx_len = ctx.shape[1]
    depth = ada_w.shape[0]
    alpha = (2.0 * depth) ** 0.25
    last_attn = 2 * ((depth - 1) // 2)
    tm = 512
    assert seq == GRID_W * GRID_W and seq % tm == 0 and batch < SUBLANES and ctx_len % SUBLANES == 0
    assert depth % 2 == 0

    cond8 = jnp.zeros((SUBLANES, d), F32).at[:batch].set(c).at[batch].set(c_ctx)
    mod = _Mod(_modulation(cond8, ada_w, ada_b), d)
    lat_row = lambda i: i // (seq // tm)
    ctx_row = lambda i: batch

    w_in_first = w_in[0:1].astype(BF16)

    rope = _rope_tables(seq)
    no_rope = _no_rope_tables(ctx_len)

    gc, gs = _dft_cos_sin(FOURIER_GROUP, FOURIER_GROUP)
    gcs = _bf16_const(np.concatenate([gc, -gs], axis=1))
    gmat = _bf16_const(np.block([[gc, -gs], [gs, gc]]))
    k1, k2 = _fourier_tables(seq, (seq * FOURIER_GROUP) ** -0.5)
    cc, sc_ = _dft_cos_sin(ctx_len, ctx_len)
    cs_ctx = _bf16_const(np.concatenate([cc, sc_], axis=1))

    h = x.reshape(batch * seq, d)
    hc = ctx.reshape(batch * ctx_len, d)

    for i in range(depth):
        j = i // 2
        update_ctx = i < last_attn
        if i % 2 == 0:
            lam_init = 0.8 - 0.6 * math.exp(-0.3 * i)
            w_in_j = _Stacked(w_in_first, 0) if j == 0 else _Stacked(w_in_next, 0)
            mix_row = lambda t: t // (seq // (2 * tm))
            gb, g, q, k, v = _inproj(h, mod, i, mix_row, w_in_j, rope, 2 * tm, tm, seq)
            gbc, gcx, qc, kc, vc = _inproj(hc, mod, i, ctx_row, w_in_j, no_rope, ctx_len, ctx_len, ctx_len)
            casts = [(mlp_w1, i, 2), (mlp_w2, i, 2), (w_out_mix, j, 1), (w_out_fourier, j, 1)]
            if j + 1 < w_in.shape[0]:
                casts.append((w_in, j + 1, 1))
            attn, w1_pair, w2_pair, w_mix_one, w_four_one, *rest = _attention_stream(
                lambda_qk[j], q, [k, kc], [v, vc], subln_g[j], lam_init, 1024, 256, casts)
            w_in_next = rest[0] if rest else None
            w_mix_j = _Stacked(w_mix_one, 0)
            w1, w2 = _Stacked(w1_pair, 0), _Stacked(w2_pair, 0)
            mlp_ctx = _MlpTail(w1, w2, mod, i, ctx_row, ln_g[i, 1], ln_b[i, 1])
            mlp_lat = _MlpTail(w1, w2, mod, i, mix_row, ln_g[i, 1], ln_b[i, 1])
            h = _mix_out(h, gb, g, attn, w_mix_j, conv_w[j], mod, i, mix_row,
                         ln_g[i, 0], ln_b[i, 0], mlp_lat, 2 * tm, tm, seq, alpha)
            if update_ctx:
                attn_c = _attention(lambda_qk[j], qc, [kc], [vc], subln_g[j], lam_init, ctx_len, ctx_len)
                hc = _mix_out(hc, gbc, gcx, attn_c, w_mix_j, conv_w[j], mod, i, ctx_row,
                              ln_g[i, 0], ln_b[i, 0], mlp_ctx, ctx_len, ctx_len, ctx_len, alpha)
        else:
            w_four_j = _Stacked(w_four_one, 0)
            w1, w2 = _Stacked(w1_pair, 1), _Stacked(w2_pair, 1)
            mlp_ctx = _MlpTail(w1, w2, mod, i, ctx_row, ln_g[i, 1], ln_b[i, 1])
            z = _fourier_rows(h, mod, i, k1, gmat, batch)
            mlp_slab = _MlpTail(w1, w2, mod, i, lambda bi, sl: bi, ln_g[i, 1], ln_b[i, 1])
            h = _fourier_cols(z, h, k2, w_four_j, mod, i, ln_g[i, 0], ln_b[i, 0], mlp_slab, alpha)
            if update_ctx:
                zc = _fourier_in(hc, mod, i, ctx_row, gcs, ctx_len, ctx_len)
                fc = _fourier_dense(zc, cs_ctx, (ctx_len * FOURIER_GROUP) ** -0.5)
                hc1 = _proj_out(hc, fc.reshape(batch * ctx_len, d), w_four_j, mod, i, ctx_row,
                                ln_g[i, 0], ln_b[i, 0], batch * ctx_len, alpha)
                hc = _mlp(hc1, mlp_ctx, batch * ctx_len, alpha)
    return h.reshape(batch, seq, d)
```
